```python
import math
import jax
import jax.numpy as jnp
from jax import lax
import numpy as np

D_MODEL = 1024
BATCH = 2
SEQ = 8192
DEPTH = 2

HEAD_DIM = 64
D_MIX = D_MODEL
M_HEADS = 4
M_CHUNK = 64
G_HEADS = 4
G_CHUNK = 64
CONV_K = 4
N_HEADS = 8
N_KV_HEADS = 2
N_GROUP = N_HEADS // N_KV_HEADS
CMP_BLOCK = 32
CMP_STRIDE = 16
CMP_HIDDEN = 256
SLC_BLOCK = 64
N_SELECTED = 16
WINDOW = 512
Q_BLOCK = 128
ROPE_THETA = 10000.0
D_FF = 2816
EPS = 1e-6
NEG = -1e30
BIG = 1e30

M_WIDTH = M_HEADS * HEAD_DIM
G_WIDTH = G_HEADS * HEAD_DIM
N_WIDTH = N_HEADS * HEAD_DIM
KV_WIDTH = N_KV_HEADS * HEAD_DIM
M_IN = 4 * M_WIDTH + 2 * M_HEADS
G_IN = 4 * G_WIDTH + 2 * G_HEADS
N_IN = N_WIDTH + 6 * KV_WIDTH + 3 * N_HEADS
IN_WIDTH = M_IN + G_IN + N_IN

kernel_name = 'hybrid_mlstm_gdn_nsa_macaron'


def rms_norm(x, w):
    xf = x.astype(jnp.float32)
    y = xf * lax.rsqrt(jnp.mean(xf * xf, axis=-1, keepdims=True) + EPS)
    return y * w.astype(jnp.float32)


def l2_normalize(x):
    return x * lax.rsqrt(jnp.sum(x * x, axis=-1, keepdims=True) + EPS)


def rope(x, pos):
    half = HEAD_DIM // 2
    inv_freq = jnp.power(ROPE_THETA, -jnp.arange(half, dtype=jnp.float32) / half)
    ang = pos.astype(jnp.float32)[:, None] * inv_freq[None, :]
    cos, sin = jnp.cos(ang), jnp.sin(ang)
    x1, x2 = x[..., :half], x[..., half:]
    return jnp.concatenate([x1 * cos - x2 * sin, x1 * sin + x2 * cos], axis=-1)


def to_heads(x, n_heads):
    b, t, _ = x.shape
    return x.reshape(b, t, n_heads, HEAD_DIM).transpose(0, 2, 1, 3)


def from_heads(x):
    b, h, t, d = x.shape
    return x.transpose(0, 2, 1, 3).reshape(b, t, h * d)


def adaln(x, w, shift, scale):
    return (rms_norm(x, w) * (1.0 + scale) + shift).astype(x.dtype)


def swiglu(h, w_up, w_down):
    gate, up = jnp.split(h @ w_up, 2, axis=-1)
    return (jax.nn.silu(gate) * up) @ w_down


def mlstm_group(p, gate_b, norm_w):
    b, t, _ = p.shape
    nc = t // M_CHUNK
    w = M_WIDTH
    q = to_heads(p[..., 0:w], M_HEADS) * (HEAD_DIM ** -0.5)
    k = to_heads(p[..., w:2 * w], M_HEADS)
    v = to_heads(p[..., 2 * w:3 * w], M_HEADS)
    o_gate = jax.nn.sigmoid(p[..., 3 * w:4 * w])
    gates = p[..., 4 * w:] + gate_b.reshape(-1)
    log_i = gates[..., :M_HEADS].transpose(0, 2, 1)
    log_f = jax.nn.log_sigmoid(gates[..., M_HEADS:]).transpose(0, 2, 1)
    ch = lambda a: a.reshape(b, M_HEADS, nc, M_CHUNK, *a.shape[3:])
    q, k, v, log_i, log_f = ch(q), ch(k), ch(v), ch(log_i), ch(log_f)
    cum = jnp.cumsum(log_f, axis=-1)
    g_tot = cum[..., -1]
    a = g_tot[..., None] - cum + log_i
    m_loc = jnp.max(a, axis=-1)
    wgt = jnp.exp(a - m_loc[..., None])
    c_loc = jnp.einsum('bhcl,bhcld,bhcle->bhcde', wgt, k, v)
    n_loc = jnp.einsum('bhcl,bhcld->bhcd', wgt, k)

    def step(carry, inp):
        c_st, n_st, m_st = carry
        g_c, m_l, c_l, n_l = inp
        m_new = jnp.maximum(g_c + m_st, m_l)
        s_old = jnp.exp(g_c + m_st - m_new)
        s_new = jnp.exp(m_l - m_new)
        c_next = s_old[..., None, None] * c_st + s_new[..., None, None] * c_l
        n_next = s_old[..., None] * n_st + s_new[..., None] * n_l
        return (c_next, n_next, m_new), (c_st, n_st, m_st)

    init = (jnp.zeros((b, M_HEADS, HEAD_DIM, HEAD_DIM), jnp.float32),
            jnp.zeros((b, M_HEADS, HEAD_DIM), jnp.float32),
            jnp.zeros((b, M_HEADS), jnp.float32))
    xs = (jnp.moveaxis(g_tot, 2, 0), jnp.moveaxis(m_loc, 2, 0),
          jnp.moveaxis(c_loc, 2, 0), jnp.moveaxis(n_loc, 2, 0))
    _, (c_prev, n_prev, m_prev) = lax.scan(step, init, xs)
    c_prev = jnp.moveaxis(c_prev, 0, 2)
    n_prev = jnp.moveaxis(n_prev, 0, 2)
    m_prev = jnp.moveaxis(m_prev, 0, 2)

    causal = jnp.tril(jnp.ones((M_CHUNK, M_CHUNK), dtype=bool))
    d = jnp.where(causal, cum[..., :, None] - cum[..., None, :] + log_i[..., None, :], -jnp.inf)
    m_inter = cum + m_prev[..., None]
    m_t = jnp.maximum(m_inter, jnp.max(d, axis=-1))
    s = jnp.einsum('bhctd,bhcjd->bhctj', q, k) * jnp.exp(d - m_t[..., None])
    inter = jnp.exp(m_inter - m_t)
    num = (jnp.einsum('bhctj,bhcje->bhcte', s, v)
           + inter[..., None] * jnp.einsum('bhctd,bhcde->bhcte', q, c_prev))
    den = jnp.sum(s, axis=-1) + inter * jnp.einsum('bhctd,bhcd->bhct', q, n_prev)
    den = jnp.maximum(jnp.abs(den), jnp.exp(-m_t))
    h = (num / den[..., None]).reshape(b, M_HEADS, t, HEAD_DIM)
    return o_gate * from_heads(rms_norm(h, norm_w))


def causal_depthwise_conv(x, w):
    c = x.shape[-1]
    return lax.conv_general_dilated(x, w[:, None, :], window_strides=(1,), padding=[(CONV_K - 1, 0)],
                                    dimension_numbers=('NWC', 'WIO', 'NWC'), feature_group_count=c)


def gdn_group(p, conv_w, a_log, dt_bias, norm_w):
    b, t, _ = p.shape
    nc = t // G_CHUNK
    w = G_WIDTH
    qkv = jax.nn.silu(causal_depthwise_conv(p[..., :3 * w], conv_w.astype(jnp.float32)))
    q = l2_normalize(to_heads(qkv[..., :w], G_HEADS)) * (HEAD_DIM ** -0.5)
    k = l2_normalize(to_heads(qkv[..., w:2 * w], G_HEADS))
    v = to_heads(qkv[..., 2 * w:3 * w], G_HEADS)
    z = p[..., 3 * w:4 * w]
    a_pre = p[..., 4 * w:4 * w + G_HEADS].transpose(0, 2, 1)
    beta = jax.nn.sigmoid(p[..., 4 * w + G_HEADS:]).transpose(0, 2, 1)
    log_alpha = -jnp.exp(a_log.astype(jnp.float32))[None, :, None] * jax.nn.softplus(a_pre + dt_bias[None, :, None])
    ch = lambda a: a.reshape(b, G_HEADS, nc, G_CHUNK, *a.shape[3:])
    q, k, v, beta, log_alpha = ch(q), ch(k), ch(v), ch(beta), ch(log_alpha)
    gam = jnp.cumsum(log_alpha, axis=-1)
    idx = jnp.arange(G_CHUNK)
    strict = idx[:, None] > idx[None, :]
    incl = idx[:, None] >= idx[None, :]
    diff = gam[..., :, None] - gam[..., None, :]
    kk = jnp.einsum('bhctd,bhcjd->bhctj', k, k)
    a_mat = beta[..., None] * kk * jnp.exp(jnp.where(strict, diff, -jnp.inf))
    eye = jnp.eye(G_CHUNK, dtype=jnp.float32)
    rhs = jnp.concatenate([beta[..., None] * v, (beta * jnp.exp(gam))[..., None] * k], axis=-1)
    sol = lax.linalg.triangular_solve(eye + a_mat, rhs, left_side=True, lower=True, unit_diagonal=True)
    u0, w_mat = sol[..., :HEAD_DIM], sol[..., HEAD_DIM:]
    p_mat = jnp.einsum('bhctd,bhcjd->bhctj', q, k) * jnp.exp(jnp.where(incl, diff, -jnp.inf))
    q_g = q * jnp.exp(gam)[..., None]
    k_d = k * jnp.exp(gam[..., -1:] - gam)[..., None]
    g_last = jnp.exp(gam[..., -1])

    def step(s, inp):
        u0_c, w_c, qg_c, p_c, kd_c, gl_c = inp
        u = u0_c - jnp.einsum('bhld,bhde->bhle', w_c, s)
        o = jnp.einsum('bhld,bhde->bhle', qg_c, s) + jnp.einsum('bhlj,bhje->bhle', p_c, u)
        s_next = gl_c[..., None, None] * s + jnp.einsum('bhld,bhle->bhde', kd_c, u)
        return s_next, o

    xs = (jnp.moveaxis(u0, 2, 0), jnp.moveaxis(w_mat, 2, 0), jnp.moveaxis(q_g, 2, 0),
          jnp.moveaxis(p_mat, 2, 0), jnp.moveaxis(k_d, 2, 0), jnp.moveaxis(g_last, 2, 0))
    _, o = lax.scan(step, jnp.zeros((b, G_HEADS, HEAD_DIM, HEAD_DIM), jnp.float32), xs)
    o = jnp.moveaxis(o, 0, 2).reshape(b, G_HEADS, t, HEAD_DIM)
    return from_heads(rms_norm(o, norm_w)) * jax.nn.silu(z)


def compress(blocks, pos_emb, w1, w2):
    b, h, n = blocks.shape[:3]
    flat = (blocks + pos_emb).reshape(b, h, n, CMP_BLOCK * HEAD_DIM)
    return jax.nn.silu(flat @ w1) @ w2


def nsa_group(p, qk_norm, cmp_pos, cmp_w1, cmp_w2):
    b, t, _ = p.shape
    scale = HEAD_DIM ** -0.5
    pos = jnp.arange(t, dtype=jnp.int32)
    q = rope(rms_norm(to_heads(p[..., :N_WIDTH], N_HEADS), qk_norm[0]), pos)
    kv = p[..., N_WIDTH:N_WIDTH + 6 * KV_WIDTH]
    k_cmp, v_cmp, k_slc, v_slc, k_win, v_win = [
        to_heads(kv[..., i * KV_WIDTH:(i + 1) * KV_WIDTH], N_KV_HEADS) for i in range(6)]
    gates = jax.nn.sigmoid(p[..., N_WIDTH + 6 * KV_WIDTH:])
    k_slc = rope(rms_norm(k_slc, qk_norm[2]), pos)
    k_win = rope(rms_norm(k_win, qk_norm[3]), pos)

    n_cmp = (t - CMP_BLOCK) // CMP_STRIDE + 1
    cmp_start = jnp.arange(n_cmp) * CMP_STRIDE
    cmp_idx = cmp_start[:, None] + jnp.arange(CMP_BLOCK)[None, :]
    cmp_end = cmp_start + CMP_BLOCK - 1
    kc = compress(k_cmp[:, :, cmp_idx], cmp_pos[0], cmp_w1[0], cmp_w2[0])
    kc = rope(rms_norm(kc, qk_norm[1]), cmp_end)
    vc = compress(v_cmp[:, :, cmp_idx], cmp_pos[1], cmp_w1[1], cmp_w2[1])

    n_slc = t // SLC_BLOCK
    n_sel = min(N_SELECTED, n_slc)
    ks_blocks = k_slc.reshape(b, N_KV_HEADS, n_slc, SLC_BLOCK, HEAD_DIM)
    vs_blocks = v_slc.reshape(b, N_KV_HEADS, n_slc, SLC_BLOCK, HEAD_DIM)
    slc_start = jnp.arange(n_slc) * SLC_BLOCK
    overlap = jnp.maximum(jnp.minimum(cmp_start[:, None] + CMP_BLOCK, slc_start[None, :] + SLC_BLOCK)
                          - jnp.maximum(cmp_start[:, None], slc_start[None, :]), 0).astype(jnp.float32) / CMP_BLOCK

    kw_pad = jnp.pad(k_win, ((0, 0), (0, 0), (WINDOW, 0), (0, 0)))
    vw_pad = jnp.pad(v_win, ((0, 0), (0, 0), (WINDOW, 0), (0, 0)))

    n_qb = t // Q_BLOCK
    q_blocks = jnp.moveaxis(q.reshape(b, N_KV_HEADS, N_GROUP, n_qb, Q_BLOCK, HEAD_DIM), 3, 0)
    g_blocks = gates.reshape(b, t, 3, N_KV_HEADS, N_GROUP).transpose(0, 3, 4, 1, 2)
    g_blocks = jnp.moveaxis(g_blocks.reshape(b, N_KV_HEADS, N_GROUP, n_qb, Q_BLOCK, 3), 3, 0)
    starts = jnp.arange(n_qb, dtype=jnp.int32) * Q_BLOCK
    bi = jnp.arange(b)[:, None, None, None]
    hi = jnp.arange(N_KV_HEADS)[None, :, None, None]
    blk = jnp.arange(n_slc)
    tok = jnp.arange(SLC_BLOCK)
    win_off = jnp.arange(Q_BLOCK + WINDOW)

    def attend_block(args):
        qb, gb, start = args
        tq = start + jnp.arange(Q_BLOCK)
        s_c = jnp.einsum('bhgqd,bhnd->bhgqn', qb, kc) * scale
        valid_c = cmp_end[None, :] <= tq[:, None]
        p_c = jax.nn.softmax(jnp.where(valid_c, s_c, NEG), axis=-1)
        p_c = p_c * jnp.any(valid_c, axis=-1)[:, None].astype(jnp.float32)
        o_c = jnp.einsum('bhgqn,bhnd->bhgqd', p_c, vc)
        imp = jnp.einsum('bhgqn,ns->bhqs', p_c, overlap)
        cur = tq // SLC_BLOCK
        forced = (blk[None, :] == 0) | (blk[None, :] == cur[:, None]) | (blk[None, :] == cur[:, None] - 1)
        imp = jnp.where(forced, BIG, imp)
        imp = jnp.where(slc_start[None, :] <= tq[:, None], imp, NEG)
        _, sel = lax.top_k(imp, n_sel)
        ks = ks_blocks[bi, hi, sel]
        vs = vs_blocks[bi, hi, sel]
        s_s = jnp.einsum('bhgqd,bhqnpd->bhgqnp', qb, ks) * scale
        tok_pos = sel[..., None] * SLC_BLOCK + tok
        valid_s = (tok_pos <= tq[:, None, None])[:, :, None]
        s_s = jnp.where(valid_s, s_s, NEG).reshape(b, N_KV_HEADS, N_GROUP, Q_BLOCK, n_sel * SLC_BLOCK)
        p_s = jax.nn.softmax(s_s, axis=-1).reshape(b, N_KV_HEADS, N_GROUP, Q_BLOCK, n_sel, SLC_BLOCK)
        o_s = jnp.einsum('bhgqnp,bhqnpd->bhgqd', p_s, vs)
        kw = lax.dynamic_slice_in_dim(kw_pad, start, Q_BLOCK + WINDOW, axis=2)
        vw = lax.dynamic_slice_in_dim(vw_pad, start, Q_BLOCK + WINDOW, axis=2)
        kpos = start - WINDOW + win_off
        dist = tq[:, None] - kpos[None, :]
        valid_w = (dist >= 0) & (dist < WINDOW) & (kpos[None, :] >= 0)
        s_w = jnp.einsum('bhgqd,bhkd->bhgqk', qb, kw) * scale
        p_w = jax.nn.softmax(jnp.where(valid_w, s_w, NEG), axis=-1)
        o_w = jnp.einsum('bhgqk,bhkd->bhgqd', p_w, vw)
        return gb[..., 0:1] * o_c + gb[..., 1:2] * o_s + gb[..., 2:3] * o_w

    out = lax.map(attend_block, (q_blocks, g_blocks, starts))
    out = jnp.moveaxis(out, 0, 3).reshape(b, N_HEADS, t, HEAD_DIM)
    return from_heads(out)


def hybrid_mixer(h, w_in, w_out, m_gate_b, m_norm_w, g_conv_w, g_a_log, g_dt_bias, g_norm_w,
                 n_qk_norm, n_cmp_pos, n_cmp_w1, n_cmp_w2):
    proj = (h @ w_in).astype(jnp.float32)
    y_m = mlstm_group(proj[..., :M_IN], m_gate_b, m_norm_w)
    y_g = gdn_group(proj[..., M_IN:M_IN + G_IN], g_conv_w, g_a_log, g_dt_bias, g_norm_w)
    y_n = nsa_group(proj[..., M_IN + G_IN:], n_qk_norm, n_cmp_pos, n_cmp_w1, n_cmp_w2)
    y = jnp.concatenate([y_m, y_g, y_n], axis=-1).astype(h.dtype)
    return y @ w_out


def setup_inputs(seed: int = 0) -> dict:
    key = jax.random.key(seed)
    ks = jax.random.split(key, 20)
    f32 = jnp.float32
    nrm = lambda k, shape, s: jax.random.normal(k, shape, f32) * s
    x = nrm(ks[0], (BATCH, SEQ, D_MODEL), 1.0)
    c = nrm(ks[1], (BATCH, D_MODEL), 1.0)
    ada_w = nrm(ks[2], (DEPTH, D_MODEL, 9 * D_MODEL), 0.5 * D_MODEL ** -0.5)
    ada_b = nrm(ks[3], (DEPTH, 9 * D_MODEL), 0.01)
    norm_w = 1.0 + nrm(ks[4], (DEPTH, 3, D_MODEL), 0.02)
    ffn_w_up = nrm(ks[5], (DEPTH, 2, D_MODEL, 2 * D_FF), D_MODEL ** -0.5)
    ffn_w_down = nrm(ks[6], (DEPTH, 2, D_FF, D_MODEL), D_FF ** -0.5)
    w_in = nrm(ks[7], (DEPTH, D_MODEL, IN_WIDTH), D_MODEL ** -0.5)
    w_out = nrm(ks[8], (DEPTH, D_MIX, D_MODEL), D_MIX ** -0.5)
    i_b = nrm(ks[9], (DEPTH, 1, M_HEADS), 0.01)
    f_b = 3.0 + nrm(ks[10], (DEPTH, 1, M_HEADS), 0.1)
    mlstm_gate_b = jnp.concatenate([i_b, f_b], axis=1)
    mlstm_norm_w = 1.0 + nrm(ks[11], (DEPTH, HEAD_DIM), 0.02)
    gdn_conv_w = nrm(ks[12], (DEPTH, CONV_K, 3 * G_WIDTH), CONV_K ** -0.5)
    gdn_a_log = jnp.log(jax.random.uniform(ks[13], (DEPTH, G_HEADS), f32, 1.0, 16.0))
    dt = jnp.exp(jax.random.uniform(ks[14], (DEPTH, G_HEADS), f32, math.log(1e-3), math.log(1e-1)))
    gdn_dt_bias = dt + jnp.log(-jnp.expm1(-dt))
    gdn_norm_w = 1.0 + nrm(ks[15], (DEPTH, HEAD_DIM), 0.02)
    nsa_qk_norm = 1.0 + nrm(ks[16], (DEPTH, 4, HEAD_DIM), 0.02)
    nsa_cmp_pos = nrm(ks[17], (DEPTH, 2, CMP_BLOCK, HEAD_DIM), 0.02)
    nsa_cmp_w1 = nrm(ks[18], (DEPTH, 2, CMP_BLOCK * HEAD_DIM, CMP_HIDDEN), (CMP_BLOCK * HEAD_DIM) ** -0.5)
    nsa_cmp_w2 = nrm(ks[19], (DEPTH, 2, CMP_HIDDEN, HEAD_DIM), CMP_HIDDEN ** -0.5)
    return {'x': x, 'c': c, 'ada_w': ada_w, 'ada_b': ada_b, 'norm_w': norm_w,
            'ffn_w_up': ffn_w_up, 'ffn_w_down': ffn_w_down, 'w_in': w_in, 'w_out': w_out,
            'mlstm_gate_b': mlstm_gate_b, 'mlstm_norm_w': mlstm_norm_w,
            'gdn_conv_w': gdn_conv_w, 'gdn_a_log': gdn_a_log, 'gdn_dt_bias': gdn_dt_bias,
            'gdn_norm_w': gdn_norm_w, 'nsa_qk_norm': nsa_qk_norm, 'nsa_cmp_pos': nsa_cmp_pos,
            'nsa_cmp_w1': nsa_cmp_w1, 'nsa_cmp_w2': nsa_cmp_w2}


def reference(x, c, ada_w, ada_b, norm_w, ffn_w_up, ffn_w_down, w_in, w_out,
              mlstm_gate_b, mlstm_norm_w, gdn_conv_w, gdn_a_log, gdn_dt_bias, gdn_norm_w,
              nsa_qk_norm, nsa_cmp_pos, nsa_cmp_w1, nsa_cmp_w2):
    b = x.shape[0]
    c_act = jax.nn.silu(c)
    for l in range(DEPTH):
        mod = (c_act @ ada_w[l] + ada_b[l]).reshape(b, 9, 1, D_MODEL)
        h = adaln(x, norm_w[l, 0], mod[:, 0], mod[:, 1])
        x = x + 0.5 * mod[:, 2] * swiglu(h, ffn_w_up[l, 0], ffn_w_down[l, 0])
        h = adaln(x, norm_w[l, 1], mod[:, 3], mod[:, 4])
        x = x + mod[:, 5] * hybrid_mixer(h, w_in[l], w_out[l], mlstm_gate_b[l], mlstm_norm_w[l],
                                         gdn_conv_w[l], gdn_a_log[l], gdn_dt_bias[l], gdn_norm_w[l],
                                         nsa_qk_norm[l], nsa_cmp_pos[l], nsa_cmp_w1[l], nsa_cmp_w2[l])
        h = adaln(x, norm_w[l, 2], mod[:, 6], mod[:, 7])
        x = x + 0.5 * mod[:, 8] * swiglu(h, ffn_w_up[l, 1], ffn_w_down[l, 1])
    return x
```

```python
import functools

import jax
import jax.numpy as jnp
from jax import lax
from jax.experimental import pallas as pl
from jax.experimental.pallas import tpu as pltpu

F32 = jnp.float32
BF16 = jnp.bfloat16

D_MODEL = 1024
HEAD_DIM = 64
D_FF = 2816
EPS = 1e-6
NEG = -1e30
BIG = 1e30

M_HEADS = 4
G_HEADS = 4
N_HEADS = 8
N_KV_HEADS = 2
N_GROUP = N_HEADS // N_KV_HEADS
CHUNK = 64
CONV_K = 4
CMP_BLOCK = 32
CMP_STRIDE = 16
CMP_HIDDEN = 256
SLC_BLOCK = 64
N_SELECTED = 16
WINDOW = 512
Q_BLOCK = 128
ROPE_THETA = 10000.0

M_WIDTH = M_HEADS * HEAD_DIM
G_WIDTH = G_HEADS * HEAD_DIM
N_WIDTH = N_HEADS * HEAD_DIM
KV_WIDTH = N_KV_HEADS * HEAD_DIM
M_IN = 4 * M_WIDTH + 2 * M_HEADS
G_IN = 4 * G_WIDTH + 2 * G_HEADS
N_IN = N_WIDTH + 6 * KV_WIDTH + 3 * N_HEADS

LANES = 128
FF_TILE = 256
VMEM_LIMIT = 56 * 1024 * 1024

GT_M = 0
GT_G = 8
GT_N = 16


def _params(*sem):
    return pltpu.CompilerParams(dimension_semantics=sem, vmem_limit_bytes=VMEM_LIMIT)


def _dot(a, b):
    return jnp.dot(a, b, preferred_element_type=F32)


def _dot_nt(a, b):
    return lax.dot_general(a, b, (((1,), (1,)), ((), ())), preferred_element_type=F32)


def _dot_tn(a, b):
    return lax.dot_general(a, b, (((0,), (0,)), ((), ())), preferred_element_type=F32)


def _split_bf16(x, n):
    parts, r = [], x
    for _ in range(n):
        p = r.astype(BF16)
        parts.append(p)
        r = r - p.astype(F32)
    return parts


def _dot01_l(m01, x, n=3):
    acc = None
    for p in _split_bf16(x, n):
        t = _dot(m01, p)
        acc = t if acc is None else acc + t
    return acc


def _dot01_r(x, m01, n=3):
    acc = None
    for p in _split_bf16(x, n):
        t = _dot(p, m01)
        acc = t if acc is None else acc + t
    return acc


def _iota(shape, dim):
    return lax.broadcasted_iota(jnp.int32, shape, dim)


def _ind(cond, dtype=F32):
    return jnp.where(cond, 1.0, 0.0).astype(dtype)


def _softplus(x):
    return jnp.maximum(x, 0.0) + jnp.log1p(jnp.exp(-jnp.abs(x)))


def _group_ones(width):
    r, c = _iota((width, width), 0), _iota((width, width), 1)
    return _ind((r >> 6) == (c >> 6), BF16)


def _rope_lanes(x, cosf, sinf):
    lane = _iota(x.shape, 1)
    partner = jnp.where((lane & 63) < 32, pltpu.roll(x, 96, 1), pltpu.roll(x, 32, 1))
    return x * cosf + partner * sinf


def _mod_kernel(ct_ref, w_ref, b_ref, o_ref):
    ct = ct_ref[...]
    act = ct * jax.nn.sigmoid(ct)
    w = w_ref[0]
    rows = [jnp.sum(w * act[:, b:b + 1], axis=0, keepdims=True) for b in range(ct.shape[1])]
    o_ref[0] = jnp.concatenate(rows, axis=0) + b_ref[0]


def _modulation(c, ada_w, ada_b):
    depth, d, n = ada_w.shape
    b = c.shape[0]
    tn = 1024
    return pl.pallas_call(
        _mod_kernel,
        grid=(depth, n // tn),
        in_specs=[pl.BlockSpec((d, b), lambda l, j: (0, 0)),
                  pl.BlockSpec((1, d, tn), lambda l, j: (l, 0, j)),
                  pl.BlockSpec((1, 1, tn), lambda l, j: (l, 0, j))],
        out_specs=pl.BlockSpec((1, b, tn), lambda l, j: (l, 0, j)),
        out_shape=jax.ShapeDtypeStruct((depth, b, n), F32),
        compiler_params=_params("parallel", "parallel"),
        name="adaln_mod",
    )(c.T, ada_w, ada_b.reshape(depth, 1, n))


def _adaln(x, nw, shift, scale):
    y = x * lax.rsqrt(jnp.mean(x * x, axis=-1, keepdims=True) + EPS) * nw
    return y * (1.0 + scale) + shift


def _ffn_kernel(x_ref, mod_ref, nw_ref, wup_ref, wdn_ref, o_ref, *, row0, res_w):
    x = x_ref[0]
    shift = mod_ref[0, row0:row0 + 1, :]
    scale = mod_ref[0, row0 + 1:row0 + 2, :]
    gate = mod_ref[0, row0 + 2:row0 + 3, :]
    h = _adaln(x, nw_ref[...], shift, scale).astype(BF16)
    acc = jnp.zeros(x.shape, F32)
    for c in range(D_FF // FF_TILE):
        lo = c * FF_TILE
        g = _dot(h, wup_ref[:, lo:lo + FF_TILE])
        u = _dot(h, wup_ref[:, D_FF + lo:D_FF + lo + FF_TILE])
        a = (jax.nn.silu(g) * u).astype(BF16)
        acc = acc + _dot(a, wdn_ref[lo:lo + FF_TILE, :])
    o_ref[0] = x + (res_w * gate) * acc


def _ffn(x, mod, nw, w_up, w_down, row0):
    b, t, d = x.shape
    tm = min(512, t)
    kern = functools.partial(_ffn_kernel, row0=row0, res_w=0.5)
    return pl.pallas_call(
        kern,
        grid=(b, t // tm),
        in_specs=[pl.BlockSpec((1, tm, d), lambda i, j: (i, j, 0)),
                  pl.BlockSpec((1, 9, d), lambda i, j: (i, 0, 0)),
                  pl.BlockSpec((1, d), lambda i, j: (0, 0)),
                  pl.BlockSpec((d, 2 * D_FF), lambda i, j: (0, 0), pipeline_mode=pl.Buffered(1)),
                  pl.BlockSpec((D_FF, d), lambda i, j: (0, 0), pipeline_mode=pl.Buffered(1))],
        out_specs=pl.BlockSpec((1, tm, d), lambda i, j: (i, j, 0)),
        out_shape=jax.ShapeDtypeStruct(x.shape, F32),
        compiler_params=_params("parallel", "parallel"),
        name="ffn",
    )(x, mod, nw.reshape(1, d), w_up, w_down)


def _inproj_kernel(x_ref, mod_ref, nw_ref, w_ref, pm_ref, pg_ref, pq_ref, pkv_ref, gt_ref):
    x = x_ref[0]
    h = _adaln(x, nw_ref[...], mod_ref[0, 3:4, :], mod_ref[0, 4:5, :]).astype(BF16)
    off = 0
    for ref in (pm_ref, pg_ref, pq_ref, pkv_ref, gt_ref):
        wd = ref.shape[-1]
        ref[0] = _dot(h, w_ref[:, off:off + wd])
        off += wd


IN_PERM_WIDTH = 4 * M_WIDTH + 4 * G_WIDTH + N_WIDTH + 6 * KV_WIDTH + LANES


def _permute_w_in(w_in):
    g0, n0 = M_IN, M_IN + G_IN
    gates = jnp.concatenate([w_in[:, 4 * M_WIDTH:M_IN], w_in[:, g0 + 4 * G_WIDTH:g0 + G_IN],
                             w_in[:, n0 + N_WIDTH + 6 * KV_WIDTH:n0 + N_IN]], axis=1)
    gates = jnp.pad(gates, ((0, 0), (0, LANES - gates.shape[1])))
    return jnp.concatenate([w_in[:, :4 * M_WIDTH], w_in[:, g0:g0 + 4 * G_WIDTH],
                            w_in[:, n0:n0 + N_WIDTH + 6 * KV_WIDTH], gates], axis=1).astype(BF16)


def _in_proj(x, mod, nw, w_perm):
    b, t, d = x.shape
    tm = min(512, t)
    widths = (4 * M_WIDTH, 4 * G_WIDTH, N_WIDTH, 6 * KV_WIDTH, LANES)
    return pl.pallas_call(
        _inproj_kernel,
        grid=(b, t // tm),
        in_specs=[pl.BlockSpec((1, tm, d), lambda i, j: (i, j, 0)),
                  pl.BlockSpec((1, 9, d), lambda i, j: (i, 0, 0)),
                  pl.BlockSpec((1, d), lambda i, j: (0, 0)),
                  pl.BlockSpec((d, IN_PERM_WIDTH), lambda i, j: (0, 0), pipeline_mode=pl.Buffered(1))],
        out_specs=[pl.BlockSpec((1, tm, w), lambda i, j: (i, j, 0)) for w in widths],
        out_shape=[jax.ShapeDtypeStruct((b, t, w), F32) for w in widths],
        compiler_params=_params("parallel", "parallel"),
        name="mixer_in_proj",
    )(x, mod, nw.reshape(1, d), w_perm)


def _outproj_kernel(x_ref, mod_ref, ym_ref, yg_ref, yn_ref, w_ref, o_ref):
    y = jnp.concatenate([ym_ref[0], yg_ref[0], yn_ref[0]], axis=-1).astype(BF16)
    o_ref[0] = x_ref[0] + mod_ref[0, 5:6, :] * _dot(y, w_ref[...])


def _out_proj(x, mod, y_m, y_g, y_n, w_out):
    b, t, d = x.shape
    tm = min(512, t)
    row = lambda w: pl.BlockSpec((1, tm, w), lambda i, j: (i, j, 0))
    return pl.pallas_call(
        _outproj_kernel,
        grid=(b, t // tm),
        in_specs=[row(d), pl.BlockSpec((1, 9, d), lambda i, j: (i, 0, 0)),
                  row(M_WIDTH), row(G_WIDTH), row(N_WIDTH),
                  pl.BlockSpec((d, d), lambda i, j: (0, 0), pipeline_mode=pl.Buffered(1))],
        out_specs=row(d),
        out_shape=jax.ShapeDtypeStruct(x.shape, F32),
        compiler_params=_params("parallel", "parallel"),
        name="mixer_out_proj",
    )(x, mod, y_m, y_g, y_n, w_out.astype(BF16))


def _chunk_masks():
    r, c = _iota((CHUNK, CHUNK), 0), _iota((CHUNK, CHUNK), 1)
    return r, c


def _mlstm_kernel(pm_ref, gt_ref, gb_ref, nw_ref, o_ref, c_scr, n_scr, m_scr, *, n_chunks):
    @pl.when(pl.program_id(1) == 0)
    def _():
        c_scr[...] = jnp.zeros(c_scr.shape, F32)
        n_scr[...] = jnp.zeros(n_scr.shape, F32)
        m_scr[...] = jnp.zeros(m_scr.shape, F32)

    r, c = _chunk_masks()
    causal = r >= c
    low = _ind(causal, BF16)
    low_ones = jnp.concatenate([low, jnp.ones((CHUNK, CHUNK), BF16)], axis=1)
    strict_up = _ind(r > c)
    eye = _ind(r == c)
    nw = nw_ref[...]
    gb = gb_ref[...]

    def chunk(ci, carry):
        r0 = pl.multiple_of(ci * CHUNK, CHUNK)
        g = gt_ref[0, pl.ds(r0, CHUNK), :] + gb
        lf = -_softplus(-g)
        cum = _dot01_l(low, lf)
        for h in range(M_HEADS):
            sl = lambda k: pm_ref[0, pl.ds(r0, CHUNK), k * M_WIDTH + h * HEAD_DIM:k * M_WIDTH + (h + 1) * HEAD_DIM]
            q = sl(0) * (HEAD_DIM ** -0.5)
            k = sl(1)
            v = sl(2)
            og = jax.nn.sigmoid(sl(3))
            li_c = g[:, GT_M + h:GT_M + h + 1]
            lf_c = lf[:, GT_M + M_HEADS + h:GT_M + M_HEADS + h + 1]
            cum_c = cum[:, GT_M + M_HEADS + h:GT_M + M_HEADS + h + 1]
            c_prev, n_prev, m_prev = c_scr[h], n_scr[h], m_scr[h]
            dmat = _dot01_l(low_ones, jnp.concatenate([lf_c * strict_up, li_c * eye], axis=0))
            dmat = jnp.where(causal, dmat, NEG)
            m_inter = cum_c + m_prev
            m_t = jnp.maximum(m_inter, jnp.max(dmat, axis=-1, keepdims=True))
            s = _dot_nt(q.astype(BF16), k.astype(BF16)) * jnp.exp(dmat - m_t)
            inter = jnp.exp(m_inter - m_t)
            num = _dot(s.astype(BF16), v.astype(BF16)) + inter * _dot(q.astype(BF16), c_prev.astype(BF16))
            den = jnp.sum(s, axis=-1, keepdims=True) + inter * jnp.sum(q * n_prev, axis=-1, keepdims=True)
            den = jnp.maximum(jnp.abs(den), jnp.exp(-m_t))
            hh = num / den
            hh = hh * lax.rsqrt(jnp.mean(hh * hh, axis=-1, keepdims=True) + EPS) * nw
            o_ref[0, pl.ds(r0, CHUNK), h * HEAD_DIM:(h + 1) * HEAD_DIM] = og * hh
            g_tot = cum_c[CHUNK - 1:CHUNK, :]
            a = g_tot - cum_c + li_c
            m_loc = jnp.max(a, axis=0, keepdims=True)
            kw = k * jnp.exp(a - m_loc)
            c_loc = _dot_tn(kw.astype(BF16), v.astype(BF16))
            n_loc = jnp.sum(kw, axis=0, keepdims=True)
            m_new = jnp.maximum(g_tot + m_prev, m_loc)
            s_old = jnp.exp(g_tot + m_prev - m_new)
            s_new = jnp.exp(m_loc - m_new)
            c_scr[h] = s_old * c_prev + s_new * c_loc
            n_scr[h] = s_old * n_prev + s_new * n_loc
            m_scr[h] = m_new
        return carry

    lax.fori_loop(0, n_chunks, chunk, 0)


def _mlstm(pm, gt, gate_b, norm_w):
    b, t, _ = pm.shape
    tc = min(512, t)
    gb = jnp.zeros((1, LANES), F32).at[0, GT_M:GT_M + 2 * M_HEADS].set(gate_b.reshape(-1))
    kern = functools.partial(_mlstm_kernel, n_chunks=tc // CHUNK)
    return pl.pallas_call(
        kern,
        grid=(b, t // tc),
        in_specs=[pl.BlockSpec((1, tc, 4 * M_WIDTH), lambda i, j: (i, j, 0)),
                  pl.BlockSpec((1, tc, LANES), lambda i, j: (i, j, 0)),
                  pl.BlockSpec((1, LANES), lambda i, j: (0, 0)),
                  pl.BlockSpec((1, HEAD_DIM), lambda i, j: (0, 0))],
        out_specs=pl.BlockSpec((1, tc, M_WIDTH), lambda i, j: (i, j, 0)),
        out_shape=jax.ShapeDtypeStruct((b, t, M_WIDTH), F32),
        scratch_shapes=[pltpu.VMEM((M_HEADS, HEAD_DIM, HEAD_DIM), F32),
                        pltpu.VMEM((M_HEADS, 1, HEAD_DIM), F32),
                        pltpu.VMEM((M_HEADS, 1, 1), F32)],
        compiler_params=_params("parallel", "arbitrary"),
        name="mlstm",
    )(pm, gt, gb, norm_w.reshape(1, HEAD_DIM))


def _gdn_kernel(pg_ref, gt_ref, cw_ref, hp_ref, nw_ref, o_ref, xbuf, qkv_scr, la_scr, s_scr, *, tc):
    first = pl.program_id(1) == 0

    @pl.when(first)
    def _():
        s_scr[...] = jnp.zeros(s_scr.shape, F32)
        xbuf[0:8, :] = jnp.zeros((8, 3 * G_WIDTH), F32)

    @pl.when(jnp.logical_not(first))
    def _():
        xbuf[0:8, :] = xbuf[tc:tc + 8, :]

    xbuf[8:8 + tc, :] = pg_ref[0, :, 0:3 * G_WIDTH]
    acc = None
    for kk in range(CONV_K):
        term = xbuf[8 - (CONV_K - 1) + kk:8 - (CONV_K - 1) + kk + tc, :] * cw_ref[kk:kk + 1, :]
        acc = term if acc is None else acc + term
    act = acc * jax.nn.sigmoid(acc)
    gones = _group_ones(G_WIDTH)
    for part in range(2):
        xx = act[:, part * G_WIDTH:(part + 1) * G_WIDTH]
        ssq = _dot01_r(xx * xx, gones, 2)
        xx = xx * lax.rsqrt(ssq + EPS)
        if part == 0:
            xx = xx * (HEAD_DIM ** -0.5)
        qkv_scr[:, part * G_WIDTH:(part + 1) * G_WIDTH] = xx
    qkv_scr[:, 2 * G_WIDTH:3 * G_WIDTH] = act[:, 2 * G_WIDTH:3 * G_WIDTH]
    gate = gt_ref[0]
    neg_rate = -jnp.exp(hp_ref[0:1, :])
    la_scr[...] = neg_rate * _softplus(gate + hp_ref[1:2, :])

    r, c = _chunk_masks()
    low = _ind(r >= c, BF16)
    strict_low = r > c
    incl_low = r >= c
    strict_up = _ind(r > c)
    eye = _ind(r == c)
    nw = nw_ref[...]

    def chunk(ci, carry):
        r0 = pl.multiple_of(ci * CHUNK, CHUNK)
        la = la_scr[pl.ds(r0, CHUNK), :]
        gam = _dot01_l(low, la)
        beta_all = jax.nn.sigmoid(gt_ref[0, pl.ds(r0, CHUNK), :])
        for h in range(G_HEADS):
            sl = lambda k: qkv_scr[pl.ds(r0, CHUNK), k * G_WIDTH + h * HEAD_DIM:k * G_WIDTH + (h + 1) * HEAD_DIM]
            q, k, v = sl(0), sl(1), sl(2)
            z = pg_ref[0, pl.ds(r0, CHUNK), 3 * G_WIDTH + h * HEAD_DIM:3 * G_WIDTH + (h + 1) * HEAD_DIM]
            la_c = la[:, GT_G + h:GT_G + h + 1]
            gam_c = gam[:, GT_G + h:GT_G + h + 1]
            beta_c = beta_all[:, GT_G + G_HEADS + h:GT_G + G_HEADS + h + 1]
            diff = _dot01_l(low, la_c * strict_up)
            dec_strict = jnp.exp(jnp.where(strict_low, diff, NEG))
            dec_incl = jnp.exp(jnp.where(incl_low, diff, NEG))
            kb = k.astype(BF16)
            kk = _dot_nt(kb, kb)
            nmat = -(beta_c * kk * dec_strict)
            inv = eye + nmat
            pw = nmat
            for _ in range(5):
                pwb = pw.astype(BF16)
                pw = _dot(pwb, pwb)
                inv = inv + _dot(inv.astype(BF16), pw.astype(BF16))
            e_gam = jnp.exp(gam_c)
            rhs = jnp.concatenate([beta_c * v, (beta_c * e_gam) * k], axis=1)
            sol = _dot(inv.astype(BF16), rhs.astype(BF16))
            u0, w_mat = sol[:, :HEAD_DIM], sol[:, HEAD_DIM:]
            p_mat = _dot_nt(q.astype(BF16), kb) * dec_incl
            gam_last = gam_c[CHUNK - 1:CHUNK, :]
            q_g = q * e_gam
            k_d = k * jnp.exp(gam_last - gam_c)
            s_prev = s_scr[h]
            sb = s_prev.astype(BF16)
            u = u0 - _dot(w_mat.astype(BF16), sb)
            ub = u.astype(BF16)
            o = _dot(q_g.astype(BF16), sb) + _dot(p_mat.astype(BF16), ub)
            s_scr[h] = jnp.exp(gam_last) * s_prev + _dot_tn(k_d.astype(BF16), ub)
            o = o * lax.rsqrt(jnp.mean(o * o, axis=-1, keepdims=True) + EPS) * nw
            o_ref[0, pl.ds(r0, CHUNK), h * HEAD_DIM:(h + 1) * HEAD_DIM] = o * (z * jax.nn.sigmoid(z))
        return carry

    lax.fori_loop(0, tc // CHUNK, chunk, 0)


def _gdn(pg, gt, conv_w, a_log, dt_bias, norm_w):
    b, t, _ = pg.shape
    tc = min(512, t)
    hp = jnp.zeros((2, LANES), F32)
    hp = hp.at[0, GT_G:GT_G + G_HEADS].set(a_log).at[1, GT_G:GT_G + G_HEADS].set(dt_bias)
    kern = functools.partial(_gdn_kernel, tc=tc)
    return pl.pallas_call(
        kern,
        grid=(b, t // tc),
        in_specs=[pl.BlockSpec((1, tc, 4 * G_WIDTH), lambda i, j: (i, j, 0)),
                  pl.BlockSpec((1, tc, LANES), lambda i, j: (i, j, 0)),
                  pl.BlockSpec((CONV_K, 3 * G_WIDTH), lambda i, j: (0, 0)),
                  pl.BlockSpec((2, LANES), lambda i, j: (0, 0)),
                  pl.BlockSpec((1, HEAD_DIM), lambda i, j: (0, 0))],
        out_specs=pl.BlockSpec((1, tc, G_WIDTH), lambda i, j: (i, j, 0)),
        out_shape=jax.ShapeDtypeStruct((b, t, G_WIDTH), F32),
        scratch_shapes=[pltpu.VMEM((tc + 8, 3 * G_WIDTH), F32),
                        pltpu.VMEM((tc, 3 * G_WIDTH), F32),
                        pltpu.VMEM((tc, LANES), F32),
                        pltpu.VMEM((G_HEADS, HEAD_DIM, HEAD_DIM), F32)],
        compiler_params=_params("parallel", "arbitrary"),
        name="gdn",
    )(pg, gt, conv_w.astype(F32), hp, norm_w.reshape(1, HEAD_DIM))


def _rope_tables(pos):
    half = HEAD_DIM // 2
    inv_freq = jnp.power(ROPE_THETA, -jnp.arange(half, dtype=F32) / half)
    ang = pos.astype(F32)[:, None] * inv_freq[None, :]
    cos, sin = jnp.cos(ang), jnp.sin(ang)
    cosf = jnp.tile(cos, (1, LANES // half))
    sinf = jnp.tile(jnp.concatenate([-sin, sin], axis=1), (1, LANES // HEAD_DIM))
    return cosf, sinf


def _nsa_prep_kernel(pq_ref, pkv_ref, cos_ref, sin_ref, nw_ref, q_ref, ks_ref, vs_ref, kw_ref, vw_ref):
    cosf, sinf = cos_ref[...], sin_ref[...]
    gones = _group_ones(LANES)

    def norm_rope(x, w):
        ssq = _dot01_r(x * x, gones, 2)
        y = x * lax.rsqrt(ssq * (1.0 / HEAD_DIM) + EPS) * w
        return _rope_lanes(y, cosf, sinf)

    def put_heads(ref, slab, h0):
        ref[0, h0] = slab[:, :HEAD_DIM].astype(BF16)
        ref[0, h0 + 1] = slab[:, HEAD_DIM:].astype(BF16)

    for s in range(N_WIDTH // LANES):
        qs = norm_rope(pq_ref[0, :, s * LANES:(s + 1) * LANES], nw_ref[0:1, :]) * (HEAD_DIM ** -0.5)
        put_heads(q_ref, qs, 2 * s)
    kv = lambda i: pkv_ref[0, :, i * KV_WIDTH:(i + 1) * KV_WIDTH]
    put_heads(ks_ref, norm_rope(kv(2), nw_ref[2:3, :]), 0)
    put_heads(vs_ref, kv(3), 0)
    put_heads(kw_ref, norm_rope(kv(4), nw_ref[3:4, :]), 0)
    put_heads(vw_ref, kv(5), 0)


def _nsa_prep(pq, pkv, qk_norm):
    b, t, _ = pq.shape
    tm = min(512, t)
    cosf, sinf = _rope_tables(jnp.arange(t, dtype=jnp.int32))
    nw = jnp.tile(qk_norm, (1, LANES // HEAD_DIM))
    heads = lambda n: pl.BlockSpec((1, n, tm, HEAD_DIM), lambda i, j: (i, 0, j, 0))
    shp = lambda n: jax.ShapeDtypeStruct((b, n, t, HEAD_DIM), BF16)
    return pl.pallas_call(
        _nsa_prep_kernel,
        grid=(b, t // tm),
        in_specs=[pl.BlockSpec((1, tm, N_WIDTH), lambda i, j: (i, j, 0)),
                  pl.BlockSpec((1, tm, 6 * KV_WIDTH), lambda i, j: (i, j, 0)),
                  pl.BlockSpec((tm, LANES), lambda i, j: (j, 0)),
                  pl.BlockSpec((tm, LANES), lambda i, j: (j, 0)),
                  pl.BlockSpec((4, LANES), lambda i, j: (0, 0))],
        out_specs=[heads(N_HEADS)] + [heads(N_KV_HEADS)] * 4,
        out_shape=[shp(N_HEADS)] + [shp(N_KV_HEADS)] * 4,
        compiler_params=_params("parallel", "parallel"),
        name="nsa_prep",
    )(pq, pkv, cosf, sinf, nw)


def _compress_kernel(x_ref, pos_ref, w1_ref, w2_ref, cos_ref, sin_ref, nw_ref, o_ref):
    x = x_ref[0, 0, 0]
    n16 = x.shape[0]
    half = CMP_STRIDE * HEAD_DIM
    a = _dot((x + pos_ref[0, 0:1, :]).astype(BF16), w1_ref[0, 0:half, :])
    bmat = _dot((x + pos_ref[0, 1:2, :]).astype(BF16), w1_ref[0, half:2 * half, :])
    nxt = jnp.concatenate([bmat[1:, :], jnp.zeros((1, CMP_HIDDEN), F32)], axis=0)
    hid = a + nxt
    hid = hid * jax.nn.sigmoid(hid)
    out = _dot(hid.astype(BF16), w2_ref[0])
    ssq = jnp.sum(out * out, axis=-1, keepdims=True)
    normed = out * lax.rsqrt(ssq * (1.0 / HEAD_DIM) + EPS) * nw_ref[...]
    roped = _rope_lanes(normed, cos_ref[...], sin_ref[...])
    is_key = pl.program_id(1) == 0
    o_ref[0, 0, 0] = jnp.where(is_key, roped, out)
    del n16


def _nsa_compress(pkv, cmp_pos, cmp_w1, cmp_w2, kc_norm):
    b, t, _ = pkv.shape
    n16 = t // CMP_STRIDE
    flat = CMP_STRIDE * HEAD_DIM
    x = pkv[..., :2 * KV_WIDTH].reshape(b, n16, CMP_STRIDE, 2, N_KV_HEADS, HEAD_DIM)
    x = x.transpose(0, 3, 4, 1, 2, 5).reshape(b, 2, N_KV_HEADS, n16, flat)
    pos = cmp_pos.reshape(2, 2, flat)
    w1 = cmp_w1.astype(BF16)
    w2 = jnp.pad(cmp_w2, ((0, 0), (0, 0), (0, LANES - HEAD_DIM))).astype(BF16)
    cosf, sinf = _rope_tables(jnp.arange(n16, dtype=jnp.int32) * CMP_STRIDE + CMP_BLOCK - 1)
    nw = jnp.pad(kc_norm.reshape(1, HEAD_DIM), ((0, 0), (0, LANES - HEAD_DIM)))
    return pl.pallas_call(
        _compress_kernel,
        grid=(b, 2, N_KV_HEADS),
        in_specs=[pl.BlockSpec((1, 1, 1, n16, flat), lambda i, s, h: (i, s, h, 0, 0)),
                  pl.BlockSpec((1, 2, flat), lambda i, s, h: (s, 0, 0)),
                  pl.BlockSpec((1, 2 * flat, CMP_HIDDEN), lambda i, s, h: (s, 0, 0)),
                  pl.BlockSpec((1, CMP_HIDDEN, LANES), lambda i, s, h: (s, 0, 0)),
                  pl.BlockSpec((n16, LANES), lambda i, s, h: (0, 0)),
                  pl.BlockSpec((n16, LANES), lambda i, s, h: (0, 0)),
                  pl.BlockSpec((1, LANES), lambda i, s, h: (0, 0))],
        out_specs=pl.BlockSpec((1, 1, 1, n16, LANES), lambda i, s, h: (i, s, h, 0, 0)),
        out_shape=jax.ShapeDtypeStruct((b, 2, N_KV_HEADS, n16, LANES), F32),
        compiler_params=_params("parallel", "parallel", "parallel"),
        name="nsa_compress",
    )(x, pos, w1, w2, cosf, sinf, nw)


SLC_TILE = 512
WIN_SPAN = Q_BLOCK + WINDOW


def _nsa_attn_kernel(q_ref, cmp_ref, ks_ref, vs_ref, kw_ref, vw_ref, gt_ref, o_ref, *, n_slc):
    hkv = pl.program_id(1)
    qi = pl.program_id(2)
    start = qi * Q_BLOCK
    rows = N_GROUP * Q_BLOCK
    q = q_ref[0].reshape(rows, HEAD_DIM)
    n16 = cmp_ref.shape[3]

    def tq_of(shape):
        return start + (_iota(shape, 0) & (Q_BLOCK - 1))

    kc = cmp_ref[0, 0, 0][:, :HEAD_DIM].astype(BF16)
    vc = cmp_ref[0, 1, 0][:, :HEAD_DIM].astype(BF16)
    s_c = _dot_nt(q, kc)
    tq_c = tq_of(s_c.shape)
    valid_c = (_iota(s_c.shape, 1) * CMP_STRIDE + (CMP_BLOCK - 1)) <= tq_c
    s_c = jnp.where(valid_c, s_c, NEG)
    e_c = jnp.exp(s_c - jnp.max(s_c, axis=-1, keepdims=True))
    p_c = e_c / jnp.sum(e_c, axis=-1, keepdims=True)
    p_c = jnp.where(tq_c >= CMP_BLOCK - 1, p_c, 0.0)
    o_c = _dot(p_c.astype(BF16), vc)

    p_sum = p_c[0:Q_BLOCK]
    for g in range(1, N_GROUP):
        p_sum = p_sum + p_c[g * Q_BLOCK:(g + 1) * Q_BLOCK]
    cn, sb = _iota((n16, n_slc), 0), _iota((n16, n_slc), 1)
    c_lo, s_lo = cn * CMP_STRIDE, sb * SLC_BLOCK
    ov = jnp.maximum(jnp.minimum(c_lo + CMP_BLOCK, s_lo + SLC_BLOCK) - jnp.maximum(c_lo, s_lo), 0)
    ov = jnp.where(cn < n16 - 1, ov, 0)
    overlap = (ov.astype(F32) * (1.0 / CMP_BLOCK)).astype(BF16)
    imp = _dot01_r(p_sum, overlap, 2)
    tq_i = tq_of(imp.shape)
    blk = _iota(imp.shape, 1)
    cur = tq_i >> 6
    forced = (blk == 0) | (blk == cur) | (blk == cur - 1)
    imp = jnp.where(forced, BIG, imp)
    imp = jnp.where(blk * SLC_BLOCK <= tq_i, imp, NEG)
    rank = jnp.zeros(imp.shape, F32)
    for i in range(n_slc):
        col = imp[:, i:i + 1]
        rank = rank + _ind(col > imp) + _ind((col == imp) & (blk > i))
    sel = _ind(rank < float(min(N_SELECTED, n_slc)), BF16)

    blk_r = _iota((n_slc, SLC_TILE), 0)
    key_c = _iota((n_slc, SLC_TILE), 1)
    key_q = _iota((Q_BLOCK, SLC_TILE), 1)
    tq_s = tq_of((Q_BLOCK, SLC_TILE))

    def slc_step(kt, carry):
        m, l, acc = carry
        k0 = pl.multiple_of(kt * SLC_TILE, SLC_TILE)
        ks = ks_ref[0, 0, pl.ds(k0, SLC_TILE), :]
        vs = vs_ref[0, 0, pl.ds(k0, SLC_TILE), :]
        expand = _ind(((k0 + key_c) >> 6) == blk_r, BF16)
        picked = _dot(sel, expand)
        ok = (picked > 0.5) & ((k0 + key_q) <= tq_s)
        s = _dot_nt(q, ks).reshape(N_GROUP, Q_BLOCK, SLC_TILE)
        s = jnp.where(ok[None], s, NEG).reshape(rows, SLC_TILE)
        m_new = jnp.maximum(m, jnp.max(s, axis=-1, keepdims=True))
        alpha = jnp.exp(m - m_new)
        p = jnp.exp(s - m_new)
        l = alpha * l + jnp.sum(p, axis=-1, keepdims=True)
        acc = alpha * acc + _dot(p.astype(BF16), vs)
        return m_new, l, acc

    n_tiles = (start + Q_BLOCK + SLC_TILE - 1) // SLC_TILE
    init = (jnp.full((rows, 1), NEG, F32), jnp.zeros((rows, 1), F32), jnp.zeros((rows, HEAD_DIM), F32))
    _, l_s, acc_s = lax.fori_loop(0, n_tiles, slc_step, init)
    o_s = acc_s / l_s

    w0 = pl.multiple_of(jnp.maximum(start - WINDOW, 0), Q_BLOCK)
    kw = kw_ref[0, 0, pl.ds(w0, WIN_SPAN), :]
    vw = vw_ref[0, 0, pl.ds(w0, WIN_SPAN), :]
    s_w = _dot_nt(q, kw)
    dist = tq_of(s_w.shape) - (w0 + _iota(s_w.shape, 1))
    s_w = jnp.where((dist >= 0) & (dist < WINDOW), s_w, NEG)
    e_w = jnp.exp(s_w - jnp.max(s_w, axis=-1, keepdims=True))
    o_w = _dot(e_w.astype(BF16), vw) / jnp.sum(e_w, axis=-1, keepdims=True)

    gates = jax.nn.sigmoid(gt_ref[0])
    outs = []
    for g in range(N_GROUP):
        def gate_col(branch):
            c0 = GT_N + branch * N_HEADS + g
            c1 = c0 + N_GROUP
            return jnp.where(hkv == 0, gates[:, c0:c0 + 1], gates[:, c1:c1 + 1])
        sl = slice(g * Q_BLOCK, (g + 1) * Q_BLOCK)
        outs.append(gate_col(0) * o_c[sl] + gate_col(1) * o_s[sl] + gate_col(2) * o_w[sl])
    o_ref[0] = jnp.concatenate(outs, axis=-1)


def _nsa_attention(qh, cmp, ks, vs, kw, vw, gt):
    b, _, t, _ = qh.shape
    n16 = cmp.shape[3]
    n_slc = t // SLC_BLOCK
    full = lambda: pl.BlockSpec((1, 1, t, HEAD_DIM), lambda i, h, j: (i, h, 0, 0))
    kern = functools.partial(_nsa_attn_kernel, n_slc=n_slc)
    return pl.pallas_call(
        kern,
        grid=(b, N_KV_HEADS, t // Q_BLOCK),
        in_specs=[pl.BlockSpec((1, N_GROUP, Q_BLOCK, HEAD_DIM), lambda i, h, j: (i, h, j, 0)),
                  pl.BlockSpec((1, 2, 1, n16, LANES), lambda i, h, j: (i, 0, h, 0, 0)),
                  full(), full(), full(), full(),
                  pl.BlockSpec((1, Q_BLOCK, LANES), lambda i, h, j: (i, j, 0))],
        out_specs=pl.BlockSpec((1, Q_BLOCK, N_GROUP * HEAD_DIM), lambda i, h, j: (i, j, h)),
        out_shape=jax.ShapeDtypeStruct((b, t, N_WIDTH), F32),
        compiler_params=_params("parallel", "parallel", "arbitrary"),
        name="nsa_attention",
    )(qh, cmp, ks, vs, kw, vw, gt)


def _nsa(pq, pkv, gt, qk_norm, cmp_pos, cmp_w1, cmp_w2):
    qh, ks, vs, kw, vw = _nsa_prep(pq, pkv, qk_norm)
    cmp = _nsa_compress(pkv, cmp_pos, cmp_w1, cmp_w2, qk_norm[1])
    return _nsa_attention(qh, cmp, ks, vs, kw, vw, gt)


def _mixer(x, mod, nw, w_in, w_out, m_gate_b, m_norm_w, g_conv_w, g_a_log, g_dt_bias, g_norm_w,
           n_qk_norm, n_cmp_pos, n_cmp_w1, n_cmp_w2):
    pm, pg, pq, pkv, gt = _in_proj(x, mod, nw, _permute_w_in(w_in))
    y_m = _mlstm(pm, gt, m_gate_b, m_norm_w)
    y_g = _gdn(pg, gt, g_conv_w, g_a_log, g_dt_bias, g_norm_w)
    y_n = _nsa(pq, pkv, gt, n_qk_norm, n_cmp_pos, n_cmp_w1, n_cmp_w2)
    return _out_proj(x, mod, y_m, y_g, y_n, w_out)


def kernel(x, c, ada_w, ada_b, norm_w, ffn_w_up, ffn_w_down, w_in, w_out, mlstm_gate_b, mlstm_norm_w,
           gdn_conv_w, gdn_a_log, gdn_dt_bias, gdn_norm_w, nsa_qk_norm, nsa_cmp_pos, nsa_cmp_w1, nsa_cmp_w2):
    depth = ada_w.shape[0]
    b = x.shape[0]
    mods = _modulation(c, ada_w, ada_b).reshape(depth, b, 9, D_MODEL)
    for l in range(depth):
        mod = mods[l]
        x = _ffn(x, mod, norm_w[l, 0], ffn_w_up[l, 0].astype(BF16), ffn_w_down[l, 0].astype(BF16), 0)
        x = _mixer(x, mod, norm_w[l, 1], w_in[l], w_out[l], mlstm_gate_b[l], mlstm_norm_w[l],
                   gdn_conv_w[l], gdn_a_log[l], gdn_dt_bias[l], gdn_norm_w[l],
                   nsa_qk_norm[l], nsa_cmp_pos[l], nsa_cmp_w1[l], nsa_cmp_w2[l])
        x = _ffn(x, mod, norm_w[l, 2], ffn_w_up[l, 1].astype(BF16), ffn_w_down[l, 1].astype(BF16), 6)
    return x
```

```python
import functools

import jax
import jax.numpy as jnp
from jax import lax
from jax.experimental import pallas as pl
from jax.experimental.pallas import tpu as pltpu

F32 = jnp.float32
BF16 = jnp.bfloat16

D_MODEL = 1024
HEAD_DIM = 64
D_FF = 2816
EPS = 1e-6
NEG = -1e30
BIG = 1e30

M_HEADS = 4
G_HEADS = 4
N_HEADS = 8
N_KV_HEADS = 2
N_GROUP = N_HEADS // N_KV_HEADS
CHUNK = 64
CONV_K = 4
CMP_BLOCK = 32
CMP_STRIDE = 16
CMP_HIDDEN = 256
SLC_BLOCK = 64
N_SELECTED = 16
WINDOW = 512
Q_BLOCK = 128
ROPE_THETA = 10000.0

M_WIDTH = M_HEADS * HEAD_DIM
G_WIDTH = G_HEADS * HEAD_DIM
N_WIDTH = N_HEADS * HEAD_DIM
KV_WIDTH = N_KV_HEADS * HEAD_DIM
M_IN = 4 * M_WIDTH + 2 * M_HEADS
G_IN = 4 * G_WIDTH + 2 * G_HEADS
N_IN = N_WIDTH + 6 * KV_WIDTH + 3 * N_HEADS

LANES = 128
FF_TILE = 256
VMEM_LIMIT = 56 * 1024 * 1024

GT_M = 0
GT_G = 8
GT_N = 16


def _params(*sem):
    return pltpu.CompilerParams(dimension_semantics=sem, vmem_limit_bytes=VMEM_LIMIT)


def _dot(a, b):
    return jnp.dot(a, b, preferred_element_type=F32)


def _dot_nt(a, b):
    return lax.dot_general(a, b, (((1,), (1,)), ((), ())), preferred_element_type=F32)


def _dot_tn(a, b):
    return lax.dot_general(a, b, (((0,), (0,)), ((), ())), preferred_element_type=F32)


def _split_bf16(x, n):
    parts, r = [], x
    for _ in range(n):
        p = r.astype(BF16)
        parts.append(p)
        r = r - p.astype(F32)
    return parts


def _dot01_l(m01, x, n=3):
    acc = None
    for p in _split_bf16(x, n):
        t = _dot(m01, p)
        acc = t if acc is None else acc + t
    return acc


def _dot01_r(x, m01, n=3):
    acc = None
    for p in _split_bf16(x, n):
        t = _dot(p, m01)
        acc = t if acc is None else acc + t
    return acc


def _iota(shape, dim):
    return lax.broadcasted_iota(jnp.int32, shape, dim)


def _ind(cond, dtype=F32):
    return jnp.where(cond, 1.0, 0.0).astype(dtype)


def _softplus(x):
    return jnp.maximum(x, 0.0) + jnp.log1p(jnp.exp(-jnp.abs(x)))


def _group_ones(width):
    r, c = _iota((width, width), 0), _iota((width, width), 1)
    return _ind((r >> 6) == (c >> 6), BF16)


def _rope_lanes(x, cosf, sinf):
    lane = _iota(x.shape, 1)
    partner = jnp.where((lane & 63) < 32, pltpu.roll(x, 96, 1), pltpu.roll(x, 32, 1))
    return x * cosf + partner * sinf


def _mod_kernel(ct_ref, w_ref, b_ref, o_ref):
    ct = ct_ref[...]
    act = ct * jax.nn.sigmoid(ct)
    w = w_ref[0]
    rows = [jnp.sum(w * act[:, b:b + 1], axis=0, keepdims=True) for b in range(ct.shape[1])]
    o_ref[0] = jnp.concatenate(rows, axis=0) + b_ref[0]


def _modulation(c, ada_w, ada_b):
    depth, d, n = ada_w.shape
    b = c.shape[0]
    tn = 1024
    return pl.pallas_call(
        _mod_kernel,
        grid=(depth, n // tn),
        in_specs=[pl.BlockSpec((d, b), lambda l, j: (0, 0)),
                  pl.BlockSpec((1, d, tn), lambda l, j: (l, 0, j)),
                  pl.BlockSpec((1, 1, tn), lambda l, j: (l, 0, j))],
        out_specs=pl.BlockSpec((1, b, tn), lambda l, j: (l, 0, j)),
        out_shape=jax.ShapeDtypeStruct((depth, b, n), F32),
        compiler_params=_params("parallel", "parallel"),
        name="adaln_mod",
    )(c.T, ada_w, ada_b.reshape(depth, 1, n))


def _adaln(x, nw, shift, scale):
    y = x * lax.rsqrt(jnp.mean(x * x, axis=-1, keepdims=True) + EPS) * nw
    return y * (1.0 + scale) + shift


def _ffn_kernel(x_ref, mod_ref, nw_ref, wup_ref, wdn_ref, o_ref, *, row0, res_w):
    x = x_ref[0]
    shift = mod_ref[0, row0:row0 + 1, :]
    scale = mod_ref[0, row0 + 1:row0 + 2, :]
    gate = mod_ref[0, row0 + 2:row0 + 3, :]
    h = _adaln(x, nw_ref[...], shift, scale).astype(BF16)
    acc = jnp.zeros(x.shape, F32)
    for c in range(D_FF // FF_TILE):
        lo = c * FF_TILE
        g = _dot(h, wup_ref[:, lo:lo + FF_TILE])
        u = _dot(h, wup_ref[:, D_FF + lo:D_FF + lo + FF_TILE])
        a = (jax.nn.silu(g) * u).astype(BF16)
        acc = acc + _dot(a, wdn_ref[lo:lo + FF_TILE, :])
    o_ref[0] = x + (res_w * gate) * acc


def _ffn(x, mod, nw, w_up, w_down, row0):
    b, t, d = x.shape
    tm = min(512, t)
    kern = functools.partial(_ffn_kernel, row0=row0, res_w=0.5)
    return pl.pallas_call(
        kern,
        grid=(b, t // tm),
        in_specs=[pl.BlockSpec((1, tm, d), lambda i, j: (i, j, 0)),
                  pl.BlockSpec((1, 9, d), lambda i, j: (i, 0, 0)),
                  pl.BlockSpec((1, d), lambda i, j: (0, 0)),
                  pl.BlockSpec((d, 2 * D_FF), lambda i, j: (0, 0), pipeline_mode=pl.Buffered(1)),
                  pl.BlockSpec((D_FF, d), lambda i, j: (0, 0), pipeline_mode=pl.Buffered(1))],
        out_specs=pl.BlockSpec((1, tm, d), lambda i, j: (i, j, 0)),
        out_shape=jax.ShapeDtypeStruct(x.shape, F32),
        compiler_params=_params("parallel", "parallel"),
        name="ffn",
    )(x, mod, nw.reshape(1, d), w_up, w_down)


def _inproj_kernel(x_ref, mod_ref, nw_ref, w_ref, pm_ref, pg_ref, pq_ref, pkv_ref, gt_ref):
    x = x_ref[0]
    h = _adaln(x, nw_ref[...], mod_ref[0, 3:4, :], mod_ref[0, 4:5, :]).astype(BF16)
    off = 0
    for ref in (pm_ref, pg_ref, pq_ref, pkv_ref, gt_ref):
        wd = ref.shape[-1]
        ref[0] = _dot(h, w_ref[:, off:off + wd])
        off += wd


IN_PERM_WIDTH = 4 * M_WIDTH + 4 * G_WIDTH + N_WIDTH + 6 * KV_WIDTH + LANES


def _permute_w_in(w_in):
    g0, n0 = M_IN, M_IN + G_IN
    gates = jnp.concatenate([w_in[:, 4 * M_WIDTH:M_IN], w_in[:, g0 + 4 * G_WIDTH:g0 + G_IN],
                             w_in[:, n0 + N_WIDTH + 6 * KV_WIDTH:n0 + N_IN]], axis=1)
    gates = jnp.pad(gates, ((0, 0), (0, LANES - gates.shape[1])))
    return jnp.concatenate([w_in[:, :4 * M_WIDTH], w_in[:, g0:g0 + 4 * G_WIDTH],
                            w_in[:, n0:n0 + N_WIDTH + 6 * KV_WIDTH], gates], axis=1).astype(BF16)


def _in_proj(x, mod, nw, w_perm):
    b, t, d = x.shape
    tm = min(512, t)
    widths = (4 * M_WIDTH, 4 * G_WIDTH, N_WIDTH, 6 * KV_WIDTH, LANES)
    return pl.pallas_call(
        _inproj_kernel,
        grid=(b, t // tm),
        in_specs=[pl.BlockSpec((1, tm, d), lambda i, j: (i, j, 0)),
                  pl.BlockSpec((1, 9, d), lambda i, j: (i, 0, 0)),
                  pl.BlockSpec((1, d), lambda i, j: (0, 0)),
                  pl.BlockSpec((d, IN_PERM_WIDTH), lambda i, j: (0, 0), pipeline_mode=pl.Buffered(1))],
        out_specs=[pl.BlockSpec((1, tm, w), lambda i, j: (i, j, 0)) for w in widths],
        out_shape=[jax.ShapeDtypeStruct((b, t, w), F32) for w in widths],
        compiler_params=_params("parallel", "parallel"),
        name="mixer_in_proj",
    )(x, mod, nw.reshape(1, d), w_perm)


def _outproj_kernel(x_ref, mod_ref, ym_ref, yg_ref, yn_ref, w_ref, o_ref):
    y = jnp.concatenate([ym_ref[0], yg_ref[0], yn_ref[0]], axis=-1).astype(BF16)
    o_ref[0] = x_ref[0] + mod_ref[0, 5:6, :] * _dot(y, w_ref[...])


def _out_proj(x, mod, y_m, y_g, y_n, w_out):
    b, t, d = x.shape
    tm = min(512, t)
    row = lambda w: pl.BlockSpec((1, tm, w), lambda i, j: (i, j, 0))
    return pl.pallas_call(
        _outproj_kernel,
        grid=(b, t // tm),
        in_specs=[row(d), pl.BlockSpec((1, 9, d), lambda i, j: (i, 0, 0)),
                  row(M_WIDTH), row(G_WIDTH), row(N_WIDTH),
                  pl.BlockSpec((d, d), lambda i, j: (0, 0), pipeline_mode=pl.Buffered(1))],
        out_specs=row(d),
        out_shape=jax.ShapeDtypeStruct(x.shape, F32),
        compiler_params=_params("parallel", "parallel"),
        name="mixer_out_proj",
    )(x, mod, y_m, y_g, y_n, w_out.astype(BF16))


def _chunk_masks():
    r, c = _iota((CHUNK, CHUNK), 0), _iota((CHUNK, CHUNK), 1)
    return r, c


def _mlstm_kernel(pm_ref, gt_ref, gb_ref, nw_ref, o_ref, c_scr, n_scr, m_scr, *, n_chunks):
    @pl.when(pl.program_id(1) == 0)
    def _():
        c_scr[...] = jnp.zeros(c_scr.shape, F32)
        n_scr[...] = jnp.zeros(n_scr.shape, F32)
        m_scr[...] = jnp.zeros(m_scr.shape, F32)

    r, c = _chunk_masks()
    causal = r >= c
    low = _ind(causal, BF16)
    low_ones = jnp.concatenate([low, jnp.ones((CHUNK, CHUNK), BF16)], axis=1)
    strict_up = _ind(r > c)
    eye = _ind(r == c)
    nw = nw_ref[...]
    gb = gb_ref[...]

    def chunk(ci, carry):
        r0 = pl.multiple_of(ci * CHUNK, CHUNK)
        g = gt_ref[0, pl.ds(r0, CHUNK), :] + gb
        lf = -_softplus(-g)
        cum = _dot01_l(low, lf)
        for h in range(M_HEADS):
            sl = lambda k: pm_ref[0, pl.ds(r0, CHUNK), k * M_WIDTH + h * HEAD_DIM:k * M_WIDTH + (h + 1) * HEAD_DIM]
            q = sl(0) * (HEAD_DIM ** -0.5)
            k = sl(1)
            v = sl(2)
            og = jax.nn.sigmoid(sl(3))
            li_c = g[:, GT_M + h:GT_M + h + 1]
            lf_c = lf[:, GT_M + M_HEADS + h:GT_M + M_HEADS + h + 1]
            cum_c = cum[:, GT_M + M_HEADS + h:GT_M + M_HEADS + h + 1]
            c_prev, n_prev, m_prev = c_scr[h], n_scr[h], m_scr[h]
            dmat = _dot01_l(low_ones, jnp.concatenate([lf_c * strict_up, li_c * eye], axis=0))
            dmat = jnp.where(causal, dmat, NEG)
            m_inter = cum_c + m_prev
            m_t = jnp.maximum(m_inter, jnp.max(dmat, axis=-1, keepdims=True))
            s = _dot_nt(q.astype(BF16), k.astype(BF16)) * jnp.exp(dmat - m_t)
            inter = jnp.exp(m_inter - m_t)
            num = _dot(s.astype(BF16), v.astype(BF16)) + inter * _dot(q.astype(BF16), c_prev.astype(BF16))
            den = jnp.sum(s, axis=-1, keepdims=True) + inter * jnp.sum(q * n_prev, axis=-1, keepdims=True)
            den = jnp.maximum(jnp.abs(den), jnp.exp(-m_t))
            hh = num / den
            hh = hh * lax.rsqrt(jnp.mean(hh * hh, axis=-1, keepdims=True) + EPS) * nw
            o_ref[0, pl.ds(r0, CHUNK), h * HEAD_DIM:(h + 1) * HEAD_DIM] = og * hh
            g_tot = cum_c[CHUNK - 1:CHUNK, :]
            a = g_tot - cum_c + li_c
            m_loc = jnp.max(a, axis=0, keepdims=True)
            kw = k * jnp.exp(a - m_loc)
            c_loc = _dot_tn(kw.astype(BF16), v.astype(BF16))
            n_loc = jnp.sum(kw, axis=0, keepdims=True)
            m_new = jnp.maximum(g_tot + m_prev, m_loc)
            s_old = jnp.exp(g_tot + m_prev - m_new)
            s_new = jnp.exp(m_loc - m_new)
            c_scr[h] = s_old * c_prev + s_new * c_loc
            n_scr[h] = s_old * n_prev + s_new * n_loc
            m_scr[h] = m_new
        return carry

    lax.fori_loop(0, n_chunks, chunk, 0)


def _mlstm(pm, gt, gate_b, norm_w):
    b, t, _ = pm.shape
    tc = min(512, t)
    gb = jnp.zeros((1, LANES), F32).at[0, GT_M:GT_M + 2 * M_HEADS].set(gate_b.reshape(-1))
    kern = functools.partial(_mlstm_kernel, n_chunks=tc // CHUNK)
    return pl.pallas_call(
        kern,
        grid=(b, t // tc),
        in_specs=[pl.BlockSpec((1, tc, 4 * M_WIDTH), lambda i, j: (i, j, 0)),
                  pl.BlockSpec((1, tc, LANES), lambda i, j: (i, j, 0)),
                  pl.BlockSpec((1, LANES), lambda i, j: (0, 0)),
                  pl.BlockSpec((1, HEAD_DIM), lambda i, j: (0, 0))],
        out_specs=pl.BlockSpec((1, tc, M_WIDTH), lambda i, j: (i, j, 0)),
        out_shape=jax.ShapeDtypeStruct((b, t, M_WIDTH), F32),
        scratch_shapes=[pltpu.VMEM((M_HEADS, HEAD_DIM, HEAD_DIM), F32),
                        pltpu.VMEM((M_HEADS, 1, HEAD_DIM), F32),
                        pltpu.VMEM((M_HEADS, 1, 1), F32)],
        compiler_params=_params("parallel", "arbitrary"),
        name="mlstm",
    )(pm, gt, gb, norm_w.reshape(1, HEAD_DIM))


def _gdn_kernel(pg_ref, gt_ref, cw_ref, hp_ref, nw_ref, o_ref, xbuf, qkv_scr, la_scr, s_scr, *, tc):
    first = pl.program_id(1) == 0

    @pl.when(first)
    def _():
        s_scr[...] = jnp.zeros(s_scr.shape, F32)
        xbuf[0:8, :] = jnp.zeros((8, 3 * G_WIDTH), F32)

    @pl.when(jnp.logical_not(first))
    def _():
        xbuf[0:8, :] = xbuf[tc:tc + 8, :]

    xbuf[8:8 + tc, :] = pg_ref[0, :, 0:3 * G_WIDTH]
    acc = None
    for kk in range(CONV_K):
        term = xbuf[8 - (CONV_K - 1) + kk:8 - (CONV_K - 1) + kk + tc, :] * cw_ref[kk:kk + 1, :]
        acc = term if acc is None else acc + term
    act = acc * jax.nn.sigmoid(acc)
    gones = _group_ones(G_WIDTH)
    for part in range(2):
        xx = act[:, part * G_WIDTH:(part + 1) * G_WIDTH]
        ssq = _dot01_r(xx * xx, gones, 2)
        xx = xx * lax.rsqrt(ssq + EPS)
        if part == 0:
            xx = xx * (HEAD_DIM ** -0.5)
        qkv_scr[:, part * G_WIDTH:(part + 1) * G_WIDTH] = xx
    qkv_scr[:, 2 * G_WIDTH:3 * G_WIDTH] = act[:, 2 * G_WIDTH:3 * G_WIDTH]
    gate = gt_ref[0]
    neg_rate = -jnp.exp(hp_ref[0:1, :])
    la_scr[...] = neg_rate * _softplus(gate + hp_ref[1:2, :])

    r, c = _chunk_masks()
    low = _ind(r >= c, BF16)
    strict_low = r > c
    incl_low = r >= c
    strict_up = _ind(r > c)
    eye = _ind(r == c)
    base_mask = _ind(((r >> 2) == (c >> 2)) & (r > c))
    merge_masks = [_ind((((r >> s) & 1) == 1) & ((c >> s) == (r >> s) - 1)) for s in (2, 3, 4, 5)]
    nw = nw_ref[...]

    def chunk(ci, carry):
        r0 = pl.multiple_of(ci * CHUNK, CHUNK)
        la = la_scr[pl.ds(r0, CHUNK), :]
        gam = _dot01_l(low, la)
        beta_all = jax.nn.sigmoid(gt_ref[0, pl.ds(r0, CHUNK), :])
        for h in range(G_HEADS):
            sl = lambda k: qkv_scr[pl.ds(r0, CHUNK), k * G_WIDTH + h * HEAD_DIM:k * G_WIDTH + (h + 1) * HEAD_DIM]
            q, k, v = sl(0), sl(1), sl(2)
            z = pg_ref[0, pl.ds(r0, CHUNK), 3 * G_WIDTH + h * HEAD_DIM:3 * G_WIDTH + (h + 1) * HEAD_DIM]
            la_c = la[:, GT_G + h:GT_G + h + 1]
            gam_c = gam[:, GT_G + h:GT_G + h + 1]
            beta_c = beta_all[:, GT_G + G_HEADS + h:GT_G + G_HEADS + h + 1]
            diff = _dot01_l(low, la_c * strict_up)
            dec_strict = jnp.exp(jnp.where(strict_low, diff, NEG))
            dec_incl = jnp.exp(jnp.where(incl_low, diff, NEG))
            kb = k.astype(BF16)
            kk = _dot_nt(kb, kb)
            amat = beta_c * kk * dec_strict
            n0 = (-(amat * base_mask)).astype(BF16)
            inv = eye + n0.astype(F32)
            inv = inv + _dot(inv.astype(BF16), _dot(n0, n0).astype(BF16))
            for mask in merge_masks:
                invb = inv.astype(BF16)
                inv = inv - _dot(_dot(invb, (amat * mask).astype(BF16)).astype(BF16), invb)
            e_gam = jnp.exp(gam_c)
            rhs = jnp.concatenate([beta_c * v, (beta_c * e_gam) * k], axis=1)
            sol = _dot(inv.astype(BF16), rhs.astype(BF16))
            u0, w_mat = sol[:, :HEAD_DIM], sol[:, HEAD_DIM:]
            p_mat = _dot_nt(q.astype(BF16), kb) * dec_incl
            gam_last = gam_c[CHUNK - 1:CHUNK, :]
            q_g = q * e_gam
            k_d = k * jnp.exp(gam_last - gam_c)
            s_prev = s_scr[h]
            sb = s_prev.astype(BF16)
            u = u0 - _dot(w_mat.astype(BF16), sb)
            ub = u.astype(BF16)
            o = _dot(q_g.astype(BF16), sb) + _dot(p_mat.astype(BF16), ub)
            s_scr[h] = jnp.exp(gam_last) * s_prev + _dot_tn(k_d.astype(BF16), ub)
            o = o * lax.rsqrt(jnp.mean(o * o, axis=-1, keepdims=True) + EPS) * nw
            o_ref[0, pl.ds(r0, CHUNK), h * HEAD_DIM:(h + 1) * HEAD_DIM] = o * (z * jax.nn.sigmoid(z))
        return carry

    lax.fori_loop(0, tc // CHUNK, chunk, 0)


def _gdn(pg, gt, conv_w, a_log, dt_bias, norm_w):
    b, t, _ = pg.shape
    tc = min(512, t)
    hp = jnp.zeros((2, LANES), F32)
    hp = hp.at[0, GT_G:GT_G + G_HEADS].set(a_log).at[1, GT_G:GT_G + G_HEADS].set(dt_bias)
    kern = functools.partial(_gdn_kernel, tc=tc)
    return pl.pallas_call(
        kern,
        grid=(b, t // tc),
        in_specs=[pl.BlockSpec((1, tc, 4 * G_WIDTH), lambda i, j: (i, j, 0)),
                  pl.BlockSpec((1, tc, LANES), lambda i, j: (i, j, 0)),
                  pl.BlockSpec((CONV_K, 3 * G_WIDTH), lambda i, j: (0, 0)),
                  pl.BlockSpec((2, LANES), lambda i, j: (0, 0)),
                  pl.BlockSpec((1, HEAD_DIM), lambda i, j: (0, 0))],
        out_specs=pl.BlockSpec((1, tc, G_WIDTH), lambda i, j: (i, j, 0)),
        out_shape=jax.ShapeDtypeStruct((b, t, G_WIDTH), F32),
        scratch_shapes=[pltpu.VMEM((tc + 8, 3 * G_WIDTH), F32),
                        pltpu.VMEM((tc, 3 * G_WIDTH), F32),
                        pltpu.VMEM((tc, LANES), F32),
                        pltpu.VMEM((G_HEADS, HEAD_DIM, HEAD_DIM), F32)],
        compiler_params=_params("parallel", "arbitrary"),
        name="gdn",
    )(pg, gt, conv_w.astype(F32), hp, norm_w.reshape(1, HEAD_DIM))


def _rope_tables(pos):
    half = HEAD_DIM // 2
    inv_freq = jnp.power(ROPE_THETA, -jnp.arange(half, dtype=F32) / half)
    ang = pos.astype(F32)[:, None] * inv_freq[None, :]
    cos, sin = jnp.cos(ang), jnp.sin(ang)
    cosf = jnp.tile(cos, (1, LANES // half))
    sinf = jnp.tile(jnp.concatenate([-sin, sin], axis=1), (1, LANES // HEAD_DIM))
    return cosf, sinf


KS_AUG = LANES + HEAD_DIM


def _nsa_prep_kernel(pq_ref, pkv_ref, cos_ref, sin_ref, nw_ref, q_ref, ks_ref, vs_ref, kw_ref, vw_ref):
    cosf, sinf = cos_ref[...], sin_ref[...]
    gones = _group_ones(LANES)

    def norm_rope(x, w):
        ssq = _dot01_r(x * x, gones, 2)
        y = x * lax.rsqrt(ssq * (1.0 / HEAD_DIM) + EPS) * w
        return _rope_lanes(y, cosf, sinf)

    def put_heads(ref, slab, h0):
        ref[0, h0] = slab[:, :HEAD_DIM].astype(BF16)
        ref[0, h0 + 1] = slab[:, HEAD_DIM:].astype(BF16)

    for s in range(N_WIDTH // LANES):
        qs = norm_rope(pq_ref[0, :, s * LANES:(s + 1) * LANES], nw_ref[0:1, :]) * (HEAD_DIM ** -0.5)
        put_heads(q_ref, qs, 2 * s)
    kv = lambda i: pkv_ref[0, :, i * KV_WIDTH:(i + 1) * KV_WIDTH]
    tm = cosf.shape[0]
    tok = pl.program_id(1) * tm + _iota((tm, LANES), 0)
    onehot = _ind((tok >> 6) == _iota((tm, LANES), 1), BF16)
    k_slc = norm_rope(kv(2), nw_ref[2:3, :])
    for h in range(N_KV_HEADS):
        ks_ref[0, h, :, 0:LANES] = onehot
        ks_ref[0, h, :, LANES:LANES + HEAD_DIM] = k_slc[:, h * HEAD_DIM:(h + 1) * HEAD_DIM].astype(BF16)
    put_heads(kw_ref, norm_rope(kv(4), nw_ref[3:4, :]), 0)
    lane = _iota((tm, LANES), 1)
    for ref, slab in ((vs_ref, kv(3)), (vw_ref, kv(5))):
        ref[0, 0] = jnp.where(lane < HEAD_DIM, slab, 1.0).astype(BF16)
        ref[0, 1] = jnp.where(lane < HEAD_DIM, pltpu.roll(slab, HEAD_DIM, 1), 1.0).astype(BF16)


def _nsa_prep(pq, pkv, qk_norm):
    b, t, _ = pq.shape
    assert t // SLC_BLOCK <= LANES
    tm = min(512, t)
    cosf, sinf = _rope_tables(jnp.arange(t, dtype=jnp.int32))
    nw = jnp.tile(qk_norm, (1, LANES // HEAD_DIM))
    heads = lambda n: pl.BlockSpec((1, n, tm, HEAD_DIM), lambda i, j: (i, 0, j, 0))
    wide = lambda w: pl.BlockSpec((1, N_KV_HEADS, tm, w), lambda i, j: (i, 0, j, 0))
    shp = lambda n, w: jax.ShapeDtypeStruct((b, n, t, w), BF16)
    return pl.pallas_call(
        _nsa_prep_kernel,
        grid=(b, t // tm),
        in_specs=[pl.BlockSpec((1, tm, N_WIDTH), lambda i, j: (i, j, 0)),
                  pl.BlockSpec((1, tm, 6 * KV_WIDTH), lambda i, j: (i, j, 0)),
                  pl.BlockSpec((tm, LANES), lambda i, j: (j, 0)),
                  pl.BlockSpec((tm, LANES), lambda i, j: (j, 0)),
                  pl.BlockSpec((4, LANES), lambda i, j: (0, 0))],
        out_specs=[heads(N_HEADS), wide(KS_AUG), wide(LANES), heads(N_KV_HEADS), wide(LANES)],
        out_shape=[shp(N_HEADS, HEAD_DIM), shp(N_KV_HEADS, KS_AUG), shp(N_KV_HEADS, LANES),
                   shp(N_KV_HEADS, HEAD_DIM), shp(N_KV_HEADS, LANES)],
        compiler_params=_params("parallel", "parallel"),
        name="nsa_prep",
    )(pq, pkv, cosf, sinf, nw)


def _compress_kernel(x_ref, pos_ref, w1_ref, w2_ref, cos_ref, sin_ref, nw_ref, o_ref):
    x = x_ref[0, 0, 0]
    n16 = x.shape[0]
    half = CMP_STRIDE * HEAD_DIM
    a = _dot((x + pos_ref[0, 0:1, :]).astype(BF16), w1_ref[0, 0:half, :])
    bmat = _dot((x + pos_ref[0, 1:2, :]).astype(BF16), w1_ref[0, half:2 * half, :])
    nxt = jnp.concatenate([bmat[1:, :], jnp.zeros((1, CMP_HIDDEN), F32)], axis=0)
    hid = a + nxt
    hid = hid * jax.nn.sigmoid(hid)
    out = _dot(hid.astype(BF16), w2_ref[0])
    ssq = jnp.sum(out * out, axis=-1, keepdims=True)
    normed = out * lax.rsqrt(ssq * (1.0 / HEAD_DIM) + EPS) * nw_ref[...]
    roped = _rope_lanes(normed, cos_ref[...], sin_ref[...])
    is_key = pl.program_id(1) == 0
    o_ref[0, 0, 0] = jnp.where(is_key, roped, out)
    del n16


def _nsa_compress(pkv, cmp_pos, cmp_w1, cmp_w2, kc_norm):
    b, t, _ = pkv.shape
    n16 = t // CMP_STRIDE
    flat = CMP_STRIDE * HEAD_DIM
    x = pkv[..., :2 * KV_WIDTH].reshape(b, n16, CMP_STRIDE, 2, N_KV_HEADS, HEAD_DIM)
    x = x.transpose(0, 3, 4, 1, 2, 5).reshape(b, 2, N_KV_HEADS, n16, flat)
    pos = cmp_pos.reshape(2, 2, flat)
    w1 = cmp_w1.astype(BF16)
    w2 = jnp.pad(cmp_w2, ((0, 0), (0, 0), (0, LANES - HEAD_DIM))).astype(BF16)
    cosf, sinf = _rope_tables(jnp.arange(n16, dtype=jnp.int32) * CMP_STRIDE + CMP_BLOCK - 1)
    nw = jnp.pad(kc_norm.reshape(1, HEAD_DIM), ((0, 0), (0, LANES - HEAD_DIM)))
    return pl.pallas_call(
        _compress_kernel,
        grid=(b, 2, N_KV_HEADS),
        in_specs=[pl.BlockSpec((1, 1, 1, n16, flat), lambda i, s, h: (i, s, h, 0, 0)),
                  pl.BlockSpec((1, 2, flat), lambda i, s, h: (s, 0, 0)),
                  pl.BlockSpec((1, 2 * flat, CMP_HIDDEN), lambda i, s, h: (s, 0, 0)),
                  pl.BlockSpec((1, CMP_HIDDEN, LANES), lambda i, s, h: (s, 0, 0)),
                  pl.BlockSpec((n16, LANES), lambda i, s, h: (0, 0)),
                  pl.BlockSpec((n16, LANES), lambda i, s, h: (0, 0)),
                  pl.BlockSpec((1, LANES), lambda i, s, h: (0, 0))],
        out_specs=pl.BlockSpec((1, 1, 1, n16, LANES), lambda i, s, h: (i, s, h, 0, 0)),
        out_shape=jax.ShapeDtypeStruct((b, 2, N_KV_HEADS, n16, LANES), F32),
        compiler_params=_params("parallel", "parallel", "parallel"),
        name="nsa_compress",
    )(x, pos, w1, w2, cosf, sinf, nw)


SLC_TILE = 512
SLC_SUB = 512
WIN_SPAN = Q_BLOCK + WINDOW


def _nsa_attn_kernel(q_ref, cmp_ref, ks_ref, vs_ref, kw_ref, vw_ref, gt_ref, o_ref, *, n_slc):
    hkv = pl.program_id(1)
    qi = pl.program_id(2)
    start = qi * Q_BLOCK
    rows = N_GROUP * Q_BLOCK
    q = q_ref[0].reshape(rows, HEAD_DIM)
    n16 = cmp_ref.shape[3]

    def tq_of(shape):
        return start + (_iota(shape, 0) & (Q_BLOCK - 1))

    kc = cmp_ref[0, 0, 0][:, :HEAD_DIM].astype(BF16)
    vc = cmp_ref[0, 1, 0][:, :HEAD_DIM].astype(BF16)
    s_c = _dot_nt(q, kc)
    tq_c = tq_of(s_c.shape)
    valid_c = (_iota(s_c.shape, 1) * CMP_STRIDE + (CMP_BLOCK - 1)) <= tq_c
    s_c = jnp.where(valid_c, s_c, NEG)
    e_c = jnp.exp(s_c - jnp.max(s_c, axis=-1, keepdims=True))
    p_c = e_c / jnp.sum(e_c, axis=-1, keepdims=True)
    p_c = jnp.where(tq_c >= CMP_BLOCK - 1, p_c, 0.0)
    o_c = _dot(p_c.astype(BF16), vc)

    p_sum = p_c[0:Q_BLOCK]
    for g in range(1, N_GROUP):
        p_sum = p_sum + p_c[g * Q_BLOCK:(g + 1) * Q_BLOCK]
    sb, cn = _iota((n_slc, n16), 0), _iota((n_slc, n16), 1)
    c_lo, s_lo = cn * CMP_STRIDE, sb * SLC_BLOCK
    ov = jnp.maximum(jnp.minimum(c_lo + CMP_BLOCK, s_lo + SLC_BLOCK) - jnp.maximum(c_lo, s_lo), 0)
    ov = jnp.where(cn < n16 - 1, ov, 0)
    overlap_t = (ov.astype(F32) * (1.0 / CMP_BLOCK)).astype(BF16)
    imp = None
    for part in _split_bf16(p_sum, 2):
        term = _dot_nt(overlap_t, part)
        imp = term if imp is None else imp + term
    tq_i = start + _iota(imp.shape, 1)
    blk = _iota(imp.shape, 0)
    cur = tq_i >> 6
    forced = (blk == 0) | (blk == cur) | (blk == cur - 1)
    imp = jnp.where(forced, BIG, imp)
    imp = jnp.where(blk * SLC_BLOCK <= tq_i, imp, NEG)
    blk_f = blk.astype(F32)
    sel_t = jnp.zeros(imp.shape, F32)
    for _ in range(min(N_SELECTED, n_slc)):
        top = jnp.max(imp, axis=0, keepdims=True)
        first = jnp.min(jnp.where(imp == top, blk_f, float(n_slc)), axis=0, keepdims=True)
        hit = blk_f == first
        sel_t = jnp.where(hit, 1.0, sel_t)
        imp = jnp.where(hit, -3e38, imp)

    own = pl.multiple_of(start, Q_BLOCK)
    kd = ks_ref[0, 0, pl.ds(own, Q_BLOCK), LANES:LANES + HEAD_DIM]
    vd = vs_ref[0, 0, pl.ds(own, Q_BLOCK), :]
    s_d = _dot_nt(q, kd)
    s_d = jnp.where(_iota(s_d.shape, 1) <= (_iota(s_d.shape, 0) & (Q_BLOCK - 1)), s_d, NEG)
    m_d = jnp.max(s_d, axis=-1, keepdims=True)
    init = (m_d, _dot(jnp.exp(s_d - m_d).astype(BF16), vd))

    bias_t = (jnp.where(blk < 2 * qi, sel_t, 0.0) - 1.0) * BIG
    if n_slc < LANES:
        bias_t = jnp.concatenate([bias_t, jnp.zeros((LANES - n_slc, Q_BLOCK), F32)], axis=0)
    eye = _ind(_iota((LANES, LANES), 0) == _iota((LANES, LANES), 1), BF16)
    bias = _dot_tn(bias_t.astype(BF16), eye).astype(BF16)
    q_aug = jnp.concatenate([jnp.concatenate([bias] * N_GROUP, axis=0), q], axis=1)

    def slc_step(kt, carry):
        m, acc = carry
        for part in range(SLC_TILE // SLC_SUB):
            k0 = pl.multiple_of(kt * SLC_TILE + part * SLC_SUB, SLC_SUB)
            s = _dot_nt(q_aug, ks_ref[0, 0, pl.ds(k0, SLC_SUB), :])
            m_new = jnp.maximum(m, jnp.max(s, axis=-1, keepdims=True))
            p = jnp.exp(s - m_new).astype(BF16)
            acc = jnp.exp(m - m_new) * acc + _dot(p, vs_ref[0, 0, pl.ds(k0, SLC_SUB), :])
            m = m_new
        return m, acc

    n_tiles = (start + SLC_TILE - 1) // SLC_TILE
    _, acc_s = lax.fori_loop(0, n_tiles, slc_step, init)
    o_s = (acc_s / pltpu.roll(acc_s, HEAD_DIM, 1))[:, :HEAD_DIM]

    w0 = pl.multiple_of(jnp.maximum(start - WINDOW, 0), Q_BLOCK)
    kw = kw_ref[0, 0, pl.ds(w0, WIN_SPAN), :]
    vw = vw_ref[0, 0, pl.ds(w0, WIN_SPAN), :]
    s_w = _dot_nt(q, kw)
    dist = tq_of(s_w.shape) - (w0 + _iota(s_w.shape, 1))
    s_w = jnp.where((dist >= 0) & (dist < WINDOW), s_w, NEG)
    e_w = jnp.exp(s_w - jnp.max(s_w, axis=-1, keepdims=True))
    acc_w = _dot(e_w.astype(BF16), vw)
    o_w = (acc_w / pltpu.roll(acc_w, HEAD_DIM, 1))[:, :HEAD_DIM]

    gates = jax.nn.sigmoid(gt_ref[0])
    outs = []
    for g in range(N_GROUP):
        def gate_col(branch):
            c0 = GT_N + branch * N_HEADS + g
            c1 = c0 + N_GROUP
            return jnp.where(hkv == 0, gates[:, c0:c0 + 1], gates[:, c1:c1 + 1])
        sl = slice(g * Q_BLOCK, (g + 1) * Q_BLOCK)
        outs.append(gate_col(0) * o_c[sl] + gate_col(1) * o_s[sl] + gate_col(2) * o_w[sl])
    o_ref[0] = jnp.concatenate(outs, axis=-1)


def _nsa_attention(qh, cmp, ks, vs, kw, vw, gt):
    b, _, t, _ = qh.shape
    n16 = cmp.shape[3]
    n_slc = t // SLC_BLOCK
    full = lambda w: pl.BlockSpec((1, 1, t, w), lambda i, h, j: (i, h, 0, 0))
    kern = functools.partial(_nsa_attn_kernel, n_slc=n_slc)
    return pl.pallas_call(
        kern,
        grid=(b, N_KV_HEADS, t // Q_BLOCK),
        in_specs=[pl.BlockSpec((1, N_GROUP, Q_BLOCK, HEAD_DIM), lambda i, h, j: (i, h, j, 0)),
                  pl.BlockSpec((1, 2, 1, n16, LANES), lambda i, h, j: (i, 0, h, 0, 0)),
                  full(KS_AUG), full(LANES), full(HEAD_DIM), full(LANES),
                  pl.BlockSpec((1, Q_BLOCK, LANES), lambda i, h, j: (i, j, 0))],
        out_specs=pl.BlockSpec((1, Q_BLOCK, N_GROUP * HEAD_DIM), lambda i, h, j: (i, j, h)),
        out_shape=jax.ShapeDtypeStruct((b, t, N_WIDTH), F32),
        compiler_params=_params("parallel", "parallel", "arbitrary"),
        name="nsa_attention",
    )(qh, cmp, ks, vs, kw, vw, gt)


def _nsa(pq, pkv, gt, qk_norm, cmp_pos, cmp_w1, cmp_w2):
    qh, ks, vs, kw, vw = _nsa_prep(pq, pkv, qk_norm)
    cmp = _nsa_compress(pkv, cmp_pos, cmp_w1, cmp_w2, qk_norm[1])
    return _nsa_attention(qh, cmp, ks, vs, kw, vw, gt)


def _mixer(x, mod, nw, w_in, w_out, m_gate_b, m_norm_w, g_conv_w, g_a_log, g_dt_bias, g_norm_w,
           n_qk_norm, n_cmp_pos, n_cmp_w1, n_cmp_w2):
    pm, pg, pq, pkv, gt = _in_proj(x, mod, nw, _permute_w_in(w_in))
    y_m = _mlstm(pm, gt, m_gate_b, m_norm_w)
    y_g = _gdn(pg, gt, g_conv_w, g_a_log, g_dt_bias, g_norm_w)
    y_n = _nsa(pq, pkv, gt, n_qk_norm, n_cmp_pos, n_cmp_w1, n_cmp_w2)
    return _out_proj(x, mod, y_m, y_g, y_n, w_out)


def kernel(x, c, ada_w, ada_b, norm_w, ffn_w_up, ffn_w_down, w_in, w_out, mlstm_gate_b, mlstm_norm_w,
           gdn_conv_w, gdn_a_log, gdn_dt_bias, gdn_norm_w, nsa_qk_norm, nsa_cmp_pos, nsa_cmp_w1, nsa_cmp_w2):
    depth = ada_w.shape[0]
    b = x.shape[0]
    mods = _modulation(c, ada_w, ada_b).reshape(depth, b, 9, D_MODEL)
    for l in range(depth):
        mod = mods[l]
        x = _ffn(x, mod, norm_w[l, 0], ffn_w_up[l, 0].astype(BF16), ffn_w_down[l, 0].astype(BF16), 0)
        x = _mixer(x, mod, norm_w[l, 1], w_in[l], w_out[l], mlstm_gate_b[l], mlstm_norm_w[l],
                   gdn_conv_w[l], gdn_a_log[l], gdn_dt_bias[l], gdn_norm_w[l],
                   nsa_qk_norm[l], nsa_cmp_pos[l], nsa_cmp_w1[l], nsa_cmp_w2[l])
        x = _ffn(x, mod, norm_w[l, 2], ffn_w_up[l, 1].astype(BF16), ffn_w_down[l, 1].astype(BF16), 6)
    return x
```

```python
import functools

import jax
import jax.numpy as jnp
from jax import lax
from jax.experimental import pallas as pl
from jax.experimental.pallas import tpu as pltpu

F32 = jnp.float32
BF16 = jnp.bfloat16

D_MODEL = 1024
HEAD_DIM = 64
D_FF = 2816
EPS = 1e-6
NEG = -1e30
BIG = 1e30

M_HEADS = 4
G_HEADS = 4
N_HEADS = 8
N_KV_HEADS = 2
N_GROUP = N_HEADS // N_KV_HEADS
CHUNK = 64
GDN_PAR = 2
MLSTM_PAR = 2
CONV_K = 4
CMP_BLOCK = 32
CMP_STRIDE = 16
CMP_HIDDEN = 256
SLC_BLOCK = 64
N_SELECTED = 16
WINDOW = 512
Q_BLOCK = 128
ROPE_THETA = 10000.0

M_WIDTH = M_HEADS * HEAD_DIM
G_WIDTH = G_HEADS * HEAD_DIM
N_WIDTH = N_HEADS * HEAD_DIM
KV_WIDTH = N_KV_HEADS * HEAD_DIM
M_IN = 4 * M_WIDTH + 2 * M_HEADS
G_IN = 4 * G_WIDTH + 2 * G_HEADS
N_IN = N_WIDTH + 6 * KV_WIDTH + 3 * N_HEADS

LANES = 128
FF_TILE = 256
VMEM_LIMIT = 56 * 1024 * 1024

GT_M = 0
GT_G = 8
GT_N = 16


def _params(*sem):
    return pltpu.CompilerParams(dimension_semantics=sem, vmem_limit_bytes=VMEM_LIMIT)


def _dot(a, b):
    return jnp.dot(a, b, preferred_element_type=F32)


def _dot_nt(a, b):
    return lax.dot_general(a, b, (((1,), (1,)), ((), ())), preferred_element_type=F32)


def _dot_tn(a, b):
    return lax.dot_general(a, b, (((0,), (0,)), ((), ())), preferred_element_type=F32)


def _split_bf16(x, n):
    parts, r = [], x
    for _ in range(n):
        p = r.astype(BF16)
        parts.append(p)
        r = r - p.astype(F32)
    return parts


def _dot01_l(m01, x, n=3):
    acc = None
    for p in _split_bf16(x, n):
        t = _dot(m01, p)
        acc = t if acc is None else acc + t
    return acc


def _dot01_r(x, m01, n=3):
    acc = None
    for p in _split_bf16(x, n):
        t = _dot(p, m01)
        acc = t if acc is None else acc + t
    return acc


def _iota(shape, dim):
    return lax.broadcasted_iota(jnp.int32, shape, dim)


def _ind(cond, dtype=F32):
    return jnp.where(cond, 1.0, 0.0).astype(dtype)


def _softplus(x):
    return jnp.maximum(x, 0.0) + jnp.log1p(jnp.exp(-jnp.abs(x)))


def _group_ones(width):
    r, c = _iota((width, width), 0), _iota((width, width), 1)
    return _ind((r >> 6) == (c >> 6), BF16)


def _rope_lanes(x, cosf, sinf):
    lane = _iota(x.shape, 1)
    partner = jnp.where((lane & 63) < 32, pltpu.roll(x, 96, 1), pltpu.roll(x, 32, 1))
    return x * cosf + partner * sinf


def _mod_kernel(ct_ref, w_ref, b_ref, o_ref):
    ct = ct_ref[...]
    act = ct * jax.nn.sigmoid(ct)
    w = w_ref[0]
    rows = [jnp.sum(w * act[:, b:b + 1], axis=0, keepdims=True) for b in range(ct.shape[1])]
    o_ref[0] = jnp.concatenate(rows, axis=0) + b_ref[0]


def _modulation(c, ada_w, ada_b):
    depth, d, n = ada_w.shape
    b = c.shape[0]
    tn = 1024
    return pl.pallas_call(
        _mod_kernel,
        grid=(depth, n // tn),
        in_specs=[pl.BlockSpec((d, b), lambda l, j: (0, 0)),
                  pl.BlockSpec((1, d, tn), lambda l, j: (l, 0, j)),
                  pl.BlockSpec((1, 1, tn), lambda l, j: (l, 0, j))],
        out_specs=pl.BlockSpec((1, b, tn), lambda l, j: (l, 0, j)),
        out_shape=jax.ShapeDtypeStruct((depth, b, n), F32),
        compiler_params=_params("parallel", "parallel"),
        name="adaln_mod",
    )(c.T, ada_w, ada_b.reshape(depth, 1, n))


def _adaln(x, nw, shift, scale):
    y = x * lax.rsqrt(jnp.mean(x * x, axis=-1, keepdims=True) + EPS) * nw
    return y * (1.0 + scale) + shift


def _ffn_kernel(x_ref, mod_ref, nw_ref, wup_ref, wdn_ref, o_ref, *, row0, res_w):
    x = x_ref[0]
    shift = mod_ref[0, row0:row0 + 1, :]
    scale = mod_ref[0, row0 + 1:row0 + 2, :]
    gate = mod_ref[0, row0 + 2:row0 + 3, :]
    h = _adaln(x, nw_ref[...], shift, scale).astype(BF16)
    acc = jnp.zeros(x.shape, F32)
    for c in range(D_FF // FF_TILE):
        lo = c * FF_TILE
        g = _dot(h, wup_ref[:, lo:lo + FF_TILE])
        u = _dot(h, wup_ref[:, D_FF + lo:D_FF + lo + FF_TILE])
        a = (jax.nn.silu(g) * u).astype(BF16)
        acc = acc + _dot(a, wdn_ref[lo:lo + FF_TILE, :])
    o_ref[0] = x + (res_w * gate) * acc


def _ffn(x, mod, nw, w_up, w_down, row0):
    b, t, d = x.shape
    tm = min(512, t)
    kern = functools.partial(_ffn_kernel, row0=row0, res_w=0.5)
    return pl.pallas_call(
        kern,
        grid=(b, t // tm),
        in_specs=[pl.BlockSpec((1, tm, d), lambda i, j: (i, j, 0)),
                  pl.BlockSpec((1, 9, d), lambda i, j: (i, 0, 0)),
                  pl.BlockSpec((1, d), lambda i, j: (0, 0)),
                  pl.BlockSpec((d, 2 * D_FF), lambda i, j: (0, 0), pipeline_mode=pl.Buffered(1)),
                  pl.BlockSpec((D_FF, d), lambda i, j: (0, 0), pipeline_mode=pl.Buffered(1))],
        out_specs=pl.BlockSpec((1, tm, d), lambda i, j: (i, j, 0)),
        out_shape=jax.ShapeDtypeStruct(x.shape, F32),
        compiler_params=_params("parallel", "parallel"),
        name="ffn",
    )(x, mod, nw.reshape(1, d), w_up, w_down)


def _inproj_kernel(x_ref, mod_ref, nw_ref, w_ref, pm_ref, pg_ref, pq_ref, pkv_ref, gt_ref):
    x = x_ref[0]
    h = _adaln(x, nw_ref[...], mod_ref[0, 3:4, :], mod_ref[0, 4:5, :]).astype(BF16)
    off = 0
    for ref in (pm_ref, pg_ref, pq_ref, pkv_ref, gt_ref):
        wd = ref.shape[-1]
        ref[0] = _dot(h, w_ref[:, off:off + wd])
        off += wd


IN_PERM_WIDTH = 4 * M_WIDTH + 4 * G_WIDTH + N_WIDTH + 6 * KV_WIDTH + LANES


def _permute_w_in(w_in):
    g0, n0 = M_IN, M_IN + G_IN
    gates = jnp.concatenate([w_in[:, 4 * M_WIDTH:M_IN], w_in[:, g0 + 4 * G_WIDTH:g0 + G_IN],
                             w_in[:, n0 + N_WIDTH + 6 * KV_WIDTH:n0 + N_IN]], axis=1)
    gates = jnp.pad(gates, ((0, 0), (0, LANES - gates.shape[1])))
    return jnp.concatenate([w_in[:, :4 * M_WIDTH], w_in[:, g0:g0 + 4 * G_WIDTH],
                            w_in[:, n0:n0 + N_WIDTH + 6 * KV_WIDTH], gates], axis=1).astype(BF16)


def _in_proj(x, mod, nw, w_perm):
    b, t, d = x.shape
    tm = min(512, t)
    widths = (4 * M_WIDTH, 4 * G_WIDTH, N_WIDTH, 6 * KV_WIDTH, LANES)
    return pl.pallas_call(
        _inproj_kernel,
        grid=(b, t // tm),
        in_specs=[pl.BlockSpec((1, tm, d), lambda i, j: (i, j, 0)),
                  pl.BlockSpec((1, 9, d), lambda i, j: (i, 0, 0)),
                  pl.BlockSpec((1, d), lambda i, j: (0, 0)),
                  pl.BlockSpec((d, IN_PERM_WIDTH), lambda i, j: (0, 0), pipeline_mode=pl.Buffered(1))],
        out_specs=[pl.BlockSpec((1, tm, w), lambda i, j: (i, j, 0)) for w in widths],
        out_shape=[jax.ShapeDtypeStruct((b, t, w), F32) for w in widths],
        compiler_params=_params("parallel", "parallel"),
        name="mixer_in_proj",
    )(x, mod, nw.reshape(1, d), w_perm)


def _outproj_kernel(x_ref, mod_ref, ym_ref, yg_ref, yn_ref, w_ref, o_ref):
    y = jnp.concatenate([ym_ref[0], yg_ref[0], yn_ref[0]], axis=-1).astype(BF16)
    o_ref[0] = x_ref[0] + mod_ref[0, 5:6, :] * _dot(y, w_ref[...])


def _out_proj(x, mod, y_m, y_g, y_n, w_out):
    b, t, d = x.shape
    tm = min(512, t)
    row = lambda w: pl.BlockSpec((1, tm, w), lambda i, j: (i, j, 0))
    return pl.pallas_call(
        _outproj_kernel,
        grid=(b, t // tm),
        in_specs=[row(d), pl.BlockSpec((1, 9, d), lambda i, j: (i, 0, 0)),
                  row(M_WIDTH), row(G_WIDTH), row(N_WIDTH),
                  pl.BlockSpec((d, d), lambda i, j: (0, 0), pipeline_mode=pl.Buffered(1))],
        out_specs=row(d),
        out_shape=jax.ShapeDtypeStruct(x.shape, F32),
        compiler_params=_params("parallel", "parallel"),
        name="mixer_out_proj",
    )(x, mod, y_m, y_g, y_n, w_out.astype(BF16))


def _chunk_masks():
    r, c = _iota((CHUNK, CHUNK), 0), _iota((CHUNK, CHUNK), 1)
    return r, c


def _mlstm_kernel(pm_ref, gt_ref, gb_ref, nw_ref, o_ref, c_scr, m_scr, *, n_chunks):
    @pl.when(pl.program_id(1) == 0)
    def _():
        c_scr[...] = jnp.zeros(c_scr.shape, F32)
        m_scr[...] = jnp.zeros(m_scr.shape, F32)

    r, c = _chunk_masks()
    causal = r >= c
    low = _ind(causal, BF16)
    low_ones = jnp.concatenate([low, jnp.ones((CHUNK, CHUNK), BF16)], axis=1)
    strict_up = _ind(r > c)
    eye = _ind(r == c)
    nw = nw_ref[...]
    gb = gb_ref[...]

    ones_v = jnp.ones((CHUNK, HEAD_DIM), BF16)
    heads = range(M_HEADS)

    def chunk_group(cp, carry):
        rows = [pl.ds(pl.multiple_of((cp * MLSTM_PAR + j) * CHUNK, CHUNK), CHUNK) for j in range(MLSTM_PAR)]
        items = [(j, h) for j in range(MLSTM_PAR) for h in heads]
        every = lambda f, *cols: [f(*args) for args in zip(*cols)]
        g_blk = [gt_ref[0, rw, :] + gb for rw in rows]
        lf_blk = [-_softplus(-g) for g in g_blk]
        cum_blk = [_dot01_l(low, x) for x in lf_blk]
        pick = lambda blk, lane: [blk[j][:, lane + h:lane + h + 1] for j, h in items]
        li_c, lf_c, cum_c = pick(g_blk, GT_M), pick(lf_blk, GT_M + M_HEADS), pick(cum_blk, GT_M + M_HEADS)
        head = lambda k: [pm_ref[0, rows[j], k * M_WIDTH + h * HEAD_DIM:k * M_WIDTH + (h + 1) * HEAD_DIM]
                          for j, h in items]
        q = every(lambda x: (x * (HEAD_DIM ** -0.5)).astype(BF16), head(0))
        k = head(1)
        v_aug = every(lambda x: jnp.concatenate([x.astype(BF16), ones_v], axis=1), head(2))
        og = every(jax.nn.sigmoid, head(3))
        dmat = every(lambda f, i: _dot01_l(low_ones, jnp.concatenate([f * strict_up, i * eye], axis=0)), lf_c, li_c)
        dmat = every(lambda x: jnp.where(causal, x, NEG), dmat)
        d_max = every(lambda x: jnp.max(x, axis=-1, keepdims=True), dmat)
        qk = every(lambda a, b: _dot_nt(a, b.astype(BF16)), q, k)
        g_tot = every(lambda x: x[CHUNK - 1:CHUNK, :], cum_c)
        a = every(lambda gt, cm, li: gt - cm + li, g_tot, cum_c, li_c)
        m_loc = every(lambda x: jnp.max(x, axis=0, keepdims=True), a)
        kw = every(lambda kk, x, ml: (kk * jnp.exp(x - ml)).astype(BF16), k, a, m_loc)
        c_loc = every(_dot_tn, kw, v_aug)
        c_prev = [c_scr[h] for h in heads]
        m_prev = [m_scr[h] for h in heads]
        for j in range(MLSTM_PAR):
            sel = [j * M_HEADS + h for h in heads]
            m_inter = [cum_c[i] + m_prev[h] for h, i in enumerate(sel)]
            m_t = [jnp.maximum(m_inter[h], d_max[i]) for h, i in enumerate(sel)]
            s = [(qk[i] * jnp.exp(dmat[i] - m_t[h])).astype(BF16) for h, i in enumerate(sel)]
            inter = [jnp.exp(m_inter[h] - m_t[h]) for h in heads]
            nd = [_dot(s[h], v_aug[i]) + inter[h] * _dot(q[i], c_prev[h].astype(BF16)) for h, i in enumerate(sel)]
            den = [jnp.maximum(jnp.abs(nd[h]), jnp.exp(-m_t[h])) for h in heads]
            hh = [(nd[h] / pltpu.roll(den[h], HEAD_DIM, 1))[:, :HEAD_DIM] for h in heads]
            hh = [x * lax.rsqrt(jnp.mean(x * x, axis=-1, keepdims=True) + EPS) * nw for x in hh]
            m_new = [jnp.maximum(g_tot[i] + m_prev[h], m_loc[i]) for h, i in enumerate(sel)]
            s_old = [jnp.exp(g_tot[i] + m_prev[h] - m_new[h]) for h, i in enumerate(sel)]
            s_new = [jnp.exp(m_loc[i] - m_new[h]) for h, i in enumerate(sel)]
            for h, i in enumerate(sel):
                o_ref[0, rows[j], h * HEAD_DIM:(h + 1) * HEAD_DIM] = og[i] * hh[h]
            c_prev = [s_old[h] * c_prev[h] + s_new[h] * c_loc[i] for h, i in enumerate(sel)]
            m_prev = m_new
        for h in heads:
            c_scr[h] = c_prev[h]
            m_scr[h] = m_prev[h]
        return carry

    lax.fori_loop(0, n_chunks // MLSTM_PAR, chunk_group, 0)


def _mlstm(pm, gt, gate_b, norm_w):
    b, t, _ = pm.shape
    tc = min(512, t)
    gb = jnp.zeros((1, LANES), F32).at[0, GT_M:GT_M + 2 * M_HEADS].set(gate_b.reshape(-1))
    kern = functools.partial(_mlstm_kernel, n_chunks=tc // CHUNK)
    return pl.pallas_call(
        kern,
        grid=(b, t // tc),
        in_specs=[pl.BlockSpec((1, tc, 4 * M_WIDTH), lambda i, j: (i, j, 0)),
                  pl.BlockSpec((1, tc, LANES), lambda i, j: (i, j, 0)),
                  pl.BlockSpec((1, LANES), lambda i, j: (0, 0)),
                  pl.BlockSpec((1, HEAD_DIM), lambda i, j: (0, 0))],
        out_specs=pl.BlockSpec((1, tc, M_WIDTH), lambda i, j: (i, j, 0)),
        out_shape=jax.ShapeDtypeStruct((b, t, M_WIDTH), F32),
        scratch_shapes=[pltpu.VMEM((M_HEADS, HEAD_DIM, 2 * HEAD_DIM), F32),
                        pltpu.VMEM((M_HEADS, 1, 1), F32)],
        compiler_params=_params("parallel", "arbitrary"),
        name="mlstm",
    )(pm, gt, gb, norm_w.reshape(1, HEAD_DIM))


def _gdn_kernel(pg_ref, gt_ref, cw_ref, hp_ref, nw_ref, o_ref, xbuf, qkv_scr, la_scr, s_scr,
                lin_scr, add_scr, g_scr, *, tc):
    first = pl.program_id(1) == 0

    @pl.when(first)
    def _():
        s_scr[...] = jnp.zeros(s_scr.shape, F32)
        xbuf[0:8, :] = jnp.zeros((8, 3 * G_WIDTH), F32)

    @pl.when(jnp.logical_not(first))
    def _():
        xbuf[0:8, :] = xbuf[tc:tc + 8, :]

    xbuf[8:8 + tc, :] = pg_ref[0, :, 0:3 * G_WIDTH]
    acc = None
    for kk in range(CONV_K):
        term = xbuf[8 - (CONV_K - 1) + kk:8 - (CONV_K - 1) + kk + tc, :] * cw_ref[kk:kk + 1, :]
        acc = term if acc is None else acc + term
    act = acc * jax.nn.sigmoid(acc)
    gones = _group_ones(G_WIDTH)
    for part in range(2):
        xx = act[:, part * G_WIDTH:(part + 1) * G_WIDTH]
        ssq = _dot01_r(xx * xx, gones, 2)
        xx = xx * lax.rsqrt(ssq + EPS)
        if part == 0:
            xx = xx * (HEAD_DIM ** -0.5)
        qkv_scr[:, part * G_WIDTH:(part + 1) * G_WIDTH] = xx
    qkv_scr[:, 2 * G_WIDTH:3 * G_WIDTH] = act[:, 2 * G_WIDTH:3 * G_WIDTH]
    gate = gt_ref[0]
    neg_rate = -jnp.exp(hp_ref[0:1, :])
    la_scr[...] = neg_rate * _softplus(gate + hp_ref[1:2, :])

    r, c = _chunk_masks()
    low = _ind(r >= c, BF16)
    strict_low = r > c
    incl_low = r >= c
    strict_up = _ind(r > c)
    eye = _ind(r == c)
    base_mask = _ind(((r >> 2) == (c >> 2)) & (r > c))
    merge_masks = [_ind((((r >> s) & 1) == 1) & ((c >> s) == (r >> s) - 1)) for s in (2, 3, 4, 5)]
    nw = nw_ref[...]

    def prepare(cp, carry):
        items = [(cp * GDN_PAR + j, h) for j in range(GDN_PAR) for h in range(G_HEADS)]
        every = lambda f, *cols: [f(*args) for args in zip(*cols)]
        rows = [pl.ds(pl.multiple_of((cp * GDN_PAR + j) * CHUNK, CHUNK), CHUNK) for j in range(GDN_PAR)]
        la_blk = [la_scr[rw, :] for rw in rows]
        gam_blk = [_dot01_l(low, x) for x in la_blk]
        beta_blk = [jax.nn.sigmoid(gt_ref[0, rw, :]) for rw in rows]
        pick = lambda blk, lane: [blk[i // G_HEADS][:, lane + h:lane + h + 1] for i, (_, h) in enumerate(items)]
        la_c, gam_c, beta_c = pick(la_blk, GT_G), pick(gam_blk, GT_G), pick(beta_blk, GT_G + G_HEADS)
        head = lambda k: [qkv_scr[rows[i // G_HEADS], k * G_WIDTH + h * HEAD_DIM:k * G_WIDTH + (h + 1) * HEAD_DIM]
                          for i, (_, h) in enumerate(items)]
        q, k, v = head(0), head(1), head(2)
        diff = every(lambda x: _dot01_l(low, x * strict_up), la_c)
        dec_strict = every(lambda x: jnp.exp(jnp.where(strict_low, x, NEG)), diff)
        dec_incl = every(lambda x: jnp.exp(jnp.where(incl_low, x, NEG)), diff)
        kb = every(lambda x: x.astype(BF16), k)
        qk_kk = every(lambda a, b: _dot_nt(jnp.concatenate([a.astype(BF16), b], axis=0), b), q, kb)
        amat = every(lambda b, x, dcy: b * x[CHUNK:] * dcy, beta_c, qk_kk, dec_strict)
        n0 = every(lambda a: (-(a * base_mask)).astype(BF16), amat)
        n0sq = every(lambda n: _dot(n, n).astype(BF16), n0)
        inv = every(lambda n: eye + n.astype(F32), n0)
        inv = every(lambda t, n2: t + _dot(t.astype(BF16), n2), inv, n0sq)
        for mask in merge_masks:
            invb = every(lambda t: t.astype(BF16), inv)
            half = every(lambda tb, a: _dot(tb, (a * mask).astype(BF16)).astype(BF16), invb, amat)
            inv = every(lambda t, hf, tb: t - _dot(hf, tb), inv, half, invb)
        e_gam = every(jnp.exp, gam_c)
        rhs = every(lambda b, e, vv, kk: jnp.concatenate([b * vv, (b * e) * kk], axis=1).astype(BF16),
                    beta_c, e_gam, v, k)
        sol = every(lambda t, x: _dot(t.astype(BF16), x).astype(BF16), inv, rhs)
        gam_last = every(lambda g: g[CHUNK - 1:CHUNK, :], gam_c)
        k_d = every(lambda kk, gl, g: (kk * jnp.exp(gl - g)).astype(BF16), k, gam_last, gam_c)
        p_sol = every(lambda x, dcy, sl: _dot((x[:CHUNK] * dcy).astype(BF16), sl), qk_kk, dec_incl, sol)
        kd_sol = every(_dot_tn, k_d, sol)
        for i, (ci, h) in enumerate(items):
            lin_scr[ci, h] = jnp.concatenate([kd_sol[i][:, HEAD_DIM:], q[i] * e_gam[i] - p_sol[i][:, HEAD_DIM:]],
                                             axis=0).astype(BF16)
            add_scr[ci, h] = jnp.concatenate([kd_sol[i][:, :HEAD_DIM], p_sol[i][:, :HEAD_DIM]], axis=0)
            g_scr[ci, h] = jnp.broadcast_to(jnp.exp(gam_last[i]), (1, HEAD_DIM))
        return carry

    lax.fori_loop(0, tc // (CHUNK * GDN_PAR), prepare, 0)

    def advance(ci, carry):
        r0 = pl.multiple_of(ci * CHUNK, CHUNK)
        heads = range(G_HEADS)
        s_prev = [s_scr[h] for h in heads]
        prod = [_dot(lin_scr[ci, h], s_prev[h].astype(BF16)) for h in heads]
        add = [add_scr[ci, h] for h in heads]
        s_next = [g_scr[ci, h] * s_prev[h] - prod[h][:CHUNK] + add[h][:CHUNK] for h in heads]
        o = [prod[h][CHUNK:] + add[h][CHUNK:] for h in heads]
        o = [x * lax.rsqrt(jnp.mean(x * x, axis=-1, keepdims=True) + EPS) * nw for x in o]
        z = [pg_ref[0, pl.ds(r0, CHUNK), 3 * G_WIDTH + h * HEAD_DIM:3 * G_WIDTH + (h + 1) * HEAD_DIM] for h in heads]
        for h in heads:
            s_scr[h] = s_next[h]
            o_ref[0, pl.ds(r0, CHUNK), h * HEAD_DIM:(h + 1) * HEAD_DIM] = o[h] * (z[h] * jax.nn.sigmoid(z[h]))
        return carry

    lax.fori_loop(0, tc // CHUNK, advance, 0)


def _gdn(pg, gt, conv_w, a_log, dt_bias, norm_w):
    b, t, _ = pg.shape
    tc = min(512, t)
    hp = jnp.zeros((2, LANES), F32)
    hp = hp.at[0, GT_G:GT_G + G_HEADS].set(a_log).at[1, GT_G:GT_G + G_HEADS].set(dt_bias)
    kern = functools.partial(_gdn_kernel, tc=tc)
    return pl.pallas_call(
        kern,
        grid=(b, t // tc),
        in_specs=[pl.BlockSpec((1, tc, 4 * G_WIDTH), lambda i, j: (i, j, 0)),
                  pl.BlockSpec((1, tc, LANES), lambda i, j: (i, j, 0)),
                  pl.BlockSpec((CONV_K, 3 * G_WIDTH), lambda i, j: (0, 0)),
                  pl.BlockSpec((2, LANES), lambda i, j: (0, 0)),
                  pl.BlockSpec((1, HEAD_DIM), lambda i, j: (0, 0))],
        out_specs=pl.BlockSpec((1, tc, G_WIDTH), lambda i, j: (i, j, 0)),
        out_shape=jax.ShapeDtypeStruct((b, t, G_WIDTH), F32),
        scratch_shapes=[pltpu.VMEM((tc + 8, 3 * G_WIDTH), F32),
                        pltpu.VMEM((tc, 3 * G_WIDTH), F32),
                        pltpu.VMEM((tc, LANES), F32),
                        pltpu.VMEM((G_HEADS, HEAD_DIM, HEAD_DIM), F32),
                        pltpu.VMEM((tc // CHUNK, G_HEADS, 2 * CHUNK, HEAD_DIM), BF16),
                        pltpu.VMEM((tc // CHUNK, G_HEADS, 2 * CHUNK, HEAD_DIM), F32),
                        pltpu.VMEM((tc // CHUNK, G_HEADS, 1, HEAD_DIM), F32)],
        compiler_params=_params("parallel", "arbitrary"),
        name="gdn",
    )(pg, gt, conv_w.astype(F32), hp, norm_w.reshape(1, HEAD_DIM))


def _rope_tables(pos):
    half = HEAD_DIM // 2
    inv_freq = jnp.power(ROPE_THETA, -jnp.arange(half, dtype=F32) / half)
    ang = pos.astype(F32)[:, None] * inv_freq[None, :]
    cos, sin = jnp.cos(ang), jnp.sin(ang)
    cosf = jnp.tile(cos, (1, LANES // half))
    sinf = jnp.tile(jnp.concatenate([-sin, sin], axis=1), (1, LANES // HEAD_DIM))
    return cosf, sinf


KS_AUG = LANES + HEAD_DIM


def _nsa_prep_kernel(pq_ref, pkv_ref, cos_ref, sin_ref, nw_ref, q_ref, ks_ref, vs_ref, kw_ref, vw_ref):
    cosf, sinf = cos_ref[...], sin_ref[...]
    gones = _group_ones(LANES)

    def norm_rope(x, w):
        ssq = _dot01_r(x * x, gones, 2)
        y = x * lax.rsqrt(ssq * (1.0 / HEAD_DIM) + EPS) * w
        return _rope_lanes(y, cosf, sinf)

    def put_heads(ref, slab, h0):
        ref[0, h0] = slab[:, :HEAD_DIM].astype(BF16)
        ref[0, h0 + 1] = slab[:, HEAD_DIM:].astype(BF16)

    for s in range(N_WIDTH // LANES):
        qs = norm_rope(pq_ref[0, :, s * LANES:(s + 1) * LANES], nw_ref[0:1, :]) * (HEAD_DIM ** -0.5)
        put_heads(q_ref, qs, 2 * s)
    kv = lambda i: pkv_ref[0, :, i * KV_WIDTH:(i + 1) * KV_WIDTH]
    tm = cosf.shape[0]
    tok = pl.program_id(1) * tm + _iota((tm, LANES), 0)
    onehot = _ind((tok >> 6) == _iota((tm, LANES), 1), BF16)
    k_slc = norm_rope(kv(2), nw_ref[2:3, :])
    for h in range(N_KV_HEADS):
        ks_ref[0, h, :, 0:LANES] = onehot
        ks_ref[0, h, :, LANES:LANES + HEAD_DIM] = k_slc[:, h * HEAD_DIM:(h + 1) * HEAD_DIM].astype(BF16)
    put_heads(kw_ref, norm_rope(kv(4), nw_ref[3:4, :]), 0)
    lane = _iota((tm, LANES), 1)
    for ref, slab in ((vs_ref, kv(3)), (vw_ref, kv(5))):
        ref[0, 0] = jnp.where(lane < HEAD_DIM, slab, 1.0).astype(BF16)
        ref[0, 1] = jnp.where(lane < HEAD_DIM, pltpu.roll(slab, HEAD_DIM, 1), 1.0).astype(BF16)


def _nsa_prep(pq, pkv, qk_norm):
    b, t, _ = pq.shape
    assert t // SLC_BLOCK <= LANES
    tm = min(512, t)
    cosf, sinf = _rope_tables(jnp.arange(t, dtype=jnp.int32))
    nw = jnp.tile(qk_norm, (1, LANES // HEAD_DIM))
    heads = lambda n: pl.BlockSpec((1, n, tm, HEAD_DIM), lambda i, j: (i, 0, j, 0))
    wide = lambda w: pl.BlockSpec((1, N_KV_HEADS, tm, w), lambda i, j: (i, 0, j, 0))
    shp = lambda n, w: jax.ShapeDtypeStruct((b, n, t, w), BF16)
    return pl.pallas_call(
        _nsa_prep_kernel,
        grid=(b, t // tm),
        in_specs=[pl.BlockSpec((1, tm, N_WIDTH), lambda i, j: (i, j, 0)),
                  pl.BlockSpec((1, tm, 6 * KV_WIDTH), lambda i, j: (i, j, 0)),
                  pl.BlockSpec((tm, LANES), lambda i, j: (j, 0)),
                  pl.BlockSpec((tm, LANES), lambda i, j: (j, 0)),
                  pl.BlockSpec((4, LANES), lambda i, j: (0, 0))],
        out_specs=[heads(N_HEADS), wide(KS_AUG), wide(LANES), heads(N_KV_HEADS), wide(LANES)],
        out_shape=[shp(N_HEADS, HEAD_DIM), shp(N_KV_HEADS, KS_AUG), shp(N_KV_HEADS, LANES),
                   shp(N_KV_HEADS, HEAD_DIM), shp(N_KV_HEADS, LANES)],
        compiler_params=_params("parallel", "parallel"),
        name="nsa_prep",
    )(pq, pkv, cosf, sinf, nw)


def _compress_kernel(x_ref, pos_ref, w1_ref, w2_ref, cos_ref, sin_ref, nw_ref, o_ref):
    x = x_ref[0, 0, 0]
    n16 = x.shape[0]
    half = CMP_STRIDE * HEAD_DIM
    a = _dot((x + pos_ref[0, 0:1, :]).astype(BF16), w1_ref[0, 0:half, :])
    bmat = _dot((x + pos_ref[0, 1:2, :]).astype(BF16), w1_ref[0, half:2 * half, :])
    nxt = jnp.concatenate([bmat[1:, :], jnp.zeros((1, CMP_HIDDEN), F32)], axis=0)
    hid = a + nxt
    hid = hid * jax.nn.sigmoid(hid)
    out = _dot(hid.astype(BF16), w2_ref[0])
    ssq = jnp.sum(out * out, axis=-1, keepdims=True)
    normed = out * lax.rsqrt(ssq * (1.0 / HEAD_DIM) + EPS) * nw_ref[...]
    roped = _rope_lanes(normed, cos_ref[...], sin_ref[...])
    is_key = pl.program_id(1) == 0
    o_ref[0, 0, 0] = jnp.where(is_key, roped, out)
    del n16


def _nsa_compress(pkv, cmp_pos, cmp_w1, cmp_w2, kc_norm):
    b, t, _ = pkv.shape
    n16 = t // CMP_STRIDE
    flat = CMP_STRIDE * HEAD_DIM
    x = pkv[..., :2 * KV_WIDTH].reshape(b, n16, CMP_STRIDE, 2, N_KV_HEADS, HEAD_DIM)
    x = x.transpose(0, 3, 4, 1, 2, 5).reshape(b, 2, N_KV_HEADS, n16, flat)
    pos = cmp_pos.reshape(2, 2, flat)
    w1 = cmp_w1.astype(BF16)
    w2 = jnp.pad(cmp_w2, ((0, 0), (0, 0), (0, LANES - HEAD_DIM))).astype(BF16)
    cosf, sinf = _rope_tables(jnp.arange(n16, dtype=jnp.int32) * CMP_STRIDE + CMP_BLOCK - 1)
    nw = jnp.pad(kc_norm.reshape(1, HEAD_DIM), ((0, 0), (0, LANES - HEAD_DIM)))
    return pl.pallas_call(
        _compress_kernel,
        grid=(b, 2, N_KV_HEADS),
        in_specs=[pl.BlockSpec((1, 1, 1, n16, flat), lambda i, s, h: (i, s, h, 0, 0)),
                  pl.BlockSpec((1, 2, flat), lambda i, s, h: (s, 0, 0)),
                  pl.BlockSpec((1, 2 * flat, CMP_HIDDEN), lambda i, s, h: (s, 0, 0)),
                  pl.BlockSpec((1, CMP_HIDDEN, LANES), lambda i, s, h: (s, 0, 0)),
                  pl.BlockSpec((n16, LANES), lambda i, s, h: (0, 0)),
                  pl.BlockSpec((n16, LANES), lambda i, s, h: (0, 0)),
                  pl.BlockSpec((1, LANES), lambda i, s, h: (0, 0))],
        out_specs=pl.BlockSpec((1, 1, 1, n16, LANES), lambda i, s, h: (i, s, h, 0, 0)),
        out_shape=jax.ShapeDtypeStruct((b, 2, N_KV_HEADS, n16, LANES), F32),
        compiler_params=_params("parallel", "parallel", "parallel"),
        name="nsa_compress",
    )(x, pos, w1, w2, cosf, sinf, nw)


SLC_TILE = 512
SLC_SUB = 512
WIN_SPAN = Q_BLOCK + WINDOW


def _nsa_attn_kernel(q_ref, cmp_ref, ks_ref, vs_ref, kw_ref, vw_ref, gt_ref, o_ref, *, n_slc):
    hkv = pl.program_id(1)
    qi = pl.program_id(2)
    start = qi * Q_BLOCK
    rows = N_GROUP * Q_BLOCK
    q = q_ref[0].reshape(rows, HEAD_DIM)
    n16 = cmp_ref.shape[3]

    def tq_of(shape):
        return start + (_iota(shape, 0) & (Q_BLOCK - 1))

    kc = cmp_ref[0, 0, 0][:, :HEAD_DIM].astype(BF16)
    vc = cmp_ref[0, 1, 0][:, :HEAD_DIM].astype(BF16)
    s_c = _dot_nt(q, kc)
    tq_c = tq_of(s_c.shape)
    valid_c = (_iota(s_c.shape, 1) * CMP_STRIDE + (CMP_BLOCK - 1)) <= tq_c
    s_c = jnp.where(valid_c, s_c, NEG)
    e_c = jnp.exp(s_c - jnp.max(s_c, axis=-1, keepdims=True))
    p_c = e_c / jnp.sum(e_c, axis=-1, keepdims=True)
    p_c = jnp.where(tq_c >= CMP_BLOCK - 1, p_c, 0.0)
    o_c = _dot(p_c.astype(BF16), vc)

    p_sum = p_c[0:Q_BLOCK]
    for g in range(1, N_GROUP):
        p_sum = p_sum + p_c[g * Q_BLOCK:(g + 1) * Q_BLOCK]
    sb, cn = _iota((n_slc, n16), 0), _iota((n_slc, n16), 1)
    c_lo, s_lo = cn * CMP_STRIDE, sb * SLC_BLOCK
    ov = jnp.maximum(jnp.minimum(c_lo + CMP_BLOCK, s_lo + SLC_BLOCK) - jnp.maximum(c_lo, s_lo), 0)
    ov = jnp.where(cn < n16 - 1, ov, 0)
    overlap_t = (ov.astype(F32) * (1.0 / CMP_BLOCK)).astype(BF16)
    imp = None
    for part in _split_bf16(p_sum, 2):
        term = _dot_nt(overlap_t, part)
        imp = term if imp is None else imp + term
    tq_i = start + _iota(imp.shape, 1)
    blk = _iota(imp.shape, 0)
    cur = tq_i >> 6
    forced = (blk == 0) | (blk == cur) | (blk == cur - 1)
    imp = jnp.where(forced, BIG, imp)
    imp = jnp.where(blk * SLC_BLOCK <= tq_i, imp, NEG)
    blk_f = blk.astype(F32)
    sel_t = jnp.zeros(imp.shape, F32)
    for _ in range(min(N_SELECTED, n_slc)):
        top = jnp.max(imp, axis=0, keepdims=True)
        first = jnp.min(jnp.where(imp == top, blk_f, float(n_slc)), axis=0, keepdims=True)
        hit = blk_f == first
        sel_t = jnp.where(hit, 1.0, sel_t)
        imp = jnp.where(hit, -3e38, imp)

    own = pl.multiple_of(start, Q_BLOCK)
    kd = ks_ref[0, 0, pl.ds(own, Q_BLOCK), LANES:LANES + HEAD_DIM]
    vd = vs_ref[0, 0, pl.ds(own, Q_BLOCK), :]
    s_d = _dot_nt(q, kd)
    s_d = jnp.where(_iota(s_d.shape, 1) <= (_iota(s_d.shape, 0) & (Q_BLOCK - 1)), s_d, NEG)
    m_d = jnp.max(s_d, axis=-1, keepdims=True)
    init = (m_d, _dot(jnp.exp(s_d - m_d).astype(BF16), vd))

    bias_t = (jnp.where(blk < 2 * qi, sel_t, 0.0) - 1.0) * BIG
    if n_slc < LANES:
        bias_t = jnp.concatenate([bias_t, jnp.zeros((LANES - n_slc, Q_BLOCK), F32)], axis=0)
    eye = _ind(_iota((LANES, LANES), 0) == _iota((LANES, LANES), 1), BF16)
    bias = _dot_tn(bias_t.astype(BF16), eye).astype(BF16)
    q_aug = jnp.concatenate([jnp.concatenate([bias] * N_GROUP, axis=0), q], axis=1)

    def slc_step(kt, carry):
        m, acc = carry
        for part in range(SLC_TILE // SLC_SUB):
            k0 = pl.multiple_of(kt * SLC_TILE + part * SLC_SUB, SLC_SUB)
            s = _dot_nt(q_aug, ks_ref[0, 0, pl.ds(k0, SLC_SUB), :])
            m_new = jnp.maximum(m, jnp.max(s, axis=-1, keepdims=True))
            p = jnp.exp(s - m_new).astype(BF16)
            acc = jnp.exp(m - m_new) * acc + _dot(p, vs_ref[0, 0, pl.ds(k0, SLC_SUB), :])
            m = m_new
        return m, acc

    n_tiles = (start + SLC_TILE - 1) // SLC_TILE
    _, acc_s = lax.fori_loop(0, n_tiles, slc_step, init)
    o_s = (acc_s / pltpu.roll(acc_s, HEAD_DIM, 1))[:, :HEAD_DIM]

    w0 = pl.multiple_of(jnp.maximum(start - WINDOW, 0), Q_BLOCK)
    kw = kw_ref[0, 0, pl.ds(w0, WIN_SPAN), :]
    vw = vw_ref[0, 0, pl.ds(w0, WIN_SPAN), :]
    s_w = _dot_nt(q, kw)
    dist = tq_of(s_w.shape) - (w0 + _iota(s_w.shape, 1))
    s_w = jnp.where((dist >= 0) & (dist < WINDOW), s_w, NEG)
    e_w = jnp.exp(s_w - jnp.max(s_w, axis=-1, keepdims=True))
    acc_w = _dot(e_w.astype(BF16), vw)
    o_w = (acc_w / pltpu.roll(acc_w, HEAD_DIM, 1))[:, :HEAD_DIM]

    gates = jax.nn.sigmoid(gt_ref[0])
    outs = []
    for g in range(N_GROUP):
        def gate_col(branch):
            c0 = GT_N + branch * N_HEADS + g
            c1 = c0 + N_GROUP
            return jnp.where(hkv == 0, gates[:, c0:c0 + 1], gates[:, c1:c1 + 1])
        sl = slice(g * Q_BLOCK, (g + 1) * Q_BLOCK)
        outs.append(gate_col(0) * o_c[sl] + gate_col(1) * o_s[sl] + gate_col(2) * o_w[sl])
    o_ref[0] = jnp.concatenate(outs, axis=-1)


def _nsa_attention(qh, cmp, ks, vs, kw, vw, gt):
    b, _, t, _ = qh.shape
    n16 = cmp.shape[3]
    n_slc = t // SLC_BLOCK
    full = lambda w: pl.BlockSpec((1, 1, t, w), lambda i, h, j: (i, h, 0, 0))
    kern = functools.partial(_nsa_attn_kernel, n_slc=n_slc)
    return pl.pallas_call(
        kern,
        grid=(b, N_KV_HEADS, t // Q_BLOCK),
        in_specs=[pl.BlockSpec((1, N_GROUP, Q_BLOCK, HEAD_DIM), lambda i, h, j: (i, h, j, 0)),
                  pl.BlockSpec((1, 2, 1, n16, LANES), lambda i, h, j: (i, 0, h, 0, 0)),
                  full(KS_AUG), full(LANES), full(HEAD_DIM), full(LANES),
                  pl.BlockSpec((1, Q_BLOCK, LANES), lambda i, h, j: (i, j, 0))],
        out_specs=pl.BlockSpec((1, Q_BLOCK, N_GROUP * HEAD_DIM), lambda i, h, j: (i, j, h)),
        out_shape=jax.ShapeDtypeStruct((b, t, N_WIDTH), F32),
        compiler_params=_params("parallel", "parallel", "arbitrary"),
        name="nsa_attention",
    )(qh, cmp, ks, vs, kw, vw, gt)


def _nsa(pq, pkv, gt, qk_norm, cmp_pos, cmp_w1, cmp_w2):
    qh, ks, vs, kw, vw = _nsa_prep(pq, pkv, qk_norm)
    cmp = _nsa_compress(pkv, cmp_pos, cmp_w1, cmp_w2, qk_norm[1])
    return _nsa_attention(qh, cmp, ks, vs, kw, vw, gt)


def _mixer(x, mod, nw, w_in, w_out, m_gate_b, m_norm_w, g_conv_w, g_a_log, g_dt_bias, g_norm_w,
           n_qk_norm, n_cmp_pos, n_cmp_w1, n_cmp_w2):
    pm, pg, pq, pkv, gt = _in_proj(x, mod, nw, _permute_w_in(w_in))
    y_m = _mlstm(pm, gt, m_gate_b, m_norm_w)
    y_g = _gdn(pg, gt, g_conv_w, g_a_log, g_dt_bias, g_norm_w)
    y_n = _nsa(pq, pkv, gt, n_qk_norm, n_cmp_pos, n_cmp_w1, n_cmp_w2)
    return _out_proj(x, mod, y_m, y_g, y_n, w_out)


def kernel(x, c, ada_w, ada_b, norm_w, ffn_w_up, ffn_w_down, w_in, w_out, mlstm_gate_b, mlstm_norm_w,
           gdn_conv_w, gdn_a_log, gdn_dt_bias, gdn_norm_w, nsa_qk_norm, nsa_cmp_pos, nsa_cmp_w1, nsa_cmp_w2):
    depth = ada_w.shape[0]
    b = x.shape[0]
    mods = _modulation(c, ada_w, ada_b).reshape(depth, b, 9, D_MODEL)
    for l in range(depth):
        mod = mods[l]
        x = _ffn(x, mod, norm_w[l, 0], ffn_w_up[l, 0].astype(BF16), ffn_w_down[l, 0].astype(BF16), 0)
        x = _mixer(x, mod, norm_w[l, 1], w_in[l], w_out[l], mlstm_gate_b[l], mlstm_norm_w[l],
                   gdn_conv_w[l], gdn_a_log[l], gdn_dt_bias[l], gdn_norm_w[l],
                   nsa_qk_norm[l], nsa_cmp_pos[l], nsa_cmp_w1[l], nsa_cmp_w2[l])
        x = _ffn(x, mod, norm_w[l, 2], ffn_w_up[l, 1].astype(BF16), ffn_w_down[l, 1].astype(BF16), 6)
    return x
```

```python
import functools

import jax
import jax.numpy as jnp
from jax import lax
from jax.experimental import pallas as pl
from jax.experimental.pallas import tpu as pltpu

F32 = jnp.float32
BF16 = jnp.bfloat16

D_MODEL = 1024
HEAD_DIM = 64
D_FF = 2816
EPS = 1e-6
NEG = -1e30
BIG = 1e30

M_HEADS = 4
G_HEADS = 4
N_HEADS = 8
N_KV_HEADS = 2
N_GROUP = N_HEADS // N_KV_HEADS
CHUNK = 64
GDN_PAR = 2
MLSTM_PAR = 2
CONV_K = 4
CMP_BLOCK = 32
CMP_STRIDE = 16
CMP_HIDDEN = 256
SLC_BLOCK = 64
N_SELECTED = 16
WINDOW = 512
Q_BLOCK = 128
ROPE_THETA = 10000.0

M_WIDTH = M_HEADS * HEAD_DIM
G_WIDTH = G_HEADS * HEAD_DIM
N_WIDTH = N_HEADS * HEAD_DIM
KV_WIDTH = N_KV_HEADS * HEAD_DIM
M_IN = 4 * M_WIDTH + 2 * M_HEADS
G_IN = 4 * G_WIDTH + 2 * G_HEADS
N_IN = N_WIDTH + 6 * KV_WIDTH + 3 * N_HEADS

LANES = 128
FF_TILE = 256
VMEM_LIMIT = 56 * 1024 * 1024

GT_M = 0
GT_G = 8
GT_N = 16


def _params(*sem):
    return pltpu.CompilerParams(dimension_semantics=sem, vmem_limit_bytes=VMEM_LIMIT)


def _dot(a, b):
    return jnp.dot(a, b, preferred_element_type=F32)


def _dot_nt(a, b):
    return lax.dot_general(a, b, (((1,), (1,)), ((), ())), preferred_element_type=F32)


def _dot_tn(a, b):
    return lax.dot_general(a, b, (((0,), (0,)), ((), ())), preferred_element_type=F32)


def _split_bf16(x, n):
    parts, r = [], x
    for _ in range(n):
        p = r.astype(BF16)
        parts.append(p)
        r = r - p.astype(F32)
    return parts


def _dot01_l(m01, x, n=3):
    acc = None
    for p in _split_bf16(x, n):
        t = _dot(m01, p)
        acc = t if acc is None else acc + t
    return acc


def _dot01_r(x, m01, n=3):
    acc = None
    for p in _split_bf16(x, n):
        t = _dot(p, m01)
        acc = t if acc is None else acc + t
    return acc


def _iota(shape, dim):
    return lax.broadcasted_iota(jnp.int32, shape, dim)


def _ind(cond, dtype=F32):
    return jnp.where(cond, 1.0, 0.0).astype(dtype)


def _softplus(x):
    return jnp.maximum(x, 0.0) + jnp.log1p(jnp.exp(-jnp.abs(x)))


def _group_ones(width):
    r, c = _iota((width, width), 0), _iota((width, width), 1)
    return _ind((r >> 6) == (c >> 6), BF16)


def _rope_lanes(x, cosf, sinf):
    lane = _iota(x.shape, 1)
    partner = jnp.where((lane & 63) < 32, pltpu.roll(x, 96, 1), pltpu.roll(x, 32, 1))
    return x * cosf + partner * sinf


def _mod_kernel(ct_ref, w_ref, b_ref, o_ref):
    ct = ct_ref[...]
    act = ct * jax.nn.sigmoid(ct)
    w = w_ref[0]
    rows = [jnp.sum(w * act[:, b:b + 1], axis=0, keepdims=True) for b in range(ct.shape[1])]
    o_ref[0] = jnp.concatenate(rows, axis=0) + b_ref[0]


def _modulation(c, ada_w, ada_b):
    depth, d, n = ada_w.shape
    b = c.shape[0]
    tn = 1024
    return pl.pallas_call(
        _mod_kernel,
        grid=(depth, n // tn),
        in_specs=[pl.BlockSpec((d, b), lambda l, j: (0, 0)),
                  pl.BlockSpec((1, d, tn), lambda l, j: (l, 0, j)),
                  pl.BlockSpec((1, 1, tn), lambda l, j: (l, 0, j))],
        out_specs=pl.BlockSpec((1, b, tn), lambda l, j: (l, 0, j)),
        out_shape=jax.ShapeDtypeStruct((depth, b, n), F32),
        compiler_params=_params("parallel", "parallel"),
        name="adaln_mod",
    )(c.T, ada_w, ada_b.reshape(depth, 1, n))


def _adaln(x, nw, shift, scale):
    y = x * lax.rsqrt(jnp.mean(x * x, axis=-1, keepdims=True) + EPS) * nw
    return y * (1.0 + scale) + shift


def _ffn_kernel(x_ref, mod_ref, nw_ref, wup_ref, wdn_ref, o_ref, *, row0, res_w):
    x = x_ref[0]
    shift = mod_ref[0, row0:row0 + 1, :]
    scale = mod_ref[0, row0 + 1:row0 + 2, :]
    gate = mod_ref[0, row0 + 2:row0 + 3, :]
    h = _adaln(x, nw_ref[...], shift, scale).astype(BF16)
    acc = jnp.zeros(x.shape, F32)
    for c in range(D_FF // FF_TILE):
        lo = c * FF_TILE
        g = _dot(h, wup_ref[:, lo:lo + FF_TILE])
        u = _dot(h, wup_ref[:, D_FF + lo:D_FF + lo + FF_TILE])
        a = (jax.nn.silu(g) * u).astype(BF16)
        acc = acc + _dot(a, wdn_ref[lo:lo + FF_TILE, :])
    o_ref[0] = x + (res_w * gate) * acc


def _ffn(x, mod, nw, w_up, w_down, row0):
    b, t, d = x.shape
    tm = min(512, t)
    kern = functools.partial(_ffn_kernel, row0=row0, res_w=0.5)
    return pl.pallas_call(
        kern,
        grid=(b, t // tm),
        in_specs=[pl.BlockSpec((1, tm, d), lambda i, j: (i, j, 0)),
                  pl.BlockSpec((1, 9, d), lambda i, j: (i, 0, 0)),
                  pl.BlockSpec((1, d), lambda i, j: (0, 0)),
                  pl.BlockSpec((d, 2 * D_FF), lambda i, j: (0, 0), pipeline_mode=pl.Buffered(1)),
                  pl.BlockSpec((D_FF, d), lambda i, j: (0, 0), pipeline_mode=pl.Buffered(1))],
        out_specs=pl.BlockSpec((1, tm, d), lambda i, j: (i, j, 0)),
        out_shape=jax.ShapeDtypeStruct(x.shape, F32),
        compiler_params=_params("parallel", "parallel"),
        name="ffn",
    )(x, mod, nw.reshape(1, d), w_up, w_down)


def _inproj_kernel(x_ref, mod_ref, nw_ref, w_ref, pm_ref, pg_ref, pq_ref, pkv_ref, gt_ref):
    x = x_ref[0]
    h = _adaln(x, nw_ref[...], mod_ref[0, 3:4, :], mod_ref[0, 4:5, :]).astype(BF16)
    off = 0
    for ref in (pm_ref, pg_ref, pq_ref, pkv_ref, gt_ref):
        wd = ref.shape[-1]
        ref[0] = _dot(h, w_ref[:, off:off + wd])
        off += wd


IN_PERM_WIDTH = 4 * M_WIDTH + 4 * G_WIDTH + N_WIDTH + 6 * KV_WIDTH + LANES


def _permute_w_in(w_in):
    g0, n0 = M_IN, M_IN + G_IN
    gates = jnp.concatenate([w_in[:, 4 * M_WIDTH:M_IN], w_in[:, g0 + 4 * G_WIDTH:g0 + G_IN],
                             w_in[:, n0 + N_WIDTH + 6 * KV_WIDTH:n0 + N_IN]], axis=1)
    gates = jnp.pad(gates, ((0, 0), (0, LANES - gates.shape[1])))
    return jnp.concatenate([w_in[:, :4 * M_WIDTH], w_in[:, g0:g0 + 4 * G_WIDTH],
                            w_in[:, n0:n0 + N_WIDTH + 6 * KV_WIDTH], gates], axis=1).astype(BF16)


def _in_proj(x, mod, nw, w_perm):
    b, t, d = x.shape
    tm = min(512, t)
    widths = (4 * M_WIDTH, 4 * G_WIDTH, N_WIDTH, 6 * KV_WIDTH, LANES)
    return pl.pallas_call(
        _inproj_kernel,
        grid=(b, t // tm),
        in_specs=[pl.BlockSpec((1, tm, d), lambda i, j: (i, j, 0)),
                  pl.BlockSpec((1, 9, d), lambda i, j: (i, 0, 0)),
                  pl.BlockSpec((1, d), lambda i, j: (0, 0)),
                  pl.BlockSpec((d, IN_PERM_WIDTH), lambda i, j: (0, 0), pipeline_mode=pl.Buffered(1))],
        out_specs=[pl.BlockSpec((1, tm, w), lambda i, j: (i, j, 0)) for w in widths],
        out_shape=[jax.ShapeDtypeStruct((b, t, w), F32) for w in widths],
        compiler_params=_params("parallel", "parallel"),
        name="mixer_in_proj",
    )(x, mod, nw.reshape(1, d), w_perm)


def _outproj_kernel(x_ref, mod_ref, ym_ref, yg_ref, yn_ref, w_ref, o_ref):
    y = jnp.concatenate([ym_ref[0], yg_ref[0], yn_ref[0]], axis=-1).astype(BF16)
    o_ref[0] = x_ref[0] + mod_ref[0, 5:6, :] * _dot(y, w_ref[...])


def _out_proj(x, mod, y_m, y_g, y_n, w_out):
    b, t, d = x.shape
    tm = min(512, t)
    row = lambda w: pl.BlockSpec((1, tm, w), lambda i, j: (i, j, 0))
    return pl.pallas_call(
        _outproj_kernel,
        grid=(b, t // tm),
        in_specs=[row(d), pl.BlockSpec((1, 9, d), lambda i, j: (i, 0, 0)),
                  row(M_WIDTH), row(G_WIDTH), row(N_WIDTH),
                  pl.BlockSpec((d, d), lambda i, j: (0, 0), pipeline_mode=pl.Buffered(1))],
        out_specs=row(d),
        out_shape=jax.ShapeDtypeStruct(x.shape, F32),
        compiler_params=_params("parallel", "parallel"),
        name="mixer_out_proj",
    )(x, mod, y_m, y_g, y_n, w_out.astype(BF16))


def _chunk_masks():
    r, c = _iota((CHUNK, CHUNK), 0), _iota((CHUNK, CHUNK), 1)
    return r, c


def _mlstm_kernel(pm_ref, gt_ref, gb_ref, nw_ref, o_ref, c_scr, m_scr, *, n_chunks):
    @pl.when(pl.program_id(1) == 0)
    def _():
        c_scr[...] = jnp.zeros(c_scr.shape, F32)
        m_scr[...] = jnp.zeros(m_scr.shape, F32)

    r, c = _chunk_masks()
    causal = r >= c
    low = _ind(causal, BF16)
    low_ones = jnp.concatenate([low, jnp.ones((CHUNK, CHUNK), BF16)], axis=1)
    strict_up = _ind(r > c)
    eye = _ind(r == c)
    nw = nw_ref[...]
    gb = gb_ref[...]

    ones_v = jnp.ones((CHUNK, HEAD_DIM), BF16)
    heads = range(M_HEADS)

    def chunk_group(cp, carry):
        rows = [pl.ds(pl.multiple_of((cp * MLSTM_PAR + j) * CHUNK, CHUNK), CHUNK) for j in range(MLSTM_PAR)]
        items = [(j, h) for j in range(MLSTM_PAR) for h in heads]
        every = lambda f, *cols: [f(*args) for args in zip(*cols)]
        g_blk = [gt_ref[0, rw, :] + gb for rw in rows]
        lf_blk = [-_softplus(-g) for g in g_blk]
        cum_blk = [_dot01_l(low, x) for x in lf_blk]
        pick = lambda blk, lane: [blk[j][:, lane + h:lane + h + 1] for j, h in items]
        li_c, lf_c, cum_c = pick(g_blk, GT_M), pick(lf_blk, GT_M + M_HEADS), pick(cum_blk, GT_M + M_HEADS)
        head = lambda k: [pm_ref[0, rows[j], k * M_WIDTH + h * HEAD_DIM:k * M_WIDTH + (h + 1) * HEAD_DIM]
                          for j, h in items]
        q = every(lambda x: (x * (HEAD_DIM ** -0.5)).astype(BF16), head(0))
        k = head(1)
        v_aug = every(lambda x: jnp.concatenate([x.astype(BF16), ones_v], axis=1), head(2))
        og = every(jax.nn.sigmoid, head(3))
        dmat = every(lambda f, i: _dot01_l(low_ones, jnp.concatenate([f * strict_up, i * eye], axis=0)), lf_c, li_c)
        dmat = every(lambda x: jnp.where(causal, x, NEG), dmat)
        d_max = every(lambda x: jnp.max(x, axis=-1, keepdims=True), dmat)
        qk = every(lambda a, b: _dot_nt(a, b.astype(BF16)), q, k)
        g_tot = every(lambda x: x[CHUNK - 1:CHUNK, :], cum_c)
        a = every(lambda gt, cm, li: gt - cm + li, g_tot, cum_c, li_c)
        m_loc = every(lambda x: jnp.max(x, axis=0, keepdims=True), a)
        kw = every(lambda kk, x, ml: (kk * jnp.exp(x - ml)).astype(BF16), k, a, m_loc)
        c_loc = every(_dot_tn, kw, v_aug)
        c_prev = [c_scr[h] for h in heads]
        m_prev = [m_scr[h] for h in heads]
        for j in range(MLSTM_PAR):
            sel = [j * M_HEADS + h for h in heads]
            m_inter = [cum_c[i] + m_prev[h] for h, i in enumerate(sel)]
            m_t = [jnp.maximum(m_inter[h], d_max[i]) for h, i in enumerate(sel)]
            s = [(qk[i] * jnp.exp(dmat[i] - m_t[h])).astype(BF16) for h, i in enumerate(sel)]
            inter = [jnp.exp(m_inter[h] - m_t[h]) for h in heads]
            nd = [_dot(s[h], v_aug[i]) + inter[h] * _dot(q[i], c_prev[h].astype(BF16)) for h, i in enumerate(sel)]
            den = [jnp.maximum(jnp.abs(nd[h]), jnp.exp(-m_t[h])) for h in heads]
            hh = [(nd[h] / pltpu.roll(den[h], HEAD_DIM, 1))[:, :HEAD_DIM] for h in heads]
            hh = [x * lax.rsqrt(jnp.mean(x * x, axis=-1, keepdims=True) + EPS) * nw for x in hh]
            m_new = [jnp.maximum(g_tot[i] + m_prev[h], m_loc[i]) for h, i in enumerate(sel)]
            s_old = [jnp.exp(g_tot[i] + m_prev[h] - m_new[h]) for h, i in enumerate(sel)]
            s_new = [jnp.exp(m_loc[i] - m_new[h]) for h, i in enumerate(sel)]
            for h, i in enumerate(sel):
                o_ref[0, rows[j], h * HEAD_DIM:(h + 1) * HEAD_DIM] = og[i] * hh[h]
            c_prev = [s_old[h] * c_prev[h] + s_new[h] * c_loc[i] for h, i in enumerate(sel)]
            m_prev = m_new
        for h in heads:
            c_scr[h] = c_prev[h]
            m_scr[h] = m_prev[h]
        return carry

    lax.fori_loop(0, n_chunks // MLSTM_PAR, chunk_group, 0)


def _mlstm(pm, gt, gate_b, norm_w):
    b, t, _ = pm.shape
    tc = min(512, t)
    gb = jnp.zeros((1, LANES), F32).at[0, GT_M:GT_M + 2 * M_HEADS].set(gate_b.reshape(-1))
    kern = functools.partial(_mlstm_kernel, n_chunks=tc // CHUNK)
    return pl.pallas_call(
        kern,
        grid=(b, t // tc),
        in_specs=[pl.BlockSpec((1, tc, 4 * M_WIDTH), lambda i, j: (i, j, 0)),
                  pl.BlockSpec((1, tc, LANES), lambda i, j: (i, j, 0)),
                  pl.BlockSpec((1, LANES), lambda i, j: (0, 0)),
                  pl.BlockSpec((1, HEAD_DIM), lambda i, j: (0, 0))],
        out_specs=pl.BlockSpec((1, tc, M_WIDTH), lambda i, j: (i, j, 0)),
        out_shape=jax.ShapeDtypeStruct((b, t, M_WIDTH), F32),
        scratch_shapes=[pltpu.VMEM((M_HEADS, HEAD_DIM, 2 * HEAD_DIM), F32),
                        pltpu.VMEM((M_HEADS, 1, 1), F32)],
        compiler_params=_params("parallel", "arbitrary"),
        name="mlstm",
    )(pm, gt, gb, norm_w.reshape(1, HEAD_DIM))


def _gdn_kernel(pg_ref, gt_ref, cw_ref, hp_ref, nw_ref, o_ref, xbuf, qkv_scr, la_scr, s_scr,
                lin_scr, add_scr, g_scr, *, tc):
    first = pl.program_id(1) == 0

    @pl.when(first)
    def _():
        s_scr[...] = jnp.zeros(s_scr.shape, F32)
        xbuf[0:8, :] = jnp.zeros((8, 3 * G_WIDTH), F32)

    @pl.when(jnp.logical_not(first))
    def _():
        xbuf[0:8, :] = xbuf[tc:tc + 8, :]

    xbuf[8:8 + tc, :] = pg_ref[0, :, 0:3 * G_WIDTH]
    acc = None
    for kk in range(CONV_K):
        term = xbuf[8 - (CONV_K - 1) + kk:8 - (CONV_K - 1) + kk + tc, :] * cw_ref[kk:kk + 1, :]
        acc = term if acc is None else acc + term
    act = acc * jax.nn.sigmoid(acc)
    gones = _group_ones(G_WIDTH)
    for part in range(2):
        xx = act[:, part * G_WIDTH:(part + 1) * G_WIDTH]
        ssq = _dot01_r(xx * xx, gones, 2)
        xx = xx * lax.rsqrt(ssq + EPS)
        if part == 0:
            xx = xx * (HEAD_DIM ** -0.5)
        qkv_scr[:, part * G_WIDTH:(part + 1) * G_WIDTH] = xx
    qkv_scr[:, 2 * G_WIDTH:3 * G_WIDTH] = act[:, 2 * G_WIDTH:3 * G_WIDTH]
    gate = gt_ref[0]
    neg_rate = -jnp.exp(hp_ref[0:1, :])
    la_scr[...] = neg_rate * _softplus(gate + hp_ref[1:2, :])

    r, c = _chunk_masks()
    low = _ind(r >= c, BF16)
    strict_low = r > c
    incl_low = r >= c
    strict_up = _ind(r > c)
    eye = _ind(r == c)
    base_mask = _ind(((r >> 2) == (c >> 2)) & (r > c))
    merge_masks = [_ind((((r >> s) & 1) == 1) & ((c >> s) == (r >> s) - 1)) for s in (2, 3, 4, 5)]
    nw = nw_ref[...]

    def prepare(cp, carry):
        items = [(cp * GDN_PAR + j, h) for j in range(GDN_PAR) for h in range(G_HEADS)]
        every = lambda f, *cols: [f(*args) for args in zip(*cols)]
        rows = [pl.ds(pl.multiple_of((cp * GDN_PAR + j) * CHUNK, CHUNK), CHUNK) for j in range(GDN_PAR)]
        la_blk = [la_scr[rw, :] for rw in rows]
        gam_blk = [_dot01_l(low, x) for x in la_blk]
        beta_blk = [jax.nn.sigmoid(gt_ref[0, rw, :]) for rw in rows]
        pick = lambda blk, lane: [blk[i // G_HEADS][:, lane + h:lane + h + 1] for i, (_, h) in enumerate(items)]
        la_c, gam_c, beta_c = pick(la_blk, GT_G), pick(gam_blk, GT_G), pick(beta_blk, GT_G + G_HEADS)
        head = lambda k: [qkv_scr[rows[i // G_HEADS], k * G_WIDTH + h * HEAD_DIM:k * G_WIDTH + (h + 1) * HEAD_DIM]
                          for i, (_, h) in enumerate(items)]
        q, k, v = head(0), head(1), head(2)
        diff = every(lambda x: _dot01_l(low, x * strict_up), la_c)
        dec_strict = every(lambda x: jnp.exp(jnp.where(strict_low, x, NEG)), diff)
        dec_incl = every(lambda x: jnp.exp(jnp.where(incl_low, x, NEG)), diff)
        kb = every(lambda x: x.astype(BF16), k)
        qk_kk = every(lambda a, b: _dot_nt(jnp.concatenate([a.astype(BF16), b], axis=0), b), q, kb)
        amat = every(lambda b, x, dcy: b * x[CHUNK:] * dcy, beta_c, qk_kk, dec_strict)
        n0 = every(lambda a: (-(a * base_mask)).astype(BF16), amat)
        n0sq = every(lambda n: _dot(n, n).astype(BF16), n0)
        inv = every(lambda n: eye + n.astype(F32), n0)
        inv = every(lambda t, n2: t + _dot(t.astype(BF16), n2), inv, n0sq)
        for mask in merge_masks:
            invb = every(lambda t: t.astype(BF16), inv)
            half = every(lambda tb, a: _dot(tb, (a * mask).astype(BF16)).astype(BF16), invb, amat)
            inv = every(lambda t, hf, tb: t - _dot(hf, tb), inv, half, invb)
        e_gam = every(jnp.exp, gam_c)
        rhs = every(lambda b, e, vv, kk: jnp.concatenate([b * vv, (b * e) * kk], axis=1).astype(BF16),
                    beta_c, e_gam, v, k)
        sol = every(lambda t, x: _dot(t.astype(BF16), x).astype(BF16), inv, rhs)
        gam_last = every(lambda g: g[CHUNK - 1:CHUNK, :], gam_c)
        k_d = every(lambda kk, gl, g: (kk * jnp.exp(gl - g)).astype(BF16), k, gam_last, gam_c)
        p_sol = every(lambda x, dcy, sl: _dot((x[:CHUNK] * dcy).astype(BF16), sl), qk_kk, dec_incl, sol)
        kd_sol = every(_dot_tn, k_d, sol)
        for i, (ci, h) in enumerate(items):
            lin_scr[ci, h] = jnp.concatenate([kd_sol[i][:, HEAD_DIM:], q[i] * e_gam[i] - p_sol[i][:, HEAD_DIM:]],
                                             axis=0).astype(BF16)
            add_scr[ci, h] = jnp.concatenate([kd_sol[i][:, :HEAD_DIM], p_sol[i][:, :HEAD_DIM]], axis=0)
            g_scr[ci, h] = jnp.broadcast_to(jnp.exp(gam_last[i]), (1, HEAD_DIM))
        return carry

    lax.fori_loop(0, tc // (CHUNK * GDN_PAR), prepare, 0)

    def advance(ci, carry):
        r0 = pl.multiple_of(ci * CHUNK, CHUNK)
        heads = range(G_HEADS)
        s_prev = [s_scr[h] for h in heads]
        prod = [_dot(lin_scr[ci, h], s_prev[h].astype(BF16)) for h in heads]
        add = [add_scr[ci, h] for h in heads]
        s_next = [g_scr[ci, h] * s_prev[h] - prod[h][:CHUNK] + add[h][:CHUNK] for h in heads]
        o = [prod[h][CHUNK:] + add[h][CHUNK:] for h in heads]
        o = [x * lax.rsqrt(jnp.mean(x * x, axis=-1, keepdims=True) + EPS) * nw for x in o]
        z = [pg_ref[0, pl.ds(r0, CHUNK), 3 * G_WIDTH + h * HEAD_DIM:3 * G_WIDTH + (h + 1) * HEAD_DIM] for h in heads]
        for h in heads:
            s_scr[h] = s_next[h]
            o_ref[0, pl.ds(r0, CHUNK), h * HEAD_DIM:(h + 1) * HEAD_DIM] = o[h] * (z[h] * jax.nn.sigmoid(z[h]))
        return carry

    lax.fori_loop(0, tc // CHUNK, advance, 0)


def _gdn(pg, gt, conv_w, a_log, dt_bias, norm_w):
    b, t, _ = pg.shape
    tc = min(512, t)
    hp = jnp.zeros((2, LANES), F32)
    hp = hp.at[0, GT_G:GT_G + G_HEADS].set(a_log).at[1, GT_G:GT_G + G_HEADS].set(dt_bias)
    kern = functools.partial(_gdn_kernel, tc=tc)
    return pl.pallas_call(
        kern,
        grid=(b, t // tc),
        in_specs=[pl.BlockSpec((1, tc, 4 * G_WIDTH), lambda i, j: (i, j, 0)),
                  pl.BlockSpec((1, tc, LANES), lambda i, j: (i, j, 0)),
                  pl.BlockSpec((CONV_K, 3 * G_WIDTH), lambda i, j: (0, 0)),
                  pl.BlockSpec((2, LANES), lambda i, j: (0, 0)),
                  pl.BlockSpec((1, HEAD_DIM), lambda i, j: (0, 0))],
        out_specs=pl.BlockSpec((1, tc, G_WIDTH), lambda i, j: (i, j, 0)),
        out_shape=jax.ShapeDtypeStruct((b, t, G_WIDTH), F32),
        scratch_shapes=[pltpu.VMEM((tc + 8, 3 * G_WIDTH), F32),
                        pltpu.VMEM((tc, 3 * G_WIDTH), F32),
                        pltpu.VMEM((tc, LANES), F32),
                        pltpu.VMEM((G_HEADS, HEAD_DIM, HEAD_DIM), F32),
                        pltpu.VMEM((tc // CHUNK, G_HEADS, 2 * CHUNK, HEAD_DIM), BF16),
                        pltpu.VMEM((tc // CHUNK, G_HEADS, 2 * CHUNK, HEAD_DIM), F32),
                        pltpu.VMEM((tc // CHUNK, G_HEADS, 1, HEAD_DIM), F32)],
        compiler_params=_params("parallel", "arbitrary"),
        name="gdn",
    )(pg, gt, conv_w.astype(F32), hp, norm_w.reshape(1, HEAD_DIM))


def _rope_tables(pos):
    half = HEAD_DIM // 2
    inv_freq = jnp.power(ROPE_THETA, -jnp.arange(half, dtype=F32) / half)
    ang = pos.astype(F32)[:, None] * inv_freq[None, :]
    cos, sin = jnp.cos(ang), jnp.sin(ang)
    cosf = jnp.tile(cos, (1, LANES // half))
    sinf = jnp.tile(jnp.concatenate([-sin, sin], axis=1), (1, LANES // HEAD_DIM))
    return cosf, sinf


KS_AUG = LANES + HEAD_DIM


def _nsa_prep_kernel(pq_ref, pkv_ref, cos_ref, sin_ref, nw_ref, q_ref, ks_ref, vs_ref, kw_ref, vw_ref):
    cosf, sinf = cos_ref[...], sin_ref[...]
    gones = _group_ones(LANES)

    def norm_rope(x, w):
        ssq = _dot01_r(x * x, gones, 2)
        y = x * lax.rsqrt(ssq * (1.0 / HEAD_DIM) + EPS) * w
        return _rope_lanes(y, cosf, sinf)

    def put_heads(ref, slab, h0):
        ref[0, h0] = slab[:, :HEAD_DIM].astype(BF16)
        ref[0, h0 + 1] = slab[:, HEAD_DIM:].astype(BF16)

    for s in range(N_WIDTH // LANES):
        qs = norm_rope(pq_ref[0, :, s * LANES:(s + 1) * LANES], nw_ref[0:1, :]) * (HEAD_DIM ** -0.5)
        put_heads(q_ref, qs, 2 * s)
    kv = lambda i: pkv_ref[0, :, i * KV_WIDTH:(i + 1) * KV_WIDTH]
    tm = cosf.shape[0]
    tok = pl.program_id(1) * tm + _iota((tm, LANES), 0)
    onehot = _ind((tok >> 6) == _iota((tm, LANES), 1), BF16)
    k_slc = norm_rope(kv(2), nw_ref[2:3, :])
    for h in range(N_KV_HEADS):
        ks_ref[0, h, :, 0:LANES] = onehot
        ks_ref[0, h, :, LANES:LANES + HEAD_DIM] = k_slc[:, h * HEAD_DIM:(h + 1) * HEAD_DIM].astype(BF16)
    put_heads(kw_ref, norm_rope(kv(4), nw_ref[3:4, :]), 0)
    lane = _iota((tm, LANES), 1)
    for ref, slab in ((vs_ref, kv(3)), (vw_ref, kv(5))):
        ref[0, 0] = jnp.where(lane < HEAD_DIM, slab, 1.0).astype(BF16)
        ref[0, 1] = jnp.where(lane < HEAD_DIM, pltpu.roll(slab, HEAD_DIM, 1), 1.0).astype(BF16)


def _nsa_prep(pq, pkv, qk_norm):
    b, t, _ = pq.shape
    assert t // SLC_BLOCK <= LANES
    tm = min(512, t)
    cosf, sinf = _rope_tables(jnp.arange(t, dtype=jnp.int32))
    nw = jnp.tile(qk_norm, (1, LANES // HEAD_DIM))
    heads = lambda n: pl.BlockSpec((1, n, tm, HEAD_DIM), lambda i, j: (i, 0, j, 0))
    wide = lambda w: pl.BlockSpec((1, N_KV_HEADS, tm, w), lambda i, j: (i, 0, j, 0))
    shp = lambda n, w: jax.ShapeDtypeStruct((b, n, t, w), BF16)
    return pl.pallas_call(
        _nsa_prep_kernel,
        grid=(b, t // tm),
        in_specs=[pl.BlockSpec((1, tm, N_WIDTH), lambda i, j: (i, j, 0)),
                  pl.BlockSpec((1, tm, 6 * KV_WIDTH), lambda i, j: (i, j, 0)),
                  pl.BlockSpec((tm, LANES), lambda i, j: (j, 0)),
                  pl.BlockSpec((tm, LANES), lambda i, j: (j, 0)),
                  pl.BlockSpec((4, LANES), lambda i, j: (0, 0))],
        out_specs=[heads(N_HEADS), wide(KS_AUG), wide(LANES), heads(N_KV_HEADS), wide(LANES)],
        out_shape=[shp(N_HEADS, HEAD_DIM), shp(N_KV_HEADS, KS_AUG), shp(N_KV_HEADS, LANES),
                   shp(N_KV_HEADS, HEAD_DIM), shp(N_KV_HEADS, LANES)],
        compiler_params=_params("parallel", "parallel"),
        name="nsa_prep",
    )(pq, pkv, cosf, sinf, nw)


def _compress_kernel(x_ref, pos_ref, w1_ref, w2_ref, cos_ref, sin_ref, nw_ref, o_ref):
    x = x_ref[0, 0, 0]
    n16 = x.shape[0]
    half = CMP_STRIDE * HEAD_DIM
    a = _dot((x + pos_ref[0, 0:1, :]).astype(BF16), w1_ref[0, 0:half, :])
    bmat = _dot((x + pos_ref[0, 1:2, :]).astype(BF16), w1_ref[0, half:2 * half, :])
    nxt = jnp.concatenate([bmat[1:, :], jnp.zeros((1, CMP_HIDDEN), F32)], axis=0)
    hid = a + nxt
    hid = hid * jax.nn.sigmoid(hid)
    out = _dot(hid.astype(BF16), w2_ref[0])
    ssq = jnp.sum(out * out, axis=-1, keepdims=True)
    normed = out * lax.rsqrt(ssq * (1.0 / HEAD_DIM) + EPS) * nw_ref[...]
    roped = _rope_lanes(normed, cos_ref[...], sin_ref[...])
    is_key = pl.program_id(1) == 0
    o_ref[0, 0, 0] = jnp.where(is_key, roped, out)
    del n16


def _nsa_compress(pkv, cmp_pos, cmp_w1, cmp_w2, kc_norm):
    b, t, _ = pkv.shape
    n16 = t // CMP_STRIDE
    flat = CMP_STRIDE * HEAD_DIM
    x = pkv[..., :2 * KV_WIDTH].reshape(b, n16, CMP_STRIDE, 2, N_KV_HEADS, HEAD_DIM)
    x = x.transpose(0, 3, 4, 1, 2, 5).reshape(b, 2, N_KV_HEADS, n16, flat)
    pos = cmp_pos.reshape(2, 2, flat)
    w1 = cmp_w1.astype(BF16)
    w2 = jnp.pad(cmp_w2, ((0, 0), (0, 0), (0, LANES - HEAD_DIM))).astype(BF16)
    cosf, sinf = _rope_tables(jnp.arange(n16, dtype=jnp.int32) * CMP_STRIDE + CMP_BLOCK - 1)
    nw = jnp.pad(kc_norm.reshape(1, HEAD_DIM), ((0, 0), (0, LANES - HEAD_DIM)))
    return pl.pallas_call(
        _compress_kernel,
        grid=(b, 2, N_KV_HEADS),
        in_specs=[pl.BlockSpec((1, 1, 1, n16, flat), lambda i, s, h: (i, s, h, 0, 0)),
                  pl.BlockSpec((1, 2, flat), lambda i, s, h: (s, 0, 0)),
                  pl.BlockSpec((1, 2 * flat, CMP_HIDDEN), lambda i, s, h: (s, 0, 0)),
                  pl.BlockSpec((1, CMP_HIDDEN, LANES), lambda i, s, h: (s, 0, 0)),
                  pl.BlockSpec((n16, LANES), lambda i, s, h: (0, 0)),
                  pl.BlockSpec((n16, LANES), lambda i, s, h: (0, 0)),
                  pl.BlockSpec((1, LANES), lambda i, s, h: (0, 0))],
        out_specs=pl.BlockSpec((1, 1, 1, n16, LANES), lambda i, s, h: (i, s, h, 0, 0)),
        out_shape=jax.ShapeDtypeStruct((b, 2, N_KV_HEADS, n16, LANES), F32),
        compiler_params=_params("parallel", "parallel", "parallel"),
        name="nsa_compress",
    )(x, pos, w1, w2, cosf, sinf, nw)


SLC_TILE = 512
SLC_GROUPS = 2
WIN_SPAN = Q_BLOCK + WINDOW


def _nsa_attn_kernel(q_ref, cmp_ref, ks_ref, vs_ref, kw_ref, vw_ref, gt_ref, o_ref, *, n_slc):
    hkv = pl.program_id(1)
    qi = pl.program_id(2)
    start = qi * Q_BLOCK
    rows = N_GROUP * Q_BLOCK
    q = q_ref[0].reshape(rows, HEAD_DIM)
    n16 = cmp_ref.shape[3]

    def tq_of(shape):
        return start + (_iota(shape, 0) & (Q_BLOCK - 1))

    kc = cmp_ref[0, 0, 0][:, :HEAD_DIM].astype(BF16)
    vc = cmp_ref[0, 1, 0][:, :HEAD_DIM].astype(BF16)
    s_c = _dot_nt(q, kc)
    tq_c = tq_of(s_c.shape)
    valid_c = (_iota(s_c.shape, 1) * CMP_STRIDE + (CMP_BLOCK - 1)) <= tq_c
    s_c = jnp.where(valid_c, s_c, NEG)
    e_c = jnp.exp(s_c - jnp.max(s_c, axis=-1, keepdims=True))
    p_c = e_c / jnp.sum(e_c, axis=-1, keepdims=True)
    p_c = jnp.where(tq_c >= CMP_BLOCK - 1, p_c, 0.0)
    o_c = _dot(p_c.astype(BF16), vc)

    p_sum = p_c[0:Q_BLOCK]
    for g in range(1, N_GROUP):
        p_sum = p_sum + p_c[g * Q_BLOCK:(g + 1) * Q_BLOCK]
    sb, cn = _iota((n_slc, n16), 0), _iota((n_slc, n16), 1)
    c_lo, s_lo = cn * CMP_STRIDE, sb * SLC_BLOCK
    ov = jnp.maximum(jnp.minimum(c_lo + CMP_BLOCK, s_lo + SLC_BLOCK) - jnp.maximum(c_lo, s_lo), 0)
    ov = jnp.where(cn < n16 - 1, ov, 0)
    overlap_t = (ov.astype(F32) * (1.0 / CMP_BLOCK)).astype(BF16)
    imp = None
    for part in _split_bf16(p_sum, 2):
        term = _dot_nt(overlap_t, part)
        imp = term if imp is None else imp + term
    tq_i = start + _iota(imp.shape, 1)
    blk = _iota(imp.shape, 0)
    cur = tq_i >> 6
    forced = (blk == 0) | (blk == cur) | (blk == cur - 1)
    imp = jnp.where(forced, BIG, imp)
    imp = jnp.where(blk * SLC_BLOCK <= tq_i, imp, NEG)
    blk_f = blk.astype(F32)
    sel_t = jnp.zeros(imp.shape, F32)
    for _ in range(min(N_SELECTED, n_slc)):
        top = jnp.max(imp, axis=0, keepdims=True)
        first = jnp.min(jnp.where(imp == top, blk_f, float(n_slc)), axis=0, keepdims=True)
        hit = blk_f == first
        sel_t = jnp.where(hit, 1.0, sel_t)
        imp = jnp.where(hit, -3e38, imp)

    own = pl.multiple_of(start, Q_BLOCK)
    kd = ks_ref[0, 0, pl.ds(own, Q_BLOCK), LANES:LANES + HEAD_DIM]
    vd = vs_ref[0, 0, pl.ds(own, Q_BLOCK), :]
    s_d = _dot_nt(q, kd)
    s_d = jnp.where(_iota(s_d.shape, 1) <= (_iota(s_d.shape, 0) & (Q_BLOCK - 1)), s_d, NEG)
    m_d = jnp.max(s_d, axis=-1, keepdims=True)
    init = (m_d, _dot(jnp.exp(s_d - m_d).astype(BF16), vd))

    bias_t = (jnp.where(blk < 2 * qi, sel_t, 0.0) - 1.0) * BIG
    if n_slc < LANES:
        bias_t = jnp.concatenate([bias_t, jnp.zeros((LANES - n_slc, Q_BLOCK), F32)], axis=0)
    eye = _ind(_iota((LANES, LANES), 0) == _iota((LANES, LANES), 1), BF16)
    bias = _dot_tn(bias_t.astype(BF16), eye).astype(BF16)
    q_aug = jnp.concatenate([jnp.concatenate([bias] * N_GROUP, axis=0), q], axis=1)

    grp = rows // SLC_GROUPS
    q_grp = [q_aug[i * grp:(i + 1) * grp] for i in range(SLC_GROUPS)]

    def slc_step(kt, carry):
        m, acc = carry
        k0 = pl.multiple_of(kt * SLC_TILE, SLC_TILE)
        ks = ks_ref[0, 0, pl.ds(k0, SLC_TILE), :]
        vs = vs_ref[0, 0, pl.ds(k0, SLC_TILE), :]
        s = [_dot_nt(qg, ks) for qg in q_grp]
        m_new = [jnp.maximum(m[i], jnp.max(s[i], axis=-1, keepdims=True)) for i in range(SLC_GROUPS)]
        p = [jnp.exp(s[i] - m_new[i]).astype(BF16) for i in range(SLC_GROUPS)]
        acc = [jnp.exp(m[i] - m_new[i]) * acc[i] + _dot(p[i], vs) for i in range(SLC_GROUPS)]
        return m_new, acc

    n_tiles = (start + SLC_TILE - 1) // SLC_TILE
    m0, acc0 = init
    split = lambda x: [x[i * grp:(i + 1) * grp] for i in range(SLC_GROUPS)]
    _, acc_s = lax.fori_loop(0, n_tiles, slc_step, (split(m0), split(acc0)))
    acc_s = jnp.concatenate(acc_s, axis=0)
    o_s = (acc_s / pltpu.roll(acc_s, HEAD_DIM, 1))[:, :HEAD_DIM]

    w0 = pl.multiple_of(jnp.maximum(start - WINDOW, 0), Q_BLOCK)
    kw = kw_ref[0, 0, pl.ds(w0, WIN_SPAN), :]
    vw = vw_ref[0, 0, pl.ds(w0, WIN_SPAN), :]
    s_w = _dot_nt(q, kw)
    dist = tq_of(s_w.shape) - (w0 + _iota(s_w.shape, 1))
    s_w = jnp.where((dist >= 0) & (dist < WINDOW), s_w, NEG)
    e_w = jnp.exp(s_w - jnp.max(s_w, axis=-1, keepdims=True))
    acc_w = _dot(e_w.astype(BF16), vw)
    o_w = (acc_w / pltpu.roll(acc_w, HEAD_DIM, 1))[:, :HEAD_DIM]

    gates = jax.nn.sigmoid(gt_ref[0])
    outs = []
    for g in range(N_GROUP):
        def gate_col(branch):
            c0 = GT_N + branch * N_HEADS + g
            c1 = c0 + N_GROUP
            return jnp.where(hkv == 0, gates[:, c0:c0 + 1], gates[:, c1:c1 + 1])
        sl = slice(g * Q_BLOCK, (g + 1) * Q_BLOCK)
        outs.append(gate_col(0) * o_c[sl] + gate_col(1) * o_s[sl] + gate_col(2) * o_w[sl])
    o_ref[0] = jnp.concatenate(outs, axis=-1)


def _nsa_attention(qh, cmp, ks, vs, kw, vw, gt):
    b, _, t, _ = qh.shape
    n16 = cmp.shape[3]
    n_slc = t // SLC_BLOCK
    full = lambda w: pl.BlockSpec((1, 1, t, w), lambda i, h, j: (i, h, 0, 0))
    kern = functools.partial(_nsa_attn_kernel, n_slc=n_slc)
    return pl.pallas_call(
        kern,
        grid=(b, N_KV_HEADS, t // Q_BLOCK),
        in_specs=[pl.BlockSpec((1, N_GROUP, Q_BLOCK, HEAD_DIM), lambda i, h, j: (i, h, j, 0)),
                  pl.BlockSpec((1, 2, 1, n16, LANES), lambda i, h, j: (i, 0, h, 0, 0)),
                  full(KS_AUG), full(LANES), full(HEAD_DIM), full(LANES),
                  pl.BlockSpec((1, Q_BLOCK, LANES), lambda i, h, j: (i, j, 0))],
        out_specs=pl.BlockSpec((1, Q_BLOCK, N_GROUP * HEAD_DIM), lambda i, h, j: (i, j, h)),
        out_shape=jax.ShapeDtypeStruct((b, t, N_WIDTH), F32),
        compiler_params=_params("parallel", "parallel", "arbitrary"),
        name="nsa_attention",
    )(qh, cmp, ks, vs, kw, vw, gt)


V_ROWS = HEAD_DIM + 16
LOG2E = 1.4426950408889634


def _nsa_prep_t_kernel(pq_ref, pkv_ref, cos_ref, sin_ref, nw_ref, q_ref, ks_ref, vs_ref, kw_ref, vw_ref):
    cosf, sinf = cos_ref[...], sin_ref[...]
    gones = _group_ones(LANES)
    tm = cosf.shape[0]

    def norm_rope(x, w):
        ssq = _dot01_r(x * x, gones, 2)
        y = x * lax.rsqrt(ssq * (1.0 / HEAD_DIM) + EPS) * w
        return _rope_lanes(y, cosf, sinf)

    for s in range(N_WIDTH // LANES):
        qs = norm_rope(pq_ref[0, :, s * LANES:(s + 1) * LANES], nw_ref[0:1, :]) * (HEAD_DIM ** -0.5 * LOG2E)
        qs_t = qs.T.astype(BF16)
        for half in range(2):
            hkv, g = divmod(2 * s + half, N_GROUP)
            for blk in range(tm // Q_BLOCK):
                col = (blk * N_GROUP + g) * Q_BLOCK
                q_ref[0, hkv, :, col:col + Q_BLOCK] = qs_t[half * HEAD_DIM:(half + 1) * HEAD_DIM,
                                                           blk * Q_BLOCK:(blk + 1) * Q_BLOCK]
    kv = lambda i: pkv_ref[0, :, i * KV_WIDTH:(i + 1) * KV_WIDTH]
    tok = pl.program_id(1) * tm + _iota((tm, LANES), 0)
    onehot = _ind((tok >> 6) == _iota((tm, LANES), 1), BF16)
    k_slc = norm_rope(kv(2), nw_ref[2:3, :])
    k_win = norm_rope(kv(4), nw_ref[3:4, :])
    ones = jnp.ones((V_ROWS - HEAD_DIM, tm), BF16)
    for h in range(N_KV_HEADS):
        ks_ref[0, h, :, 0:LANES] = onehot
        ks_ref[0, h, :, LANES:LANES + HEAD_DIM] = k_slc[:, h * HEAD_DIM:(h + 1) * HEAD_DIM].astype(BF16)
        kw_ref[0, h] = k_win[:, h * HEAD_DIM:(h + 1) * HEAD_DIM].astype(BF16)
    for ref, slab in ((vs_ref, kv(3)), (vw_ref, kv(5))):
        v_t = slab.T.astype(BF16)
        for h in range(N_KV_HEADS):
            ref[0, h, 0:HEAD_DIM, :] = v_t[h * HEAD_DIM:(h + 1) * HEAD_DIM]
            ref[0, h, HEAD_DIM:V_ROWS, :] = ones


def _nsa_prep_t(pq, pkv, qk_norm):
    b, t, _ = pq.shape
    assert t // SLC_BLOCK <= LANES
    tm = min(512, t)
    cosf, sinf = _rope_tables(jnp.arange(t, dtype=jnp.int32))
    nw = jnp.tile(qk_norm, (1, LANES // HEAD_DIM))
    rows = lambda w: pl.BlockSpec((1, N_KV_HEADS, tm, w), lambda i, j: (i, 0, j, 0))
    cols = lambda r, w: pl.BlockSpec((1, N_KV_HEADS, r, w), lambda i, j: (i, 0, 0, j))
    shp = lambda r, w: jax.ShapeDtypeStruct((b, N_KV_HEADS, r, w), BF16)
    return pl.pallas_call(
        _nsa_prep_t_kernel,
        grid=(b, t // tm),
        in_specs=[pl.BlockSpec((1, tm, N_WIDTH), lambda i, j: (i, j, 0)),
                  pl.BlockSpec((1, tm, 6 * KV_WIDTH), lambda i, j: (i, j, 0)),
                  pl.BlockSpec((tm, LANES), lambda i, j: (j, 0)),
                  pl.BlockSpec((tm, LANES), lambda i, j: (j, 0)),
                  pl.BlockSpec((4, LANES), lambda i, j: (0, 0))],
        out_specs=[cols(HEAD_DIM, N_GROUP * tm), rows(KS_AUG), cols(V_ROWS, tm), rows(HEAD_DIM), cols(V_ROWS, tm)],
        out_shape=[shp(HEAD_DIM, N_GROUP * t), shp(t, KS_AUG), shp(V_ROWS, t), shp(t, HEAD_DIM), shp(V_ROWS, t)],
        compiler_params=_params("parallel", "parallel"),
        name="nsa_prep",
    )(pq, pkv, cosf, sinf, nw)


def _compress_t_kernel(x_ref, pos_ref, w1_ref, w2_ref, cos_ref, sin_ref, nw_ref, kc_ref, vc_ref):
    x = x_ref[0, 0, 0]
    half = CMP_STRIDE * HEAD_DIM
    a = _dot((x + pos_ref[0, 0:1, :]).astype(BF16), w1_ref[0, 0:half, :])
    bmat = _dot((x + pos_ref[0, 1:2, :]).astype(BF16), w1_ref[0, half:2 * half, :])
    hid = a + jnp.concatenate([bmat[1:, :], jnp.zeros((1, CMP_HIDDEN), F32)], axis=0)
    hid = hid * jax.nn.sigmoid(hid)
    out = _dot(hid.astype(BF16), w2_ref[0])

    @pl.when(pl.program_id(2) == 0)
    def _():
        ssq = jnp.sum(out * out, axis=-1, keepdims=True)
        normed = out * lax.rsqrt(ssq * (1.0 / HEAD_DIM) + EPS) * nw_ref[...]
        kc_ref[0, 0] = _rope_lanes(normed, cos_ref[...], sin_ref[...]).astype(BF16)

    @pl.when(pl.program_id(2) == 1)
    def _():
        vc_ref[0, 0] = out.T[0:HEAD_DIM].astype(BF16)


def _nsa_compress_t(pkv, cmp_pos, cmp_w1, cmp_w2, kc_norm):
    b, t, _ = pkv.shape
    n16 = t // CMP_STRIDE
    flat = CMP_STRIDE * HEAD_DIM
    x = pkv[..., :2 * KV_WIDTH].reshape(b, n16, CMP_STRIDE, 2, N_KV_HEADS, HEAD_DIM)
    x = x.transpose(0, 3, 4, 1, 2, 5).reshape(b, 2, N_KV_HEADS, n16, flat)
    pos = cmp_pos.reshape(2, 2, flat)
    w1 = cmp_w1.astype(BF16)
    w2 = jnp.pad(cmp_w2, ((0, 0), (0, 0), (0, LANES - HEAD_DIM))).astype(BF16)
    cosf, sinf = _rope_tables(jnp.arange(n16, dtype=jnp.int32) * CMP_STRIDE + CMP_BLOCK - 1)
    nw = jnp.pad(kc_norm.reshape(1, HEAD_DIM), ((0, 0), (0, LANES - HEAD_DIM)))
    return pl.pallas_call(
        _compress_t_kernel,
        grid=(b, N_KV_HEADS, 2),
        in_specs=[pl.BlockSpec((1, 1, 1, n16, flat), lambda i, h, s: (i, s, h, 0, 0)),
                  pl.BlockSpec((1, 2, flat), lambda i, h, s: (s, 0, 0)),
                  pl.BlockSpec((1, 2 * flat, CMP_HIDDEN), lambda i, h, s: (s, 0, 0)),
                  pl.BlockSpec((1, CMP_HIDDEN, LANES), lambda i, h, s: (s, 0, 0)),
                  pl.BlockSpec((n16, LANES), lambda i, h, s: (0, 0)),
                  pl.BlockSpec((n16, LANES), lambda i, h, s: (0, 0)),
                  pl.BlockSpec((1, LANES), lambda i, h, s: (0, 0))],
        out_specs=[pl.BlockSpec((1, 1, n16, LANES), lambda i, h, s: (i, h, 0, 0)),
                   pl.BlockSpec((1, 1, HEAD_DIM, n16), lambda i, h, s: (i, h, 0, 0))],
        out_shape=[jax.ShapeDtypeStruct((b, N_KV_HEADS, n16, LANES), BF16),
                   jax.ShapeDtypeStruct((b, N_KV_HEADS, HEAD_DIM, n16), BF16)],
        compiler_params=_params("parallel", "parallel", "arbitrary"),
        name="nsa_compress",
    )(x, pos, w1, w2, cosf, sinf, nw)


def _col_max(x):
    return jnp.max(x, axis=0, keepdims=True)


def _nsa_attn_t_kernel(q_ref, kc_ref, vc_ref, ks_ref, vs_ref, kw_ref, vw_ref, gt_ref, o_ref,
                       sa_scr, sb_scr, pa_scr, pb_scr, *, n_slc):
    hkv = pl.program_id(1)
    qi = pl.program_id(2)
    start = qi * Q_BLOCK
    cols = N_GROUP * Q_BLOCK
    q_t = q_ref[0, 0]
    n16 = kc_ref.shape[2]

    def per_head(x):
        return jnp.concatenate([x] * N_GROUP, axis=1)

    def finish(acc):
        return acc[0:HEAD_DIM] * (1.0 / acc[HEAD_DIM:HEAD_DIM + 1])

    s_c = _dot(kc_ref[0, 0][:, :HEAD_DIM], q_t)
    tq_q = start + _iota((n16, Q_BLOCK), 1)
    valid_c = (_iota((n16, Q_BLOCK), 0) * CMP_STRIDE + (CMP_BLOCK - 1)) <= tq_q
    s_c = s_c + per_head(jnp.where(valid_c, 0.0, NEG))
    e_c = jnp.exp2(s_c - _col_max(s_c))
    any_c = per_head(_ind(start + _iota((1, Q_BLOCK), 1) >= CMP_BLOCK - 1))
    p_c = e_c * (any_c / jnp.sum(e_c, axis=0, keepdims=True))
    o_c = _dot(vc_ref[0, 0], p_c.astype(BF16))

    p_sum = p_c[:, 0:Q_BLOCK]
    for g in range(1, N_GROUP):
        p_sum = p_sum + p_c[:, g * Q_BLOCK:(g + 1) * Q_BLOCK]
    sb, cn = _iota((n_slc, n16), 0), _iota((n_slc, n16), 1)
    c_lo, s_lo = cn * CMP_STRIDE, sb * SLC_BLOCK
    ov = jnp.maximum(jnp.minimum(c_lo + CMP_BLOCK, s_lo + SLC_BLOCK) - jnp.maximum(c_lo, s_lo), 0)
    ov = jnp.where(cn < n16 - 1, ov, 0)
    overlap_t = (ov.astype(F32) * (1.0 / CMP_BLOCK)).astype(BF16)
    imp = _dot01_l(overlap_t, p_sum, 2)
    tq_i = start + _iota(imp.shape, 1)
    blk = _iota(imp.shape, 0)
    cur = tq_i >> 6
    forced = (blk == 0) | (blk == cur) | (blk == cur - 1)
    imp = jnp.where(forced, BIG, imp)
    imp = jnp.where(blk * SLC_BLOCK <= tq_i, imp, NEG)
    blk_f = blk.astype(F32)
    sel_t = jnp.zeros(imp.shape, F32)
    for _ in range(min(N_SELECTED, n_slc)):
        top = _col_max(imp)
        first = jnp.min(jnp.where(imp == top, blk_f, float(n_slc)), axis=0, keepdims=True)
        hit = blk_f == first
        sel_t = jnp.where(hit, 1.0, sel_t)
        imp = jnp.where(hit, -3e38, imp)

    w0 = pl.multiple_of(jnp.maximum(start - WINDOW, 0), Q_BLOCK)
    s_w = _dot(kw_ref[0, 0, pl.ds(w0, WIN_SPAN), :], q_t)
    dist = start + _iota((WIN_SPAN, Q_BLOCK), 1) - (w0 + _iota((WIN_SPAN, Q_BLOCK), 0))
    s_w = s_w + per_head(jnp.where((dist >= 0) & (dist < WINDOW), 0.0, NEG))
    p_w = jnp.exp2(s_w - _col_max(s_w)).astype(BF16)
    o_w = finish(_dot(vw_ref[0, 0, :, pl.ds(w0, WIN_SPAN)], p_w))

    own = pl.multiple_of(start, Q_BLOCK)
    s_d = _dot(ks_ref[0, 0, pl.ds(own, Q_BLOCK), LANES:LANES + HEAD_DIM], q_t)
    s_d = s_d + per_head(jnp.where(_iota((Q_BLOCK, Q_BLOCK), 0) <= _iota((Q_BLOCK, Q_BLOCK), 1), 0.0, NEG))
    m_d = _col_max(s_d)
    acc_d = _dot(vs_ref[0, 0, :, pl.ds(own, Q_BLOCK)], jnp.exp2(s_d - m_d).astype(BF16))

    bias_t = (jnp.where(blk < 2 * qi, sel_t, 0.0) - 1.0) * BIG
    if n_slc < LANES:
        bias_t = jnp.concatenate([bias_t, jnp.zeros((LANES - n_slc, Q_BLOCK), F32)], axis=0)
    bias_t = bias_t.astype(BF16)
    q_aug = jnp.concatenate([jnp.concatenate([bias_t] * N_GROUP, axis=1), q_t], axis=0)

    n_tiles = (start + SLC_TILE - 1) // SLC_TILE

    def scores(kt):
        k0 = pl.multiple_of(kt * SLC_TILE, SLC_TILE)
        return _dot(ks_ref[0, 0, pl.ds(k0, SLC_TILE), :], q_aug)

    def weighted(p, kt):
        k0 = pl.multiple_of(kt * SLC_TILE, SLC_TILE)
        return _dot(vs_ref[0, 0, :, pl.ds(k0, SLC_TILE)], p)

    top_tile = ks_ref.shape[2] // SLC_TILE - 1

    def half_step(kt, s_in, s_out, p_in, p_out, a_prev, m, acc):
        s_out[...] = scores(jnp.minimum(kt + 1, top_tile))
        acc = a_prev * acc + weighted(p_in[...], jnp.maximum(kt - 1, 0))
        s = s_in[...]
        m_new = jnp.maximum(m, _col_max(s))
        p_out[...] = jnp.exp2(s - m_new).astype(BF16)
        return jnp.exp2(m - m_new), m_new, acc

    def slc_pair(j, carry):
        carry = half_step(2 * j, sa_scr, sb_scr, pb_scr, pa_scr, *carry)
        return half_step(2 * j + 1, sb_scr, sa_scr, pa_scr, pb_scr, *carry)

    sa_scr[...] = scores(0)
    pb_scr[...] = jnp.zeros((SLC_TILE, cols), BF16)
    n_pairs = (n_tiles + 1) // 2
    a_last, _, acc_s = lax.fori_loop(0, n_pairs, slc_pair, (jnp.ones((1, cols), F32), m_d, acc_d))
    o_s = finish(a_last * acc_s + weighted(pb_scr[...], jnp.maximum(2 * n_pairs - 1, 0)))

    gates_t = jax.nn.sigmoid(gt_ref[0]).T

    def gate_row(branch):
        rows_ = []
        for g in range(N_GROUP):
            c0 = GT_N + branch * N_HEADS + g
            c1 = c0 + N_GROUP
            rows_.append(jnp.where(hkv == 0, gates_t[c0:c0 + 1], gates_t[c1:c1 + 1]))
        return jnp.concatenate(rows_, axis=1)

    y_t = gate_row(0) * o_c + gate_row(1) * o_s + gate_row(2) * o_w
    y = jnp.concatenate([y_t[:, g * Q_BLOCK:(g + 1) * Q_BLOCK] for g in range(N_GROUP)], axis=0)
    o_ref[0] = y.T


def _nsa_attention_t(q_t, kc, vc_t, ks, vs_t, kw, vw_t, gt):
    b, _, t, _ = ks.shape
    n16 = kc.shape[2]
    n_slc = t // SLC_BLOCK
    cols = N_GROUP * Q_BLOCK
    assert (t // SLC_TILE) % 2 == 0
    whole =lambda r, w: pl.BlockSpec((1, 1, r, w), lambda i, h, j: (i, h, 0, 0))
    kern = functools.partial(_nsa_attn_t_kernel, n_slc=n_slc)
    return pl.pallas_call(
        kern,
        grid=(b, N_KV_HEADS, t // Q_BLOCK),
        in_specs=[pl.BlockSpec((1, 1, HEAD_DIM, cols), lambda i, h, j: (i, h, 0, j)),
                  whole(n16, LANES), whole(HEAD_DIM, n16),
                  whole(t, KS_AUG), whole(V_ROWS, t), whole(t, HEAD_DIM), whole(V_ROWS, t),
                  pl.BlockSpec((1, Q_BLOCK, LANES), lambda i, h, j: (i, j, 0))],
        out_specs=pl.BlockSpec((1, Q_BLOCK, N_GROUP * HEAD_DIM), lambda i, h, j: (i, j, h)),
        out_shape=jax.ShapeDtypeStruct((b, t, N_WIDTH), F32),
        scratch_shapes=[pltpu.VMEM((SLC_TILE, cols), F32), pltpu.VMEM((SLC_TILE, cols), F32),
                        pltpu.VMEM((SLC_TILE, cols), BF16), pltpu.VMEM((SLC_TILE, cols), BF16)],
        compiler_params=_params("parallel", "parallel", "arbitrary"),
        name="nsa_attention",
    )(q_t, kc, vc_t, ks, vs_t, kw, vw_t, gt)


def _nsa(pq, pkv, gt, qk_norm, cmp_pos, cmp_w1, cmp_w2):
    q_t, ks, vs_t, kw, vw_t = _nsa_prep_t(pq, pkv, qk_norm)
    kc, vc_t = _nsa_compress_t(pkv, cmp_pos, cmp_w1, cmp_w2, qk_norm[1])
    return _nsa_attention_t(q_t, kc, vc_t, ks, vs_t, kw, vw_t, gt)


def _mixer(x, mod, nw, w_in, w_out, m_gate_b, m_norm_w, g_conv_w, g_a_log, g_dt_bias, g_norm_w,
           n_qk_norm, n_cmp_pos, n_cmp_w1, n_cmp_w2):
    pm, pg, pq, pkv, gt = _in_proj(x, mod, nw, _permute_w_in(w_in))
    y_m = _mlstm(pm, gt, m_gate_b, m_norm_w)
    y_g = _gdn(pg, gt, g_conv_w, g_a_log, g_dt_bias, g_norm_w)
    y_n = _nsa(pq, pkv, gt, n_qk_norm, n_cmp_pos, n_cmp_w1, n_cmp_w2)
    return _out_proj(x, mod, y_m, y_g, y_n, w_out)


def kernel(x, c, ada_w, ada_b, norm_w, ffn_w_up, ffn_w_down, w_in, w_out, mlstm_gate_b, mlstm_norm_w,
           gdn_conv_w, gdn_a_log, gdn_dt_bias, gdn_norm_w, nsa_qk_norm, nsa_cmp_pos, nsa_cmp_w1, nsa_cmp_w2):
    depth = ada_w.shape[0]
    b = x.shape[0]
    mods = _modulation(c, ada_w, ada_b).reshape(depth, b, 9, D_MODEL)
    for l in range(depth):
        mod = mods[l]
        x = _ffn(x, mod, norm_w[l, 0], ffn_w_up[l, 0].astype(BF16), ffn_w_down[l, 0].astype(BF16), 0)
        x = _mixer(x, mod, norm_w[l, 1], w_in[l], w_out[l], mlstm_gate_b[l], mlstm_norm_w[l],
                   gdn_conv_w[l], gdn_a_log[l], gdn_dt_bias[l], gdn_norm_w[l],
                   nsa_qk_norm[l], nsa_cmp_pos[l], nsa_cmp_w1[l], nsa_cmp_w2[l])
        x = _ffn(x, mod, norm_w[l, 2], ffn_w_up[l, 1].astype(BF16), ffn_w_down[l, 1].astype(BF16), 6)
    return x
```

```python
import functools

import jax
import jax.numpy as jnp
from jax import lax
from jax.experimental import pallas as pl
from jax.experimental.pallas import tpu as pltpu

F32 = jnp.float32
BF16 = jnp.bfloat16

D_MODEL = 1024
HEAD_DIM = 64
D_FF = 2816
EPS = 1e-6
NEG = -1e30
BIG = 1e30

M_HEADS = 4
G_HEADS = 4
N_HEADS = 8
N_KV_HEADS = 2
N_GROUP = N_HEADS // N_KV_HEADS
CHUNK = 64
GDN_PAR = 4
MLSTM_PAR = 4
CONV_K = 4
CMP_BLOCK = 32
CMP_STRIDE = 16
CMP_HIDDEN = 256
SLC_BLOCK = 64
N_SELECTED = 16
WINDOW = 512
Q_BLOCK = 128
ROPE_THETA = 10000.0

M_WIDTH = M_HEADS * HEAD_DIM
G_WIDTH = G_HEADS * HEAD_DIM
N_WIDTH = N_HEADS * HEAD_DIM
KV_WIDTH = N_KV_HEADS * HEAD_DIM
M_IN = 4 * M_WIDTH + 2 * M_HEADS
G_IN = 4 * G_WIDTH + 2 * G_HEADS
N_IN = N_WIDTH + 6 * KV_WIDTH + 3 * N_HEADS

LANES = 128
FF_TILE = 256
VMEM_LIMIT = 56 * 1024 * 1024

GT_M = 0
GT_G = 8
GT_N = 16


def _params(*sem):
    return pltpu.CompilerParams(dimension_semantics=sem, vmem_limit_bytes=VMEM_LIMIT)


def _dot(a, b):
    return jnp.dot(a, b, preferred_element_type=F32)


def _dot_nt(a, b):
    return lax.dot_general(a, b, (((1,), (1,)), ((), ())), preferred_element_type=F32)


def _dot_tn(a, b):
    return lax.dot_general(a, b, (((0,), (0,)), ((), ())), preferred_element_type=F32)


def _split_bf16(x, n):
    parts, r = [], x
    for _ in range(n):
        p = r.astype(BF16)
        parts.append(p)
        r = r - p.astype(F32)
    return parts


def _dot01_l(m01, x, n=3):
    acc = None
    for p in _split_bf16(x, n):
        t = _dot(m01, p)
        acc = t if acc is None else acc + t
    return acc


def _dot01_r(x, m01, n=3):
    acc = None
    for p in _split_bf16(x, n):
        t = _dot(p, m01)
        acc = t if acc is None else acc + t
    return acc


def _iota(shape, dim):
    return lax.broadcasted_iota(jnp.int32, shape, dim)


def _ind(cond, dtype=F32):
    return jnp.where(cond, 1.0, 0.0).astype(dtype)


def _softplus(x):
    return jnp.maximum(x, 0.0) + jnp.log1p(jnp.exp(-jnp.abs(x)))


def _group_ones(width):
    r, c = _iota((width, width), 0), _iota((width, width), 1)
    return _ind((r >> 6) == (c >> 6), BF16)


def _rope_lanes(x, cosf, sinf):
    lane = _iota(x.shape, 1)
    partner = jnp.where((lane & 63) < 32, pltpu.roll(x, 96, 1), pltpu.roll(x, 32, 1))
    return x * cosf + partner * sinf


def _mod_kernel(ct_ref, w_ref, b_ref, o_ref):
    ct = ct_ref[...]
    act = ct * jax.nn.sigmoid(ct)
    w = w_ref[0]
    rows = [jnp.sum(w * act[:, b:b + 1], axis=0, keepdims=True) for b in range(ct.shape[1])]
    o_ref[0] = jnp.concatenate(rows, axis=0) + b_ref[0]


def _modulation(c, ada_w, ada_b):
    depth, d, n = ada_w.shape
    b = c.shape[0]
    tn = 1024
    return pl.pallas_call(
        _mod_kernel,
        grid=(depth, n // tn),
        in_specs=[pl.BlockSpec((d, b), lambda l, j: (0, 0)),
                  pl.BlockSpec((1, d, tn), lambda l, j: (l, 0, j)),
                  pl.BlockSpec((1, 1, tn), lambda l, j: (l, 0, j))],
        out_specs=pl.BlockSpec((1, b, tn), lambda l, j: (l, 0, j)),
        out_shape=jax.ShapeDtypeStruct((depth, b, n), F32),
        compiler_params=_params("parallel", "parallel"),
        name="adaln_mod",
    )(c.T, ada_w, ada_b.reshape(depth, 1, n))


def _adaln(x, nw, shift, scale):
    y = x * lax.rsqrt(jnp.mean(x * x, axis=-1, keepdims=True) + EPS) * nw
    return y * (1.0 + scale) + shift


def _ffn_kernel(x_ref, mod_ref, nw_ref, wup_ref, wdn_ref, o_ref, *, row0, res_w):
    x = x_ref[0]
    shift = mod_ref[0, row0:row0 + 1, :]
    scale = mod_ref[0, row0 + 1:row0 + 2, :]
    gate = mod_ref[0, row0 + 2:row0 + 3, :]
    h = _adaln(x, nw_ref[...], shift, scale).astype(BF16)
    acc = jnp.zeros(x.shape, F32)
    for c in range(D_FF // FF_TILE):
        lo = c * FF_TILE
        g = _dot(h, wup_ref[:, lo:lo + FF_TILE])
        u = _dot(h, wup_ref[:, D_FF + lo:D_FF + lo + FF_TILE])
        a = (jax.nn.silu(g) * u).astype(BF16)
        acc = acc + _dot(a, wdn_ref[lo:lo + FF_TILE, :])
    o_ref[0] = x + (res_w * gate) * acc


def _ffn(x, mod, nw, w_up, w_down, row0):
    b, t, d = x.shape
    tm = min(512, t)
    kern = functools.partial(_ffn_kernel, row0=row0, res_w=0.5)
    return pl.pallas_call(
        kern,
        grid=(b, t // tm),
        in_specs=[pl.BlockSpec((1, tm, d), lambda i, j: (i, j, 0)),
                  pl.BlockSpec((1, 9, d), lambda i, j: (i, 0, 0)),
                  pl.BlockSpec((1, d), lambda i, j: (0, 0)),
                  pl.BlockSpec((d, 2 * D_FF), lambda i, j: (0, 0), pipeline_mode=pl.Buffered(1)),
                  pl.BlockSpec((D_FF, d), lambda i, j: (0, 0), pipeline_mode=pl.Buffered(1))],
        out_specs=pl.BlockSpec((1, tm, d), lambda i, j: (i, j, 0)),
        out_shape=jax.ShapeDtypeStruct(x.shape, F32),
        compiler_params=_params("parallel", "parallel"),
        name="ffn",
    )(x, mod, nw.reshape(1, d), w_up, w_down)


def _inproj_kernel(x_ref, mod_ref, nw_ref, w_ref, pm_ref, pg_ref, pq_ref, pkv_ref, gt_ref):
    x = x_ref[0]
    h = _adaln(x, nw_ref[...], mod_ref[0, 3:4, :], mod_ref[0, 4:5, :]).astype(BF16)
    off = 0
    for ref in (pm_ref, pg_ref, pq_ref, pkv_ref, gt_ref):
        wd = ref.shape[-1]
        ref[0] = _dot(h, w_ref[:, off:off + wd])
        off += wd


IN_PERM_WIDTH = 4 * M_WIDTH + 4 * G_WIDTH + N_WIDTH + 6 * KV_WIDTH + LANES


def _permute_w_in(w_in):
    g0, n0 = M_IN, M_IN + G_IN
    gates = jnp.concatenate([w_in[:, 4 * M_WIDTH:M_IN], w_in[:, g0 + 4 * G_WIDTH:g0 + G_IN],
                             w_in[:, n0 + N_WIDTH + 6 * KV_WIDTH:n0 + N_IN]], axis=1)
    gates = jnp.pad(gates, ((0, 0), (0, LANES - gates.shape[1])))
    return jnp.concatenate([w_in[:, :4 * M_WIDTH], w_in[:, g0:g0 + 4 * G_WIDTH],
                            w_in[:, n0:n0 + N_WIDTH + 6 * KV_WIDTH], gates], axis=1).astype(BF16)


def _in_proj(x, mod, nw, w_perm):
    b, t, d = x.shape
    tm = min(512, t)
    widths = (4 * M_WIDTH, 4 * G_WIDTH, N_WIDTH, 6 * KV_WIDTH, LANES)
    return pl.pallas_call(
        _inproj_kernel,
        grid=(b, t // tm),
        in_specs=[pl.BlockSpec((1, tm, d), lambda i, j: (i, j, 0)),
                  pl.BlockSpec((1, 9, d), lambda i, j: (i, 0, 0)),
                  pl.BlockSpec((1, d), lambda i, j: (0, 0)),
                  pl.BlockSpec((d, IN_PERM_WIDTH), lambda i, j: (0, 0), pipeline_mode=pl.Buffered(1))],
        out_specs=[pl.BlockSpec((1, tm, w), lambda i, j: (i, j, 0)) for w in widths],
        out_shape=[jax.ShapeDtypeStruct((b, t, w), F32) for w in widths],
        compiler_params=_params("parallel", "parallel"),
        name="mixer_in_proj",
    )(x, mod, nw.reshape(1, d), w_perm)


def _outproj_kernel(x_ref, mod_ref, ym_ref, yg_ref, yn_ref, w_ref, o_ref):
    y = jnp.concatenate([ym_ref[0], yg_ref[0], yn_ref[0]], axis=-1).astype(BF16)
    o_ref[0] = x_ref[0] + mod_ref[0, 5:6, :] * _dot(y, w_ref[...])


def _out_proj(x, mod, y_m, y_g, y_n, w_out):
    b, t, d = x.shape
    tm = min(512, t)
    row = lambda w: pl.BlockSpec((1, tm, w), lambda i, j: (i, j, 0))
    return pl.pallas_call(
        _outproj_kernel,
        grid=(b, t // tm),
        in_specs=[row(d), pl.BlockSpec((1, 9, d), lambda i, j: (i, 0, 0)),
                  row(M_WIDTH), row(G_WIDTH), row(N_WIDTH),
                  pl.BlockSpec((d, d), lambda i, j: (0, 0), pipeline_mode=pl.Buffered(1))],
        out_specs=row(d),
        out_shape=jax.ShapeDtypeStruct(x.shape, F32),
        compiler_params=_params("parallel", "parallel"),
        name="mixer_out_proj",
    )(x, mod, y_m, y_g, y_n, w_out.astype(BF16))


def _chunk_masks():
    r, c = _iota((CHUNK, CHUNK), 0), _iota((CHUNK, CHUNK), 1)
    return r, c


def _mlstm_kernel(pm_ref, gt_ref, gb_ref, nw_ref, o_ref, c_scr, m_scr, *, n_chunks):
    @pl.when(pl.program_id(1) == 0)
    def _():
        c_scr[...] = jnp.zeros(c_scr.shape, F32)
        m_scr[...] = jnp.zeros(m_scr.shape, F32)

    r, c = _chunk_masks()
    causal = r >= c
    low = _ind(causal, BF16)
    low_ones = jnp.concatenate([low, jnp.ones((CHUNK, CHUNK), BF16)], axis=1)
    strict_up = _ind(r > c)
    eye = _ind(r == c)
    nw = nw_ref[...]
    gb = gb_ref[...]

    ones_v = jnp.ones((CHUNK, HEAD_DIM), BF16)
    heads = range(M_HEADS)

    def chunk_group(cp, carry):
        rows = [pl.ds(pl.multiple_of((cp * MLSTM_PAR + j) * CHUNK, CHUNK), CHUNK) for j in range(MLSTM_PAR)]
        items = [(j, h) for j in range(MLSTM_PAR) for h in heads]
        every = lambda f, *cols: [f(*args) for args in zip(*cols)]
        g_blk = [gt_ref[0, rw, :] + gb for rw in rows]
        lf_blk = [-_softplus(-g) for g in g_blk]
        cum_blk = [_dot01_l(low, x) for x in lf_blk]
        pick = lambda blk, lane: [blk[j][:, lane + h:lane + h + 1] for j, h in items]
        li_c, lf_c, cum_c = pick(g_blk, GT_M), pick(lf_blk, GT_M + M_HEADS), pick(cum_blk, GT_M + M_HEADS)
        head = lambda k: [pm_ref[0, rows[j], k * M_WIDTH + h * HEAD_DIM:k * M_WIDTH + (h + 1) * HEAD_DIM]
                          for j, h in items]
        q = every(lambda x: (x * (HEAD_DIM ** -0.5)).astype(BF16), head(0))
        k = head(1)
        v_aug = every(lambda x: jnp.concatenate([x.astype(BF16), ones_v], axis=1), head(2))
        og = every(jax.nn.sigmoid, head(3))
        dmat = every(lambda f, i: _dot01_l(low_ones, jnp.concatenate([f * strict_up, i * eye], axis=0)), lf_c, li_c)
        dmat = every(lambda x: jnp.where(causal, x, NEG), dmat)
        d_max = every(lambda x: jnp.max(x, axis=-1, keepdims=True), dmat)
        qk = every(lambda a, b: _dot_nt(a, b.astype(BF16)), q, k)
        g_tot = every(lambda x: x[CHUNK - 1:CHUNK, :], cum_c)
        a = every(lambda gt, cm, li: gt - cm + li, g_tot, cum_c, li_c)
        m_loc = every(lambda x: jnp.max(x, axis=0, keepdims=True), a)
        kw = every(lambda kk, x, ml: (kk * jnp.exp(x - ml)).astype(BF16), k, a, m_loc)
        c_loc = every(_dot_tn, kw, v_aug)
        c_prev = [c_scr[h] for h in heads]
        m_prev = [m_scr[h] for h in heads]
        for j in range(MLSTM_PAR):
            sel = [j * M_HEADS + h for h in heads]
            m_inter = [cum_c[i] + m_prev[h] for h, i in enumerate(sel)]
            m_t = [jnp.maximum(m_inter[h], d_max[i]) for h, i in enumerate(sel)]
            s = [(qk[i] * jnp.exp(dmat[i] - m_t[h])).astype(BF16) for h, i in enumerate(sel)]
            inter = [jnp.exp(m_inter[h] - m_t[h]) for h in heads]
            nd = [_dot(s[h], v_aug[i]) + inter[h] * _dot(q[i], c_prev[h].astype(BF16)) for h, i in enumerate(sel)]
            den = [jnp.maximum(jnp.abs(nd[h]), jnp.exp(-m_t[h])) for h in heads]
            hh = [(nd[h] / pltpu.roll(den[h], HEAD_DIM, 1))[:, :HEAD_DIM] for h in heads]
            hh = [x * lax.rsqrt(jnp.mean(x * x, axis=-1, keepdims=True) + EPS) * nw for x in hh]
            m_new = [jnp.maximum(g_tot[i] + m_prev[h], m_loc[i]) for h, i in enumerate(sel)]
            s_old = [jnp.exp(g_tot[i] + m_prev[h] - m_new[h]) for h, i in enumerate(sel)]
            s_new = [jnp.exp(m_loc[i] - m_new[h]) for h, i in enumerate(sel)]
            for h, i in enumerate(sel):
                o_ref[0, rows[j], h * HEAD_DIM:(h + 1) * HEAD_DIM] = og[i] * hh[h]
            c_prev = [s_old[h] * c_prev[h] + s_new[h] * c_loc[i] for h, i in enumerate(sel)]
            m_prev = m_new
        for h in heads:
            c_scr[h] = c_prev[h]
            m_scr[h] = m_prev[h]
        return carry

    lax.fori_loop(0, n_chunks // MLSTM_PAR, chunk_group, 0)


def _mlstm(pm, gt, gate_b, norm_w):
    b, t, _ = pm.shape
    tc = min(512, t)
    gb = jnp.zeros((1, LANES), F32).at[0, GT_M:GT_M + 2 * M_HEADS].set(gate_b.reshape(-1))
    kern = functools.partial(_mlstm_kernel, n_chunks=tc // CHUNK)
    return pl.pallas_call(
        kern,
        grid=(b, t // tc),
        in_specs=[pl.BlockSpec((1, tc, 4 * M_WIDTH), lambda i, j: (i, j, 0)),
                  pl.BlockSpec((1, tc, LANES), lambda i, j: (i, j, 0)),
                  pl.BlockSpec((1, LANES), lambda i, j: (0, 0)),
                  pl.BlockSpec((1, HEAD_DIM), lambda i, j: (0, 0))],
        out_specs=pl.BlockSpec((1, tc, M_WIDTH), lambda i, j: (i, j, 0)),
        out_shape=jax.ShapeDtypeStruct((b, t, M_WIDTH), F32),
        scratch_shapes=[pltpu.VMEM((M_HEADS, HEAD_DIM, 2 * HEAD_DIM), F32),
                        pltpu.VMEM((M_HEADS, 1, 1), F32)],
        compiler_params=_params("parallel", "arbitrary"),
        name="mlstm",
    )(pm, gt, gb, norm_w.reshape(1, HEAD_DIM))


def _gdn_kernel(pg_ref, gt_ref, cw_ref, hp_ref, nw_ref, o_ref, xbuf, qkv_scr, la_scr, s_scr,
                lin_scr, add_scr, g_scr, *, tc):
    first = pl.program_id(1) == 0

    @pl.when(first)
    def _():
        s_scr[...] = jnp.zeros(s_scr.shape, F32)
        xbuf[0:8, :] = jnp.zeros((8, 3 * G_WIDTH), F32)

    @pl.when(jnp.logical_not(first))
    def _():
        xbuf[0:8, :] = xbuf[tc:tc + 8, :]

    xbuf[8:8 + tc, :] = pg_ref[0, :, 0:3 * G_WIDTH]
    acc = None
    for kk in range(CONV_K):
        term = xbuf[8 - (CONV_K - 1) + kk:8 - (CONV_K - 1) + kk + tc, :] * cw_ref[kk:kk + 1, :]
        acc = term if acc is None else acc + term
    act = acc * jax.nn.sigmoid(acc)
    gones = _group_ones(G_WIDTH)
    for part in range(2):
        xx = act[:, part * G_WIDTH:(part + 1) * G_WIDTH]
        ssq = _dot01_r(xx * xx, gones, 2)
        xx = xx * lax.rsqrt(ssq + EPS)
        if part == 0:
            xx = xx * (HEAD_DIM ** -0.5)
        qkv_scr[:, part * G_WIDTH:(part + 1) * G_WIDTH] = xx
    qkv_scr[:, 2 * G_WIDTH:3 * G_WIDTH] = act[:, 2 * G_WIDTH:3 * G_WIDTH]
    gate = gt_ref[0]
    neg_rate = -jnp.exp(hp_ref[0:1, :])
    la_scr[...] = neg_rate * _softplus(gate + hp_ref[1:2, :])

    r, c = _chunk_masks()
    low = _ind(r >= c, BF16)
    strict_low = r > c
    incl_low = r >= c
    strict_up = _ind(r > c)
    eye = _ind(r == c)
    base_mask = _ind(((r >> 2) == (c >> 2)) & (r > c))
    merge_masks = [_ind((((r >> s) & 1) == 1) & ((c >> s) == (r >> s) - 1)) for s in (2, 3, 4, 5)]
    nw = nw_ref[...]

    def prepare(cp, carry):
        items = [(cp * GDN_PAR + j, h) for j in range(GDN_PAR) for h in range(G_HEADS)]
        every = lambda f, *cols: [f(*args) for args in zip(*cols)]
        rows = [pl.ds(pl.multiple_of((cp * GDN_PAR + j) * CHUNK, CHUNK), CHUNK) for j in range(GDN_PAR)]
        la_blk = [la_scr[rw, :] for rw in rows]
        gam_blk = [_dot01_l(low, x) for x in la_blk]
        beta_blk = [jax.nn.sigmoid(gt_ref[0, rw, :]) for rw in rows]
        pick = lambda blk, lane: [blk[i // G_HEADS][:, lane + h:lane + h + 1] for i, (_, h) in enumerate(items)]
        la_c, gam_c, beta_c = pick(la_blk, GT_G), pick(gam_blk, GT_G), pick(beta_blk, GT_G + G_HEADS)
        head = lambda k: [qkv_scr[rows[i // G_HEADS], k * G_WIDTH + h * HEAD_DIM:k * G_WIDTH + (h + 1) * HEAD_DIM]
                          for i, (_, h) in enumerate(items)]
        q, k, v = head(0), head(1), head(2)
        diff = every(lambda x: _dot01_l(low, x * strict_up), la_c)
        dec_strict = every(lambda x: jnp.exp(jnp.where(strict_low, x, NEG)), diff)
        dec_incl = every(lambda x: jnp.exp(jnp.where(incl_low, x, NEG)), diff)
        kb = every(lambda x: x.astype(BF16), k)
        qk_kk = every(lambda a, b: _dot_nt(jnp.concatenate([a.astype(BF16), b], axis=0), b), q, kb)
        amat = every(lambda b, x, dcy: b * x[CHUNK:] * dcy, beta_c, qk_kk, dec_strict)
        n0 = every(lambda a: (-(a * base_mask)).astype(BF16), amat)
        n0sq = every(lambda n: _dot(n, n).astype(BF16), n0)
        inv = every(lambda n: eye + n.astype(F32), n0)
        inv = every(lambda t, n2: t + _dot(t.astype(BF16), n2), inv, n0sq)
        for mask in merge_masks:
            invb = every(lambda t: t.astype(BF16), inv)
            half = every(lambda tb, a: _dot(tb, (a * mask).astype(BF16)).astype(BF16), invb, amat)
            inv = every(lambda t, hf, tb: t - _dot(hf, tb), inv, half, invb)
        e_gam = every(jnp.exp, gam_c)
        rhs = every(lambda b, e, vv, kk: jnp.concatenate([b * vv, (b * e) * kk], axis=1).astype(BF16),
                    beta_c, e_gam, v, k)
        sol = every(lambda t, x: _dot(t.astype(BF16), x).astype(BF16), inv, rhs)
        gam_last = every(lambda g: g[CHUNK - 1:CHUNK, :], gam_c)
        k_d = every(lambda kk, gl, g: (kk * jnp.exp(gl - g)).astype(BF16), k, gam_last, gam_c)
        p_sol = every(lambda x, dcy, sl: _dot((x[:CHUNK] * dcy).astype(BF16), sl), qk_kk, dec_incl, sol)
        kd_sol = every(_dot_tn, k_d, sol)
        for i, (ci, h) in enumerate(items):
            lin_scr[ci, h] = jnp.concatenate([kd_sol[i][:, HEAD_DIM:], q[i] * e_gam[i] - p_sol[i][:, HEAD_DIM:]],
                                             axis=0).astype(BF16)
            add_scr[ci, h] = jnp.concatenate([kd_sol[i][:, :HEAD_DIM], p_sol[i][:, :HEAD_DIM]], axis=0)
            g_scr[ci, h] = jnp.broadcast_to(jnp.exp(gam_last[i]), (1, HEAD_DIM))
        return carry

    lax.fori_loop(0, tc // (CHUNK * GDN_PAR), prepare, 0)

    def advance(ci, carry):
        r0 = pl.multiple_of(ci * CHUNK, CHUNK)
        heads = range(G_HEADS)
        s_prev = [s_scr[h] for h in heads]
        prod = [_dot(lin_scr[ci, h], s_prev[h].astype(BF16)) for h in heads]
        add = [add_scr[ci, h] for h in heads]
        s_next = [g_scr[ci, h] * s_prev[h] - prod[h][:CHUNK] + add[h][:CHUNK] for h in heads]
        o = [prod[h][CHUNK:] + add[h][CHUNK:] for h in heads]
        o = [x * lax.rsqrt(jnp.mean(x * x, axis=-1, keepdims=True) + EPS) * nw for x in o]
        z = [pg_ref[0, pl.ds(r0, CHUNK), 3 * G_WIDTH + h * HEAD_DIM:3 * G_WIDTH + (h + 1) * HEAD_DIM] for h in heads]
        for h in heads:
            s_scr[h] = s_next[h]
            o_ref[0, pl.ds(r0, CHUNK), h * HEAD_DIM:(h + 1) * HEAD_DIM] = o[h] * (z[h] * jax.nn.sigmoid(z[h]))
        return carry

    lax.fori_loop(0, tc // CHUNK, advance, 0)


def _gdn(pg, gt, conv_w, a_log, dt_bias, norm_w):
    b, t, _ = pg.shape
    tc = min(512, t)
    hp = jnp.zeros((2, LANES), F32)
    hp = hp.at[0, GT_G:GT_G + G_HEADS].set(a_log).at[1, GT_G:GT_G + G_HEADS].set(dt_bias)
    kern = functools.partial(_gdn_kernel, tc=tc)
    return pl.pallas_call(
        kern,
        grid=(b, t // tc),
        in_specs=[pl.BlockSpec((1, tc, 4 * G_WIDTH), lambda i, j: (i, j, 0)),
                  pl.BlockSpec((1, tc, LANES), lambda i, j: (i, j, 0)),
                  pl.BlockSpec((CONV_K, 3 * G_WIDTH), lambda i, j: (0, 0)),
                  pl.BlockSpec((2, LANES), lambda i, j: (0, 0)),
                  pl.BlockSpec((1, HEAD_DIM), lambda i, j: (0, 0))],
        out_specs=pl.BlockSpec((1, tc, G_WIDTH), lambda i, j: (i, j, 0)),
        out_shape=jax.ShapeDtypeStruct((b, t, G_WIDTH), F32),
        scratch_shapes=[pltpu.VMEM((tc + 8, 3 * G_WIDTH), F32),
                        pltpu.VMEM((tc, 3 * G_WIDTH), F32),
                        pltpu.VMEM((tc, LANES), F32),
                        pltpu.VMEM((G_HEADS, HEAD_DIM, HEAD_DIM), F32),
                        pltpu.VMEM((tc // CHUNK, G_HEADS, 2 * CHUNK, HEAD_DIM), BF16),
                        pltpu.VMEM((tc // CHUNK, G_HEADS, 2 * CHUNK, HEAD_DIM), F32),
                        pltpu.VMEM((tc // CHUNK, G_HEADS, 1, HEAD_DIM), F32)],
        compiler_params=_params("parallel", "arbitrary"),
        name="gdn",
    )(pg, gt, conv_w.astype(F32), hp, norm_w.reshape(1, HEAD_DIM))


def _rope_tables(pos):
    half = HEAD_DIM // 2
    inv_freq = jnp.power(ROPE_THETA, -jnp.arange(half, dtype=F32) / half)
    ang = pos.astype(F32)[:, None] * inv_freq[None, :]
    cos, sin = jnp.cos(ang), jnp.sin(ang)
    cosf = jnp.tile(cos, (1, LANES // half))
    sinf = jnp.tile(jnp.concatenate([-sin, sin], axis=1), (1, LANES // HEAD_DIM))
    return cosf, sinf


KS_AUG = LANES + HEAD_DIM


SLC_TILE = 512
WIN_SPAN = Q_BLOCK + WINDOW


V_ROWS = HEAD_DIM + 16
LOG2E = 1.4426950408889634


def _nsa_prep_t_kernel(pq_ref, pkv_ref, cos_ref, sin_ref, nw_ref, q_ref, ks_ref, vs_ref, kw_ref, vw_ref, xc_ref,
                       cmp_scr):
    cosf, sinf = cos_ref[...], sin_ref[...]
    gones = _group_ones(LANES)
    tm = cosf.shape[0]

    for s in range(2):
        cmp_scr[s] = pkv_ref[0, :, s * KV_WIDTH:(s + 1) * KV_WIDTH]
    for p in range(CMP_STRIDE):
        for s in range(2):
            grp = cmp_scr[s, pl.ds(p, tm // CMP_STRIDE, stride=CMP_STRIDE), :]
            for h in range(N_KV_HEADS):
                xc_ref[0, s, h, :, p * HEAD_DIM:(p + 1) * HEAD_DIM] = grp[:, h * HEAD_DIM:(h + 1) * HEAD_DIM]

    def norm_rope(x, w):
        ssq = _dot01_r(x * x, gones, 2)
        y = x * lax.rsqrt(ssq * (1.0 / HEAD_DIM) + EPS) * w
        return _rope_lanes(y, cosf, sinf)

    for s in range(N_WIDTH // LANES):
        qs = norm_rope(pq_ref[0, :, s * LANES:(s + 1) * LANES], nw_ref[0:1, :]) * (HEAD_DIM ** -0.5 * LOG2E)
        qs_t = qs.T.astype(BF16)
        for half in range(2):
            hkv, g = divmod(2 * s + half, N_GROUP)
            for blk in range(tm // Q_BLOCK):
                col = (blk * N_GROUP + g) * Q_BLOCK
                q_ref[0, hkv, :, col:col + Q_BLOCK] = qs_t[half * HEAD_DIM:(half + 1) * HEAD_DIM,
                                                           blk * Q_BLOCK:(blk + 1) * Q_BLOCK]
    kv = lambda i: pkv_ref[0, :, i * KV_WIDTH:(i + 1) * KV_WIDTH]
    tok = pl.program_id(1) * tm + _iota((tm, LANES), 0)
    onehot = _ind((tok >> 6) == _iota((tm, LANES), 1), BF16)
    k_slc = norm_rope(kv(2), nw_ref[2:3, :])
    k_win = norm_rope(kv(4), nw_ref[3:4, :])
    ones = jnp.ones((V_ROWS - HEAD_DIM, tm), BF16)
    for h in range(N_KV_HEADS):
        ks_ref[0, h, :, 0:LANES] = onehot
        ks_ref[0, h, :, LANES:LANES + HEAD_DIM] = k_slc[:, h * HEAD_DIM:(h + 1) * HEAD_DIM].astype(BF16)
        kw_ref[0, h] = k_win[:, h * HEAD_DIM:(h + 1) * HEAD_DIM].astype(BF16)
    for ref, slab in ((vs_ref, kv(3)), (vw_ref, kv(5))):
        v_t = slab.T.astype(BF16)
        for h in range(N_KV_HEADS):
            ref[0, h, 0:HEAD_DIM, :] = v_t[h * HEAD_DIM:(h + 1) * HEAD_DIM]
            ref[0, h, HEAD_DIM:V_ROWS, :] = ones


def _nsa_prep_t(pq, pkv, qk_norm):
    b, t, _ = pq.shape
    assert t // SLC_BLOCK <= LANES
    tm = min(512, t)
    flat = CMP_STRIDE * HEAD_DIM
    cosf, sinf = _rope_tables(jnp.arange(t, dtype=jnp.int32))
    nw = jnp.tile(qk_norm, (1, LANES // HEAD_DIM))
    rows = lambda w: pl.BlockSpec((1, N_KV_HEADS, tm, w), lambda i, j: (i, 0, j, 0))
    cols = lambda r, w: pl.BlockSpec((1, N_KV_HEADS, r, w), lambda i, j: (i, 0, 0, j))
    shp = lambda r, w: jax.ShapeDtypeStruct((b, N_KV_HEADS, r, w), BF16)
    return pl.pallas_call(
        _nsa_prep_t_kernel,
        grid=(b, t // tm),
        in_specs=[pl.BlockSpec((1, tm, N_WIDTH), lambda i, j: (i, j, 0)),
                  pl.BlockSpec((1, tm, 6 * KV_WIDTH), lambda i, j: (i, j, 0)),
                  pl.BlockSpec((tm, LANES), lambda i, j: (j, 0)),
                  pl.BlockSpec((tm, LANES), lambda i, j: (j, 0)),
                  pl.BlockSpec((4, LANES), lambda i, j: (0, 0))],
        out_specs=[cols(HEAD_DIM, N_GROUP * tm), rows(KS_AUG), cols(V_ROWS, tm), rows(HEAD_DIM), cols(V_ROWS, tm),
                   pl.BlockSpec((1, 2, N_KV_HEADS, tm // CMP_STRIDE, flat), lambda i, j: (i, 0, 0, j, 0))],
        out_shape=[shp(HEAD_DIM, N_GROUP * t), shp(t, KS_AUG), shp(V_ROWS, t), shp(t, HEAD_DIM), shp(V_ROWS, t),
                   jax.ShapeDtypeStruct((b, 2, N_KV_HEADS, t // CMP_STRIDE, flat), F32)],
        scratch_shapes=[pltpu.VMEM((2, tm, KV_WIDTH), F32)],
        compiler_params=_params("parallel", "parallel"),
        name="nsa_prep",
    )(pq, pkv, cosf, sinf, nw)


def _compress_t_kernel(x_ref, pos_ref, w1_ref, w2_ref, cos_ref, sin_ref, nw_ref, kc_ref, vc_ref):
    x = x_ref[0, 0, 0]
    half = CMP_STRIDE * HEAD_DIM
    a = _dot((x + pos_ref[0, 0:1, :]).astype(BF16), w1_ref[0, 0:half, :])
    bmat = _dot((x + pos_ref[0, 1:2, :]).astype(BF16), w1_ref[0, half:2 * half, :])
    hid = a + jnp.concatenate([bmat[1:, :], jnp.zeros((1, CMP_HIDDEN), F32)], axis=0)
    hid = hid * jax.nn.sigmoid(hid)
    out = _dot(hid.astype(BF16), w2_ref[0])

    @pl.when(pl.program_id(2) == 0)
    def _():
        ssq = jnp.sum(out * out, axis=-1, keepdims=True)
        normed = out * lax.rsqrt(ssq * (1.0 / HEAD_DIM) + EPS) * nw_ref[...]
        kc_ref[0, 0] = _rope_lanes(normed, cos_ref[...], sin_ref[...]).astype(BF16)

    @pl.when(pl.program_id(2) == 1)
    def _():
        vc_ref[0, 0] = out.T[0:HEAD_DIM].astype(BF16)


def _nsa_compress_t(x, cmp_pos, cmp_w1, cmp_w2, kc_norm):
    b, _, _, n16, flat = x.shape
    pos = cmp_pos.reshape(2, 2, flat)
    w1 = cmp_w1.astype(BF16)
    w2 = jnp.pad(cmp_w2, ((0, 0), (0, 0), (0, LANES - HEAD_DIM))).astype(BF16)
    cosf, sinf = _rope_tables(jnp.arange(n16, dtype=jnp.int32) * CMP_STRIDE + CMP_BLOCK - 1)
    nw = jnp.pad(kc_norm.reshape(1, HEAD_DIM), ((0, 0), (0, LANES - HEAD_DIM)))
    return pl.pallas_call(
        _compress_t_kernel,
        grid=(b, N_KV_HEADS, 2),
        in_specs=[pl.BlockSpec((1, 1, 1, n16, flat), lambda i, h, s: (i, s, h, 0, 0)),
                  pl.BlockSpec((1, 2, flat), lambda i, h, s: (s, 0, 0)),
                  pl.BlockSpec((1, 2 * flat, CMP_HIDDEN), lambda i, h, s: (s, 0, 0)),
                  pl.BlockSpec((1, CMP_HIDDEN, LANES), lambda i, h, s: (s, 0, 0)),
                  pl.BlockSpec((n16, LANES), lambda i, h, s: (0, 0)),
                  pl.BlockSpec((n16, LANES), lambda i, h, s: (0, 0)),
                  pl.BlockSpec((1, LANES), lambda i, h, s: (0, 0))],
        out_specs=[pl.BlockSpec((1, 1, n16, LANES), lambda i, h, s: (i, h, 0, 0)),
                   pl.BlockSpec((1, 1, HEAD_DIM, n16), lambda i, h, s: (i, h, 0, 0))],
        out_shape=[jax.ShapeDtypeStruct((b, N_KV_HEADS, n16, LANES), BF16),
                   jax.ShapeDtypeStruct((b, N_KV_HEADS, HEAD_DIM, n16), BF16)],
        compiler_params=_params("parallel", "parallel", "arbitrary"),
        name="nsa_compress",
    )(x, pos, w1, w2, cosf, sinf, nw)


def _col_max(x):
    return jnp.max(x, axis=0, keepdims=True)


def _nsa_attn_t_kernel(q_ref, kc_ref, vc_ref, ks_ref, vs_ref, kw_ref, vw_ref, gt_ref, o_ref,
                       sa_scr, sb_scr, pa_scr, pb_scr, *, n_slc):
    hkv = pl.program_id(1)
    qi = pl.program_id(2)
    start = qi * Q_BLOCK
    cols = N_GROUP * Q_BLOCK
    q_t = q_ref[0, 0]
    n16 = kc_ref.shape[2]

    def per_head(x):
        return jnp.concatenate([x] * N_GROUP, axis=1)

    def finish(acc):
        return acc[0:HEAD_DIM] * (1.0 / acc[HEAD_DIM:HEAD_DIM + 1])

    s_c = _dot(kc_ref[0, 0][:, :HEAD_DIM], q_t)
    tq_q = start + _iota((n16, Q_BLOCK), 1)
    valid_c = (_iota((n16, Q_BLOCK), 0) * CMP_STRIDE + (CMP_BLOCK - 1)) <= tq_q
    s_c = s_c + per_head(jnp.where(valid_c, 0.0, NEG))
    e_c = jnp.exp2(s_c - _col_max(s_c))
    any_c = per_head(_ind(start + _iota((1, Q_BLOCK), 1) >= CMP_BLOCK - 1))
    p_c = e_c * (any_c / jnp.sum(e_c, axis=0, keepdims=True))
    o_c = _dot(vc_ref[0, 0], p_c.astype(BF16))

    p_sum = p_c[:, 0:Q_BLOCK]
    for g in range(1, N_GROUP):
        p_sum = p_sum + p_c[:, g * Q_BLOCK:(g + 1) * Q_BLOCK]
    sb, cn = _iota((n_slc, n16), 0), _iota((n_slc, n16), 1)
    c_lo, s_lo = cn * CMP_STRIDE, sb * SLC_BLOCK
    ov = jnp.maximum(jnp.minimum(c_lo + CMP_BLOCK, s_lo + SLC_BLOCK) - jnp.maximum(c_lo, s_lo), 0)
    ov = jnp.where(cn < n16 - 1, ov, 0)
    overlap_t = (ov.astype(F32) * (1.0 / CMP_BLOCK)).astype(BF16)
    imp = _dot01_l(overlap_t, p_sum, 2)
    tq_i = start + _iota(imp.shape, 1)
    blk = _iota(imp.shape, 0)
    cur = tq_i >> 6
    forced = (blk == 0) | (blk == cur) | (blk == cur - 1)
    imp = jnp.where(forced, BIG, imp)
    imp = jnp.where(blk * SLC_BLOCK <= tq_i, imp, NEG)
    blk_f = blk.astype(F32)
    sel_t = jnp.zeros(imp.shape, F32)
    for _ in range(min(N_SELECTED, n_slc)):
        top = _col_max(imp)
        first = jnp.min(jnp.where(imp == top, blk_f, float(n_slc)), axis=0, keepdims=True)
        hit = blk_f == first
        sel_t = jnp.where(hit, 1.0, sel_t)
        imp = jnp.where(hit, -3e38, imp)

    w0 = pl.multiple_of(jnp.maximum(start - WINDOW, 0), Q_BLOCK)
    s_w = _dot(kw_ref[0, 0, pl.ds(w0, WIN_SPAN), :], q_t)
    dist = start + _iota((WIN_SPAN, Q_BLOCK), 1) - (w0 + _iota((WIN_SPAN, Q_BLOCK), 0))
    s_w = s_w + per_head(jnp.where((dist >= 0) & (dist < WINDOW), 0.0, NEG))
    p_w = jnp.exp2(s_w - _col_max(s_w)).astype(BF16)
    o_w = finish(_dot(vw_ref[0, 0, :, pl.ds(w0, WIN_SPAN)], p_w))

    own = pl.multiple_of(start, Q_BLOCK)
    s_d = _dot(ks_ref[0, 0, pl.ds(own, Q_BLOCK), LANES:LANES + HEAD_DIM], q_t)
    s_d = s_d + per_head(jnp.where(_iota((Q_BLOCK, Q_BLOCK), 0) <= _iota((Q_BLOCK, Q_BLOCK), 1), 0.0, NEG))
    m_d = _col_max(s_d)
    acc_d = _dot(vs_ref[0, 0, :, pl.ds(own, Q_BLOCK)], jnp.exp2(s_d - m_d).astype(BF16))

    bias_t = (jnp.where(blk < 2 * qi, sel_t, 0.0) - 1.0) * BIG
    if n_slc < LANES:
        bias_t = jnp.concatenate([bias_t, jnp.zeros((LANES - n_slc, Q_BLOCK), F32)], axis=0)
    bias_t = bias_t.astype(BF16)
    q_aug = jnp.concatenate([jnp.concatenate([bias_t] * N_GROUP, axis=1), q_t], axis=0)

    n_tiles = (start + SLC_TILE - 1) // SLC_TILE

    def scores(kt):
        k0 = pl.multiple_of(kt * SLC_TILE, SLC_TILE)
        return _dot(ks_ref[0, 0, pl.ds(k0, SLC_TILE), :], q_aug)

    def weighted(p, kt):
        k0 = pl.multiple_of(kt * SLC_TILE, SLC_TILE)
        return _dot(vs_ref[0, 0, :, pl.ds(k0, SLC_TILE)], p)

    top_tile = ks_ref.shape[2] // SLC_TILE - 1

    def half_step(kt, s_in, s_out, p_in, p_out, a_prev, m, acc):
        s_out[...] = scores(jnp.minimum(kt + 1, top_tile))
        acc = a_prev * acc + weighted(p_in[...], jnp.maximum(kt - 1, 0))
        s = s_in[...]
        m_new = jnp.maximum(m, _col_max(s))
        p_out[...] = jnp.exp2(s - m_new).astype(BF16)
        return jnp.exp2(m - m_new), m_new, acc

    def slc_pair(j, carry):
        carry = half_step(2 * j, sa_scr, sb_scr, pb_scr, pa_scr, *carry)
        return half_step(2 * j + 1, sb_scr, sa_scr, pa_scr, pb_scr, *carry)

    sa_scr[...] = scores(0)
    pb_scr[...] = jnp.zeros((SLC_TILE, cols), BF16)
    n_pairs = (n_tiles + 1) // 2
    a_last, _, acc_s = lax.fori_loop(0, n_pairs, slc_pair, (jnp.ones((1, cols), F32), m_d, acc_d))
    o_s = finish(a_last * acc_s + weighted(pb_scr[...], jnp.maximum(2 * n_pairs - 1, 0)))

    gates_t = jax.nn.sigmoid(gt_ref[0]).T

    def gate_row(branch):
        rows_ = []
        for g in range(N_GROUP):
            c0 = GT_N + branch * N_HEADS + g
            c1 = c0 + N_GROUP
            rows_.append(jnp.where(hkv == 0, gates_t[c0:c0 + 1], gates_t[c1:c1 + 1]))
        return jnp.concatenate(rows_, axis=1)

    y_t = gate_row(0) * o_c + gate_row(1) * o_s + gate_row(2) * o_w
    y = jnp.concatenate([y_t[:, g * Q_BLOCK:(g + 1) * Q_BLOCK] for g in range(N_GROUP)], axis=0)
    o_ref[0] = y.T


def _nsa_attention_t(q_t, kc, vc_t, ks, vs_t, kw, vw_t, gt):
    b, _, t, _ = ks.shape
    n16 = kc.shape[2]
    n_slc = t // SLC_BLOCK
    cols = N_GROUP * Q_BLOCK
    assert (t // SLC_TILE) % 2 == 0
    whole =lambda r, w: pl.BlockSpec((1, 1, r, w), lambda i, h, j: (i, h, 0, 0))
    kern = functools.partial(_nsa_attn_t_kernel, n_slc=n_slc)
    return pl.pallas_call(
        kern,
        grid=(b, N_KV_HEADS, t // Q_BLOCK),
        in_specs=[pl.BlockSpec((1, 1, HEAD_DIM, cols), lambda i, h, j: (i, h, 0, j)),
                  whole(n16, LANES), whole(HEAD_DIM, n16),
                  whole(t, KS_AUG), whole(V_ROWS, t), whole(t, HEAD_DIM), whole(V_ROWS, t),
                  pl.BlockSpec((1, Q_BLOCK, LANES), lambda i, h, j: (i, j, 0))],
        out_specs=pl.BlockSpec((1, Q_BLOCK, N_GROUP * HEAD_DIM), lambda i, h, j: (i, j, h)),
        out_shape=jax.ShapeDtypeStruct((b, t, N_WIDTH), F32),
        scratch_shapes=[pltpu.VMEM((SLC_TILE, cols), F32), pltpu.VMEM((SLC_TILE, cols), F32),
                        pltpu.VMEM((SLC_TILE, cols), BF16), pltpu.VMEM((SLC_TILE, cols), BF16)],
        compiler_params=_params("parallel", "parallel", "arbitrary"),
        name="nsa_attention",
    )(q_t, kc, vc_t, ks, vs_t, kw, vw_t, gt)


def _nsa(pq, pkv, gt, qk_norm, cmp_pos, cmp_w1, cmp_w2):
    q_t, ks, vs_t, kw, vw_t, x_cmp = _nsa_prep_t(pq, pkv, qk_norm)
    kc, vc_t = _nsa_compress_t(x_cmp, cmp_pos, cmp_w1, cmp_w2, qk_norm[1])
    return _nsa_attention_t(q_t, kc, vc_t, ks, vs_t, kw, vw_t, gt)


def _mixer(x, mod, nw, w_in, w_out, m_gate_b, m_norm_w, g_conv_w, g_a_log, g_dt_bias, g_norm_w,
           n_qk_norm, n_cmp_pos, n_cmp_w1, n_cmp_w2):
    pm, pg, pq, pkv, gt = _in_proj(x, mod, nw, _permute_w_in(w_in))
    y_m = _mlstm(pm, gt, m_gate_b, m_norm_w)
    y_g = _gdn(pg, gt, g_conv_w, g_a_log, g_dt_bias, g_norm_w)
    y_n = _nsa(pq, pkv, gt, n_qk_norm, n_cmp_pos, n_cmp_w1, n_cmp_w2)
    return _out_proj(x, mod, y_m, y_g, y_n, w_out)


def kernel(x, c, ada_w, ada_b, norm_w, ffn_w_up, ffn_w_down, w_in, w_out, mlstm_gate_b, mlstm_norm_w,
           gdn_conv_w, gdn_a_log, gdn_dt_bias, gdn_norm_w, nsa_qk_norm, nsa_cmp_pos, nsa_cmp_w1, nsa_cmp_w2):
    depth = ada_w.shape[0]
    b = x.shape[0]
    mods = _modulation(c, ada_w, ada_b).reshape(depth, b, 9, D_MODEL)
    for l in range(depth):
        mod = mods[l]
        x = _ffn(x, mod, norm_w[l, 0], ffn_w_up[l, 0].astype(BF16), ffn_w_down[l, 0].astype(BF16), 0)
        x = _mixer(x, mod, norm_w[l, 1], w_in[l], w_out[l], mlstm_gate_b[l], mlstm_norm_w[l],
                   gdn_conv_w[l], gdn_a_log[l], gdn_dt_bias[l], gdn_norm_w[l],
                   nsa_qk_norm[l], nsa_cmp_pos[l], nsa_cmp_w1[l], nsa_cmp_w2[l])
        x = _ffn(x, mod, norm_w[l, 2], ffn_w_up[l, 1].astype(BF16), ffn_w_down[l, 1].astype(BF16), 6)
    return x
```

```python
import functools

import jax
import jax.numpy as jnp
from jax import lax
from jax.experimental import pallas as pl
from jax.experimental.pallas import tpu as pltpu

F32 = jnp.float32
BF16 = jnp.bfloat16

D_MODEL = 1024
HEAD_DIM = 64
D_FF = 2816
EPS = 1e-6
NEG = -1e30
BIG = 1e30

M_HEADS = 4
G_HEADS = 4
N_HEADS = 8
N_KV_HEADS = 2
N_GROUP = N_HEADS // N_KV_HEADS
CHUNK = 64
GDN_PAR = 4
MLSTM_PAR = 4
CONV_K = 4
CMP_BLOCK = 32
CMP_STRIDE = 16
CMP_HIDDEN = 256
SLC_BLOCK = 64
N_SELECTED = 16
WINDOW = 512
Q_BLOCK = 128
ROPE_THETA = 10000.0

M_WIDTH = M_HEADS * HEAD_DIM
G_WIDTH = G_HEADS * HEAD_DIM
N_WIDTH = N_HEADS * HEAD_DIM
KV_WIDTH = N_KV_HEADS * HEAD_DIM
M_IN = 4 * M_WIDTH + 2 * M_HEADS
G_IN = 4 * G_WIDTH + 2 * G_HEADS
N_IN = N_WIDTH + 6 * KV_WIDTH + 3 * N_HEADS

LANES = 128
FF_TILE = 256
VMEM_LIMIT = 56 * 1024 * 1024

GT_M = 0
GT_G = 8
GT_N = 16


def _params(*sem):
    return pltpu.CompilerParams(dimension_semantics=sem, vmem_limit_bytes=VMEM_LIMIT)


def _dot(a, b):
    return jnp.dot(a, b, preferred_element_type=F32)


def _dot_nt(a, b):
    return lax.dot_general(a, b, (((1,), (1,)), ((), ())), preferred_element_type=F32)


def _dot_tn(a, b):
    return lax.dot_general(a, b, (((0,), (0,)), ((), ())), preferred_element_type=F32)


def _split_bf16(x, n):
    parts, r = [], x
    for _ in range(n):
        p = r.astype(BF16)
        parts.append(p)
        r = r - p.astype(F32)
    return parts


def _dot01_l(m01, x, n=3):
    acc = None
    for p in _split_bf16(x, n):
        t = _dot(m01, p)
        acc = t if acc is None else acc + t
    return acc


def _dot01_r(x, m01, n=3):
    acc = None
    for p in _split_bf16(x, n):
        t = _dot(p, m01)
        acc = t if acc is None else acc + t
    return acc


def _iota(shape, dim):
    return lax.broadcasted_iota(jnp.int32, shape, dim)


def _ind(cond, dtype=F32):
    return jnp.where(cond, 1.0, 0.0).astype(dtype)


def _softplus(x):
    return jnp.maximum(x, 0.0) + jnp.log1p(jnp.exp(-jnp.abs(x)))


def _group_ones(width):
    r, c = _iota((width, width), 0), _iota((width, width), 1)
    return _ind((r >> 6) == (c >> 6), BF16)


def _rope_lanes(x, cosf, sinf):
    lane = _iota(x.shape, 1)
    partner = jnp.where((lane & 63) < 32, pltpu.roll(x, 96, 1), pltpu.roll(x, 32, 1))
    return x * cosf + partner * sinf


def _mod_kernel(ct_ref, w_ref, b_ref, o_ref):
    ct = ct_ref[...]
    act = ct * jax.nn.sigmoid(ct)
    w = w_ref[0]
    rows = [jnp.sum(w * act[:, b:b + 1], axis=0, keepdims=True) for b in range(ct.shape[1])]
    o_ref[0] = jnp.concatenate(rows, axis=0) + b_ref[0]


def _modulation(c, ada_w, ada_b):
    depth, d, n = ada_w.shape
    b = c.shape[0]
    tn = 1024
    return pl.pallas_call(
        _mod_kernel,
        grid=(depth, n // tn),
        in_specs=[pl.BlockSpec((d, b), lambda l, j: (0, 0)),
                  pl.BlockSpec((1, d, tn), lambda l, j: (l, 0, j)),
                  pl.BlockSpec((1, 1, tn), lambda l, j: (l, 0, j))],
        out_specs=pl.BlockSpec((1, b, tn), lambda l, j: (l, 0, j)),
        out_shape=jax.ShapeDtypeStruct((depth, b, n), F32),
        compiler_params=_params("parallel", "parallel"),
        name="adaln_mod",
    )(c.T, ada_w, ada_b.reshape(depth, 1, n))


def _adaln(x, nw, shift, scale):
    y = x * lax.rsqrt(jnp.mean(x * x, axis=-1, keepdims=True) + EPS) * nw
    return y * (1.0 + scale) + shift


def _ffn_kernel(x_ref, mod_ref, nw_ref, wup_ref, wdn_ref, *rest, row0, res_w):
    x = x_ref[0]
    if len(rest) > 1:
        ym_ref, yg_ref, yn_ref, wout_ref, o_ref = rest
        y = jnp.concatenate([ym_ref[0], yg_ref[0], yn_ref[0]], axis=-1).astype(BF16)
        x = x + mod_ref[0, 5:6, :] * _dot(y, wout_ref[...])
    else:
        o_ref, = rest
    shift = mod_ref[0, row0:row0 + 1, :]
    scale = mod_ref[0, row0 + 1:row0 + 2, :]
    gate = mod_ref[0, row0 + 2:row0 + 3, :]
    h = _adaln(x, nw_ref[...], shift, scale).astype(BF16)
    acc = jnp.zeros(x.shape, F32)
    for c in range(D_FF // FF_TILE):
        lo = c * FF_TILE
        g = _dot(h, wup_ref[:, lo:lo + FF_TILE])
        u = _dot(h, wup_ref[:, D_FF + lo:D_FF + lo + FF_TILE])
        a = (jax.nn.silu(g) * u).astype(BF16)
        acc = acc + _dot(a, wdn_ref[lo:lo + FF_TILE, :])
    o_ref[0] = x + (res_w * gate) * acc


def _ffn(x, mod, nw, w_up, w_down, row0, mixer=None):
    b, t, d = x.shape
    tm = min(512, t)
    kern = functools.partial(_ffn_kernel, row0=row0, res_w=0.5)
    row = lambda w: pl.BlockSpec((1, tm, w), lambda i, j: (i, j, 0))
    const = lambda r, w: pl.BlockSpec((r, w), lambda i, j: (0, 0), pipeline_mode=pl.Buffered(1))
    in_specs = [row(d), pl.BlockSpec((1, 9, d), lambda i, j: (i, 0, 0)), pl.BlockSpec((1, d), lambda i, j: (0, 0)),
                const(d, 2 * D_FF), const(D_FF, d)]
    args = [x, mod, nw.reshape(1, d), w_up, w_down]
    if mixer is not None:
        y_m, y_g, y_n, w_out = mixer
        in_specs += [row(M_WIDTH), row(G_WIDTH), row(N_WIDTH), const(d, d)]
        args += [y_m, y_g, y_n, w_out.astype(BF16)]
    return pl.pallas_call(
        kern,
        grid=(b, t // tm),
        in_specs=in_specs,
        out_specs=row(d),
        out_shape=jax.ShapeDtypeStruct(x.shape, F32),
        compiler_params=_params("parallel", "parallel"),
        name="ffn",
    )(*args)


def _inproj_kernel(x_ref, mod_ref, nw_ref, w_ref, pm_ref, pg_ref, pq_ref, pkv_ref, gt_ref):
    x = x_ref[0]
    h = _adaln(x, nw_ref[...], mod_ref[0, 3:4, :], mod_ref[0, 4:5, :]).astype(BF16)
    off = 0
    for ref in (pm_ref, pg_ref, pq_ref, pkv_ref, gt_ref):
        wd = ref.shape[-1]
        ref[0] = _dot(h, w_ref[:, off:off + wd])
        off += wd


IN_PERM_WIDTH = 4 * M_WIDTH + 4 * G_WIDTH + N_WIDTH + 6 * KV_WIDTH + LANES


def _permute_w_in(w_in):
    g0, n0 = M_IN, M_IN + G_IN
    gates = jnp.concatenate([w_in[:, 4 * M_WIDTH:M_IN], w_in[:, g0 + 4 * G_WIDTH:g0 + G_IN],
                             w_in[:, n0 + N_WIDTH + 6 * KV_WIDTH:n0 + N_IN]], axis=1)
    gates = jnp.pad(gates, ((0, 0), (0, LANES - gates.shape[1])))
    return jnp.concatenate([w_in[:, :4 * M_WIDTH], w_in[:, g0:g0 + 4 * G_WIDTH],
                            w_in[:, n0:n0 + N_WIDTH + 6 * KV_WIDTH], gates], axis=1).astype(BF16)


def _in_proj(x, mod, nw, w_perm):
    b, t, d = x.shape
    tm = min(512, t)
    widths = (4 * M_WIDTH, 4 * G_WIDTH, N_WIDTH, 6 * KV_WIDTH, LANES)
    return pl.pallas_call(
        _inproj_kernel,
        grid=(b, t // tm),
        in_specs=[pl.BlockSpec((1, tm, d), lambda i, j: (i, j, 0)),
                  pl.BlockSpec((1, 9, d), lambda i, j: (i, 0, 0)),
                  pl.BlockSpec((1, d), lambda i, j: (0, 0)),
                  pl.BlockSpec((d, IN_PERM_WIDTH), lambda i, j: (0, 0), pipeline_mode=pl.Buffered(1))],
        out_specs=[pl.BlockSpec((1, tm, w), lambda i, j: (i, j, 0)) for w in widths],
        out_shape=[jax.ShapeDtypeStruct((b, t, w), F32) for w in widths],
        compiler_params=_params("parallel", "parallel"),
        name="mixer_in_proj",
    )(x, mod, nw.reshape(1, d), w_perm)


def _chunk_masks():
    r, c = _iota((CHUNK, CHUNK), 0), _iota((CHUNK, CHUNK), 1)
    return r, c


def _mlstm_kernel(pm_ref, gt_ref, gb_ref, nw_ref, o_ref, c_scr, m_scr, *, n_chunks):
    @pl.when(pl.program_id(1) == 0)
    def _():
        c_scr[...] = jnp.zeros(c_scr.shape, F32)
        m_scr[...] = jnp.zeros(m_scr.shape, F32)

    r, c = _chunk_masks()
    causal = r >= c
    low = _ind(causal, BF16)
    low_ones = jnp.concatenate([low, jnp.ones((CHUNK, CHUNK), BF16)], axis=1)
    strict_up = _ind(r > c)
    eye = _ind(r == c)
    nw = nw_ref[...]
    gb = gb_ref[...]

    ones_v = jnp.ones((CHUNK, HEAD_DIM), BF16)
    heads = range(M_HEADS)

    def chunk_group(cp, carry):
        rows = [pl.ds(pl.multiple_of((cp * MLSTM_PAR + j) * CHUNK, CHUNK), CHUNK) for j in range(MLSTM_PAR)]
        items = [(j, h) for j in range(MLSTM_PAR) for h in heads]
        every = lambda f, *cols: [f(*args) for args in zip(*cols)]
        g_blk = [gt_ref[0, rw, :] + gb for rw in rows]
        lf_blk = [-_softplus(-g) for g in g_blk]
        cum_blk = [_dot01_l(low, x) for x in lf_blk]
        pick = lambda blk, lane: [blk[j][:, lane + h:lane + h + 1] for j, h in items]
        li_c, lf_c, cum_c = pick(g_blk, GT_M), pick(lf_blk, GT_M + M_HEADS), pick(cum_blk, GT_M + M_HEADS)
        head = lambda k: [pm_ref[0, rows[j], k * M_WIDTH + h * HEAD_DIM:k * M_WIDTH + (h + 1) * HEAD_DIM]
                          for j, h in items]
        q = every(lambda x: (x * (HEAD_DIM ** -0.5)).astype(BF16), head(0))
        k = head(1)
        v_aug = every(lambda x: jnp.concatenate([x.astype(BF16), ones_v], axis=1), head(2))
        og = every(jax.nn.sigmoid, head(3))
        dmat = every(lambda f, i: _dot01_l(low_ones, jnp.concatenate([f * strict_up, i * eye], axis=0)), lf_c, li_c)
        dmat = every(lambda x: jnp.where(causal, x, NEG), dmat)
        d_max = every(lambda x: jnp.max(x, axis=-1, keepdims=True), dmat)
        qk = every(lambda a, b: _dot_nt(a, b.astype(BF16)), q, k)
        g_tot = every(lambda x: x[CHUNK - 1:CHUNK, :], cum_c)
        a = every(lambda gt, cm, li: gt - cm + li, g_tot, cum_c, li_c)
        m_loc = every(lambda x: jnp.max(x, axis=0, keepdims=True), a)
        kw = every(lambda kk, x, ml: (kk * jnp.exp(x - ml)).astype(BF16), k, a, m_loc)
        c_loc = every(_dot_tn, kw, v_aug)
        c_prev = [c_scr[h] for h in heads]
        m_prev = [m_scr[h] for h in heads]
        for j in range(MLSTM_PAR):
            sel = [j * M_HEADS + h for h in heads]
            m_inter = [cum_c[i] + m_prev[h] for h, i in enumerate(sel)]
            m_t = [jnp.maximum(m_inter[h], d_max[i]) for h, i in enumerate(sel)]
            s = [(qk[i] * jnp.exp(dmat[i] - m_t[h])).astype(BF16) for h, i in enumerate(sel)]
            inter = [jnp.exp(m_inter[h] - m_t[h]) for h in heads]
            nd = [_dot(s[h], v_aug[i]) + inter[h] * _dot(q[i], c_prev[h].astype(BF16)) for h, i in enumerate(sel)]
            den = [jnp.maximum(jnp.abs(nd[h]), jnp.exp(-m_t[h])) for h in heads]
            hh = [(nd[h] / pltpu.roll(den[h], HEAD_DIM, 1))[:, :HEAD_DIM] for h in heads]
            hh = [x * lax.rsqrt(jnp.mean(x * x, axis=-1, keepdims=True) + EPS) * nw for x in hh]
            m_new = [jnp.maximum(g_tot[i] + m_prev[h], m_loc[i]) for h, i in enumerate(sel)]
            s_old = [jnp.exp(g_tot[i] + m_prev[h] - m_new[h]) for h, i in enumerate(sel)]
            s_new = [jnp.exp(m_loc[i] - m_new[h]) for h, i in enumerate(sel)]
            for h, i in enumerate(sel):
                o_ref[0, rows[j], h * HEAD_DIM:(h + 1) * HEAD_DIM] = og[i] * hh[h]
            c_prev = [s_old[h] * c_prev[h] + s_new[h] * c_loc[i] for h, i in enumerate(sel)]
            m_prev = m_new
        for h in heads:
            c_scr[h] = c_prev[h]
            m_scr[h] = m_prev[h]
        return carry

    lax.fori_loop(0, n_chunks // MLSTM_PAR, chunk_group, 0)


def _mlstm(pm, gt, gate_b, norm_w):
    b, t, _ = pm.shape
    tc = min(512, t)
    gb = jnp.zeros((1, LANES), F32).at[0, GT_M:GT_M + 2 * M_HEADS].set(gate_b.reshape(-1))
    kern = functools.partial(_mlstm_kernel, n_chunks=tc // CHUNK)
    return pl.pallas_call(
        kern,
        grid=(b, t // tc),
        in_specs=[pl.BlockSpec((1, tc, 4 * M_WIDTH), lambda i, j: (i, j, 0)),
                  pl.BlockSpec((1, tc, LANES), lambda i, j: (i, j, 0)),
                  pl.BlockSpec((1, LANES), lambda i, j: (0, 0)),
                  pl.BlockSpec((1, HEAD_DIM), lambda i, j: (0, 0))],
        out_specs=pl.BlockSpec((1, tc, M_WIDTH), lambda i, j: (i, j, 0)),
        out_shape=jax.ShapeDtypeStruct((b, t, M_WIDTH), F32),
        scratch_shapes=[pltpu.VMEM((M_HEADS, HEAD_DIM, 2 * HEAD_DIM), F32),
                        pltpu.VMEM((M_HEADS, 1, 1), F32)],
        compiler_params=_params("parallel", "arbitrary"),
        name="mlstm",
    )(pm, gt, gb, norm_w.reshape(1, HEAD_DIM))


def _gdn_kernel(pg_ref, gt_ref, cw_ref, hp_ref, nw_ref, o_ref, xbuf, qkv_scr, la_scr, s_scr,
                lin_scr, add_scr, g_scr, *, tc):
    first = pl.program_id(1) == 0

    @pl.when(first)
    def _():
        s_scr[...] = jnp.zeros(s_scr.shape, F32)
        xbuf[0:8, :] = jnp.zeros((8, 3 * G_WIDTH), F32)

    @pl.when(jnp.logical_not(first))
    def _():
        xbuf[0:8, :] = xbuf[tc:tc + 8, :]

    xbuf[8:8 + tc, :] = pg_ref[0, :, 0:3 * G_WIDTH]
    acc = None
    for kk in range(CONV_K):
        term = xbuf[8 - (CONV_K - 1) + kk:8 - (CONV_K - 1) + kk + tc, :] * cw_ref[kk:kk + 1, :]
        acc = term if acc is None else acc + term
    act = acc * jax.nn.sigmoid(acc)
    gones = _group_ones(G_WIDTH)
    for part in range(2):
        xx = act[:, part * G_WIDTH:(part + 1) * G_WIDTH]
        ssq = _dot01_r(xx * xx, gones, 2)
        xx = xx * lax.rsqrt(ssq + EPS)
        if part == 0:
            xx = xx * (HEAD_DIM ** -0.5)
        qkv_scr[:, part * G_WIDTH:(part + 1) * G_WIDTH] = xx
    qkv_scr[:, 2 * G_WIDTH:3 * G_WIDTH] = act[:, 2 * G_WIDTH:3 * G_WIDTH]
    gate = gt_ref[0]
    neg_rate = -jnp.exp(hp_ref[0:1, :])
    la_scr[...] = neg_rate * _softplus(gate + hp_ref[1:2, :])

    r, c = _chunk_masks()
    low = _ind(r >= c, BF16)
    strict_low = r > c
    incl_low = r >= c
    strict_up = _ind(r > c)
    eye = _ind(r == c)
    base_mask = _ind(((r >> 2) == (c >> 2)) & (r > c))
    merge_masks = [_ind((((r >> s) & 1) == 1) & ((c >> s) == (r >> s) - 1)) for s in (2, 3, 4, 5)]
    nw = nw_ref[...]

    def prepare(cp, carry):
        items = [(cp * GDN_PAR + j, h) for j in range(GDN_PAR) for h in range(G_HEADS)]
        every = lambda f, *cols: [f(*args) for args in zip(*cols)]
        rows = [pl.ds(pl.multiple_of((cp * GDN_PAR + j) * CHUNK, CHUNK), CHUNK) for j in range(GDN_PAR)]
        la_blk = [la_scr[rw, :] for rw in rows]
        gam_blk = [_dot01_l(low, x) for x in la_blk]
        beta_blk = [jax.nn.sigmoid(gt_ref[0, rw, :]) for rw in rows]
        pick = lambda blk, lane: [blk[i // G_HEADS][:, lane + h:lane + h + 1] for i, (_, h) in enumerate(items)]
        la_c, gam_c, beta_c = pick(la_blk, GT_G), pick(gam_blk, GT_G), pick(beta_blk, GT_G + G_HEADS)
        head = lambda k: [qkv_scr[rows[i // G_HEADS], k * G_WIDTH + h * HEAD_DIM:k * G_WIDTH + (h + 1) * HEAD_DIM]
                          for i, (_, h) in enumerate(items)]
        q, k, v = head(0), head(1), head(2)
        diff = every(lambda x: _dot01_l(low, x * strict_up), la_c)
        dec_strict = every(lambda x: jnp.exp(jnp.where(strict_low, x, NEG)), diff)
        dec_incl = every(lambda x: jnp.exp(jnp.where(incl_low, x, NEG)), diff)
        kb = every(lambda x: x.astype(BF16), k)
        qk_kk = every(lambda a, b: _dot_nt(jnp.concatenate([a.astype(BF16), b], axis=0), b), q, kb)
        amat = every(lambda b, x, dcy: b * x[CHUNK:] * dcy, beta_c, qk_kk, dec_strict)
        n0 = every(lambda a: (-(a * base_mask)).astype(BF16), amat)
        n0sq = every(lambda n: _dot(n, n).astype(BF16), n0)
        inv = every(lambda n: eye + n.astype(F32), n0)
        inv = every(lambda t, n2: t + _dot(t.astype(BF16), n2), inv, n0sq)
        for mask in merge_masks:
            invb = every(lambda t: t.astype(BF16), inv)
            half = every(lambda tb, a: _dot(tb, (a * mask).astype(BF16)).astype(BF16), invb, amat)
            inv = every(lambda t, hf, tb: t - _dot(hf, tb), inv, half, invb)
        e_gam = every(jnp.exp, gam_c)
        rhs = every(lambda b, e, vv, kk: jnp.concatenate([b * vv, (b * e) * kk], axis=1).astype(BF16),
                    beta_c, e_gam, v, k)
        sol = every(lambda t, x: _dot(t.astype(BF16), x).astype(BF16), inv, rhs)
        gam_last = every(lambda g: g[CHUNK - 1:CHUNK, :], gam_c)
        k_d = every(lambda kk, gl, g: (kk * jnp.exp(gl - g)).astype(BF16), k, gam_last, gam_c)
        p_sol = every(lambda x, dcy, sl: _dot((x[:CHUNK] * dcy).astype(BF16), sl), qk_kk, dec_incl, sol)
        kd_sol = every(_dot_tn, k_d, sol)
        for i, (ci, h) in enumerate(items):
            lin_scr[ci, h] = jnp.concatenate([kd_sol[i][:, HEAD_DIM:], q[i] * e_gam[i] - p_sol[i][:, HEAD_DIM:]],
                                             axis=0).astype(BF16)
            add_scr[ci, h] = jnp.concatenate([kd_sol[i][:, :HEAD_DIM], p_sol[i][:, :HEAD_DIM]], axis=0)
            g_scr[ci, h] = jnp.broadcast_to(jnp.exp(gam_last[i]), (1, HEAD_DIM))
        return carry

    lax.fori_loop(0, tc // (CHUNK * GDN_PAR), prepare, 0)

    def advance(ci, carry):
        r0 = pl.multiple_of(ci * CHUNK, CHUNK)
        heads = range(G_HEADS)
        s_prev = [s_scr[h] for h in heads]
        prod = [_dot(lin_scr[ci, h], s_prev[h].astype(BF16)) for h in heads]
        add = [add_scr[ci, h] for h in heads]
        s_next = [g_scr[ci, h] * s_prev[h] - prod[h][:CHUNK] + add[h][:CHUNK] for h in heads]
        o = [prod[h][CHUNK:] + add[h][CHUNK:] for h in heads]
        o = [x * lax.rsqrt(jnp.mean(x * x, axis=-1, keepdims=True) + EPS) * nw for x in o]
        z = [pg_ref[0, pl.ds(r0, CHUNK), 3 * G_WIDTH + h * HEAD_DIM:3 * G_WIDTH + (h + 1) * HEAD_DIM] for h in heads]
        for h in heads:
            s_scr[h] = s_next[h]
            o_ref[0, pl.ds(r0, CHUNK), h * HEAD_DIM:(h + 1) * HEAD_DIM] = o[h] * (z[h] * jax.nn.sigmoid(z[h]))
        return carry

    lax.fori_loop(0, tc // CHUNK, advance, 0)


def _gdn(pg, gt, conv_w, a_log, dt_bias, norm_w):
    b, t, _ = pg.shape
    tc = min(512, t)
    hp = jnp.zeros((2, LANES), F32)
    hp = hp.at[0, GT_G:GT_G + G_HEADS].set(a_log).at[1, GT_G:GT_G + G_HEADS].set(dt_bias)
    kern = functools.partial(_gdn_kernel, tc=tc)
    return pl.pallas_call(
        kern,
        grid=(b, t // tc),
        in_specs=[pl.BlockSpec((1, tc, 4 * G_WIDTH), lambda i, j: (i, j, 0)),
                  pl.BlockSpec((1, tc, LANES), lambda i, j: (i, j, 0)),
                  pl.BlockSpec((CONV_K, 3 * G_WIDTH), lambda i, j: (0, 0)),
                  pl.BlockSpec((2, LANES), lambda i, j: (0, 0)),
                  pl.BlockSpec((1, HEAD_DIM), lambda i, j: (0, 0))],
        out_specs=pl.BlockSpec((1, tc, G_WIDTH), lambda i, j: (i, j, 0)),
        out_shape=jax.ShapeDtypeStruct((b, t, G_WIDTH), F32),
        scratch_shapes=[pltpu.VMEM((tc + 8, 3 * G_WIDTH), F32),
                        pltpu.VMEM((tc, 3 * G_WIDTH), F32),
                        pltpu.VMEM((tc, LANES), F32),
                        pltpu.VMEM((G_HEADS, HEAD_DIM, HEAD_DIM), F32),
                        pltpu.VMEM((tc // CHUNK, G_HEADS, 2 * CHUNK, HEAD_DIM), BF16),
                        pltpu.VMEM((tc // CHUNK, G_HEADS, 2 * CHUNK, HEAD_DIM), F32),
                        pltpu.VMEM((tc // CHUNK, G_HEADS, 1, HEAD_DIM), F32)],
        compiler_params=_params("parallel", "arbitrary"),
        name="gdn",
    )(pg, gt, conv_w.astype(F32), hp, norm_w.reshape(1, HEAD_DIM))


def _rope_tables(pos):
    half = HEAD_DIM // 2
    inv_freq = jnp.power(ROPE_THETA, -jnp.arange(half, dtype=F32) / half)
    ang = pos.astype(F32)[:, None] * inv_freq[None, :]
    cos, sin = jnp.cos(ang), jnp.sin(ang)
    cosf = jnp.tile(cos, (1, LANES // half))
    sinf = jnp.tile(jnp.concatenate([-sin, sin], axis=1), (1, LANES // HEAD_DIM))
    return cosf, sinf


KS_AUG = LANES + HEAD_DIM


SLC_TILE = 512
WIN_SPAN = Q_BLOCK + WINDOW


V_ROWS = HEAD_DIM + 16
FORCED_BLOCKS = 3
LOG2E = 1.4426950408889634


def _nsa_prep_t_kernel(pq_ref, pkv_ref, cos_ref, sin_ref, nw_ref, q_ref, ks_ref, vs_ref, kw_ref, vw_ref, xc_ref,
                       cmp_scr):
    cosf, sinf = cos_ref[...], sin_ref[...]
    gones = _group_ones(LANES)
    tm = cosf.shape[0]

    for s in range(2):
        cmp_scr[s] = pkv_ref[0, :, s * KV_WIDTH:(s + 1) * KV_WIDTH]
    for p in range(CMP_STRIDE):
        for s in range(2):
            grp = cmp_scr[s, pl.ds(p, tm // CMP_STRIDE, stride=CMP_STRIDE), :]
            for h in range(N_KV_HEADS):
                xc_ref[0, s, h, :, p * HEAD_DIM:(p + 1) * HEAD_DIM] = grp[:, h * HEAD_DIM:(h + 1) * HEAD_DIM]

    def norm_rope(x, w):
        ssq = _dot01_r(x * x, gones, 2)
        y = x * lax.rsqrt(ssq * (1.0 / HEAD_DIM) + EPS) * w
        return _rope_lanes(y, cosf, sinf)

    for s in range(N_WIDTH // LANES):
        qs = norm_rope(pq_ref[0, :, s * LANES:(s + 1) * LANES], nw_ref[0:1, :]) * (HEAD_DIM ** -0.5 * LOG2E)
        qs_t = qs.T.astype(BF16)
        for half in range(2):
            hkv, g = divmod(2 * s + half, N_GROUP)
            for blk in range(tm // Q_BLOCK):
                col = (blk * N_GROUP + g) * Q_BLOCK
                q_ref[0, hkv, :, col:col + Q_BLOCK] = qs_t[half * HEAD_DIM:(half + 1) * HEAD_DIM,
                                                           blk * Q_BLOCK:(blk + 1) * Q_BLOCK]
    kv = lambda i: pkv_ref[0, :, i * KV_WIDTH:(i + 1) * KV_WIDTH]
    tok = pl.program_id(1) * tm + _iota((tm, LANES), 0)
    onehot = _ind((tok >> 6) == _iota((tm, LANES), 1), BF16)
    k_slc = norm_rope(kv(2), nw_ref[2:3, :])
    k_win = norm_rope(kv(4), nw_ref[3:4, :])
    ones = jnp.ones((V_ROWS - HEAD_DIM, tm), BF16)
    for h in range(N_KV_HEADS):
        ks_ref[0, h, :, 0:LANES] = onehot
        ks_ref[0, h, :, LANES:LANES + HEAD_DIM] = k_slc[:, h * HEAD_DIM:(h + 1) * HEAD_DIM].astype(BF16)
        kw_ref[0, h] = k_win[:, h * HEAD_DIM:(h + 1) * HEAD_DIM].astype(BF16)
    for ref, slab in ((vs_ref, kv(3)), (vw_ref, kv(5))):
        v_t = slab.T.astype(BF16)
        for h in range(N_KV_HEADS):
            ref[0, h, 0:HEAD_DIM, :] = v_t[h * HEAD_DIM:(h + 1) * HEAD_DIM]
            ref[0, h, HEAD_DIM:V_ROWS, :] = ones


def _nsa_prep_t(pq, pkv, qk_norm):
    b, t, _ = pq.shape
    assert t // SLC_BLOCK <= LANES
    tm = min(512, t)
    flat = CMP_STRIDE * HEAD_DIM
    cosf, sinf = _rope_tables(jnp.arange(t, dtype=jnp.int32))
    nw = jnp.tile(qk_norm, (1, LANES // HEAD_DIM))
    rows = lambda w: pl.BlockSpec((1, N_KV_HEADS, tm, w), lambda i, j: (i, 0, j, 0))
    cols = lambda r, w: pl.BlockSpec((1, N_KV_HEADS, r, w), lambda i, j: (i, 0, 0, j))
    shp = lambda r, w: jax.ShapeDtypeStruct((b, N_KV_HEADS, r, w), BF16)
    return pl.pallas_call(
        _nsa_prep_t_kernel,
        grid=(b, t // tm),
        in_specs=[pl.BlockSpec((1, tm, N_WIDTH), lambda i, j: (i, j, 0)),
                  pl.BlockSpec((1, tm, 6 * KV_WIDTH), lambda i, j: (i, j, 0)),
                  pl.BlockSpec((tm, LANES), lambda i, j: (j, 0)),
                  pl.BlockSpec((tm, LANES), lambda i, j: (j, 0)),
                  pl.BlockSpec((4, LANES), lambda i, j: (0, 0))],
        out_specs=[cols(HEAD_DIM, N_GROUP * tm), rows(KS_AUG), cols(V_ROWS, tm), rows(HEAD_DIM), cols(V_ROWS, tm),
                   pl.BlockSpec((1, 2, N_KV_HEADS, tm // CMP_STRIDE, flat), lambda i, j: (i, 0, 0, j, 0))],
        out_shape=[shp(HEAD_DIM, N_GROUP * t), shp(t, KS_AUG), shp(V_ROWS, t), shp(t, HEAD_DIM), shp(V_ROWS, t),
                   jax.ShapeDtypeStruct((b, 2, N_KV_HEADS, t // CMP_STRIDE, flat), F32)],
        scratch_shapes=[pltpu.VMEM((2, tm, KV_WIDTH), F32)],
        compiler_params=_params("parallel", "parallel"),
        name="nsa_prep",
    )(pq, pkv, cosf, sinf, nw)


def _compress_t_kernel(x_ref, pos_ref, w1_ref, w2_ref, cos_ref, sin_ref, nw_ref, kc_ref, vc_ref):
    x = x_ref[0, 0, 0]
    half = CMP_STRIDE * HEAD_DIM
    a = _dot((x + pos_ref[0, 0:1, :]).astype(BF16), w1_ref[0, 0:half, :])
    bmat = _dot((x + pos_ref[0, 1:2, :]).astype(BF16), w1_ref[0, half:2 * half, :])
    hid = a + jnp.concatenate([bmat[1:, :], jnp.zeros((1, CMP_HIDDEN), F32)], axis=0)
    hid = hid * jax.nn.sigmoid(hid)
    out = _dot(hid.astype(BF16), w2_ref[0])

    @pl.when(pl.program_id(2) == 0)
    def _():
        ssq = jnp.sum(out * out, axis=-1, keepdims=True)
        normed = out * lax.rsqrt(ssq * (1.0 / HEAD_DIM) + EPS) * nw_ref[...]
        kc_ref[0, 0] = _rope_lanes(normed, cos_ref[...], sin_ref[...]).astype(BF16)

    @pl.when(pl.program_id(2) == 1)
    def _():
        vc_ref[0, 0] = out.T[0:HEAD_DIM].astype(BF16)


def _nsa_compress_t(x, cmp_pos, cmp_w1, cmp_w2, kc_norm):
    b, _, _, n16, flat = x.shape
    pos = cmp_pos.reshape(2, 2, flat)
    w1 = cmp_w1.astype(BF16)
    w2 = jnp.pad(cmp_w2, ((0, 0), (0, 0), (0, LANES - HEAD_DIM))).astype(BF16)
    cosf, sinf = _rope_tables(jnp.arange(n16, dtype=jnp.int32) * CMP_STRIDE + CMP_BLOCK - 1)
    nw = jnp.pad(kc_norm.reshape(1, HEAD_DIM), ((0, 0), (0, LANES - HEAD_DIM)))
    return pl.pallas_call(
        _compress_t_kernel,
        grid=(b, N_KV_HEADS, 2),
        in_specs=[pl.BlockSpec((1, 1, 1, n16, flat), lambda i, h, s: (i, s, h, 0, 0)),
                  pl.BlockSpec((1, 2, flat), lambda i, h, s: (s, 0, 0)),
                  pl.BlockSpec((1, 2 * flat, CMP_HIDDEN), lambda i, h, s: (s, 0, 0)),
                  pl.BlockSpec((1, CMP_HIDDEN, LANES), lambda i, h, s: (s, 0, 0)),
                  pl.BlockSpec((n16, LANES), lambda i, h, s: (0, 0)),
                  pl.BlockSpec((n16, LANES), lambda i, h, s: (0, 0)),
                  pl.BlockSpec((1, LANES), lambda i, h, s: (0, 0))],
        out_specs=[pl.BlockSpec((1, 1, n16, LANES), lambda i, h, s: (i, h, 0, 0)),
                   pl.BlockSpec((1, 1, HEAD_DIM, n16), lambda i, h, s: (i, h, 0, 0))],
        out_shape=[jax.ShapeDtypeStruct((b, N_KV_HEADS, n16, LANES), BF16),
                   jax.ShapeDtypeStruct((b, N_KV_HEADS, HEAD_DIM, n16), BF16)],
        compiler_params=_params("parallel", "parallel", "arbitrary"),
        name="nsa_compress",
    )(x, pos, w1, w2, cosf, sinf, nw)


def _col_max(x):
    return jnp.max(x, axis=0, keepdims=True)


def _nsa_attn_t_kernel(q_ref, kc_ref, vc_ref, ks_ref, vs_ref, kw_ref, vw_ref, gt_ref, o_ref,
                       sa_scr, sb_scr, pa_scr, pb_scr, *, n_slc):
    hkv = pl.program_id(1)
    qi = pl.program_id(2)
    start = qi * Q_BLOCK
    cols = N_GROUP * Q_BLOCK
    q_t = q_ref[0, 0]
    n16 = kc_ref.shape[2]

    def per_head(x):
        return jnp.concatenate([x] * N_GROUP, axis=1)

    def finish(acc):
        return acc[0:HEAD_DIM] * (1.0 / acc[HEAD_DIM:HEAD_DIM + 1])

    def probs(s, m):
        return jnp.exp2(s - m).astype(BF16)

    gates_t = jax.nn.sigmoid(gt_ref[0]).T

    s_c = _dot(kc_ref[0, 0][:, :HEAD_DIM], q_t)
    tq_q = start + _iota((n16, Q_BLOCK), 1)
    valid_c = (_iota((n16, Q_BLOCK), 0) * CMP_STRIDE + (CMP_BLOCK - 1)) <= tq_q
    s_c = s_c + per_head(jnp.where(valid_c, 0.0, NEG))
    e_c = jnp.exp2(s_c - _col_max(s_c))
    any_c = per_head(_ind(start + _iota((1, Q_BLOCK), 1) >= CMP_BLOCK - 1))
    p_c = e_c * (any_c / jnp.sum(e_c, axis=0, keepdims=True))
    o_c = _dot(vc_ref[0, 0], p_c.astype(BF16))

    p_sum = p_c[:, 0:Q_BLOCK]
    for g in range(1, N_GROUP):
        p_sum = p_sum + p_c[:, g * Q_BLOCK:(g + 1) * Q_BLOCK]
    sb, cn = _iota((n_slc, n16), 0), _iota((n_slc, n16), 1)
    c_lo, s_lo = cn * CMP_STRIDE, sb * SLC_BLOCK
    ov = jnp.maximum(jnp.minimum(c_lo + CMP_BLOCK, s_lo + SLC_BLOCK) - jnp.maximum(c_lo, s_lo), 0)
    ov = jnp.where(cn < n16 - 1, ov, 0)
    overlap_t = (ov.astype(F32) * (1.0 / CMP_BLOCK)).astype(BF16)
    imp = _dot01_l(overlap_t, p_sum, 2)
    tq_i = start + _iota(imp.shape, 1)
    blk = _iota(imp.shape, 0)
    cur = tq_i >> 6
    forced = (blk == 0) | (blk == cur) | (blk == cur - 1)
    imp = jnp.where(forced, -3e38, imp)
    imp = jnp.where(blk * SLC_BLOCK <= tq_i, imp, NEG)
    blk_f = blk.astype(F32)
    sel_t = _ind(forced)
    for _ in range(N_SELECTED - FORCED_BLOCKS):
        top = _col_max(imp)
        first = jnp.min(jnp.where(imp == top, blk_f, float(n_slc)), axis=0, keepdims=True)
        hit = blk_f == first
        sel_t = jnp.where(hit, 1.0, sel_t)
        imp = jnp.where(hit, -3e38, imp)

    w0 = pl.multiple_of(jnp.maximum(start - WINDOW, 0), Q_BLOCK)
    s_w = _dot(kw_ref[0, 0, pl.ds(w0, WIN_SPAN), :], q_t)
    dist = start + _iota((WIN_SPAN, Q_BLOCK), 1) - (w0 + _iota((WIN_SPAN, Q_BLOCK), 0))
    s_w = s_w + per_head(jnp.where((dist >= 0) & (dist < WINDOW), 0.0, NEG))
    p_w = probs(s_w, _col_max(s_w))
    o_w = finish(_dot(vw_ref[0, 0, :, pl.ds(w0, WIN_SPAN)], p_w))

    own = pl.multiple_of(start, Q_BLOCK)
    s_d = _dot(ks_ref[0, 0, pl.ds(own, Q_BLOCK), LANES:LANES + HEAD_DIM], q_t)
    s_d = s_d + per_head(jnp.where(_iota((Q_BLOCK, Q_BLOCK), 0) <= _iota((Q_BLOCK, Q_BLOCK), 1), 0.0, NEG))
    m_d = _col_max(s_d)
    acc_d = _dot(vs_ref[0, 0, :, pl.ds(own, Q_BLOCK)], probs(s_d, m_d))

    bias_t = (jnp.where(blk < 2 * qi, sel_t, 0.0) - 1.0) * BIG
    if n_slc < LANES:
        bias_t = jnp.concatenate([bias_t, jnp.zeros((LANES - n_slc, Q_BLOCK), F32)], axis=0)
    bias_t = bias_t.astype(BF16)
    q_aug = jnp.concatenate([jnp.concatenate([bias_t] * N_GROUP, axis=1), q_t], axis=0)

    n_tiles = (start + SLC_TILE - 1) // SLC_TILE

    def scores(kt):
        k0 = pl.multiple_of(kt * SLC_TILE, SLC_TILE)
        return _dot(ks_ref[0, 0, pl.ds(k0, SLC_TILE), :], q_aug)

    def weighted(p, kt):
        k0 = pl.multiple_of(kt * SLC_TILE, SLC_TILE)
        return _dot(vs_ref[0, 0, :, pl.ds(k0, SLC_TILE)], p)

    top_tile = ks_ref.shape[2] // SLC_TILE - 1

    def half_step(kt, s_in, s_out, p_in, p_out, a_prev, m, acc):
        s_out[...] = scores(jnp.minimum(kt + 1, top_tile))
        acc = a_prev * acc + weighted(p_in[...], jnp.maximum(kt - 1, 0))
        s = s_in[...]
        m_new = jnp.maximum(m, _col_max(s))
        p_out[...] = probs(s, m_new)
        return jnp.exp2(m - m_new), m_new, acc

    def slc_pair(j, carry):
        carry = half_step(2 * j, sa_scr, sb_scr, pb_scr, pa_scr, *carry)
        return half_step(2 * j + 1, sb_scr, sa_scr, pa_scr, pb_scr, *carry)

    sa_scr[...] = scores(0)
    pb_scr[...] = jnp.zeros((SLC_TILE, cols), BF16)
    n_pairs = (n_tiles + 1) // 2
    a_last, _, acc_s = lax.fori_loop(0, n_pairs, slc_pair, (jnp.ones((1, cols), F32), m_d, acc_d))
    o_s = finish(a_last * acc_s + weighted(pb_scr[...], jnp.maximum(2 * n_pairs - 1, 0)))

    def gate_row(branch):
        rows_ = []
        for g in range(N_GROUP):
            c0 = GT_N + branch * N_HEADS + g
            c1 = c0 + N_GROUP
            rows_.append(jnp.where(hkv == 0, gates_t[c0:c0 + 1], gates_t[c1:c1 + 1]))
        return jnp.concatenate(rows_, axis=1)

    y_t = gate_row(0) * o_c + gate_row(1) * o_s + gate_row(2) * o_w
    y = jnp.concatenate([y_t[:, g * Q_BLOCK:(g + 1) * Q_BLOCK] for g in range(N_GROUP)], axis=0)
    o_ref[0] = y.T


def _nsa_attention_t(q_t, kc, vc_t, ks, vs_t, kw, vw_t, gt):
    b, _, t, _ = ks.shape
    n16 = kc.shape[2]
    n_slc = t // SLC_BLOCK
    cols = N_GROUP * Q_BLOCK
    assert (t // SLC_TILE) % 2 == 0
    assert n_slc >= N_SELECTED
    whole =lambda r, w: pl.BlockSpec((1, 1, r, w), lambda i, h, j: (i, h, 0, 0))
    kern = functools.partial(_nsa_attn_t_kernel, n_slc=n_slc)
    return pl.pallas_call(
        kern,
        grid=(b, N_KV_HEADS, t // Q_BLOCK),
        in_specs=[pl.BlockSpec((1, 1, HEAD_DIM, cols), lambda i, h, j: (i, h, 0, j)),
                  whole(n16, LANES), whole(HEAD_DIM, n16),
                  whole(t, KS_AUG), whole(V_ROWS, t), whole(t, HEAD_DIM), whole(V_ROWS, t),
                  pl.BlockSpec((1, Q_BLOCK, LANES), lambda i, h, j: (i, j, 0))],
        out_specs=pl.BlockSpec((1, Q_BLOCK, N_GROUP * HEAD_DIM), lambda i, h, j: (i, j, h)),
        out_shape=jax.ShapeDtypeStruct((b, t, N_WIDTH), F32),
        scratch_shapes=[pltpu.VMEM((SLC_TILE, cols), F32), pltpu.VMEM((SLC_TILE, cols), F32),
                        pltpu.VMEM((SLC_TILE, cols), BF16), pltpu.VMEM((SLC_TILE, cols), BF16)],
        compiler_params=_params("parallel", "parallel", "arbitrary"),
        name="nsa_attention",
    )(q_t, kc, vc_t, ks, vs_t, kw, vw_t, gt)


def _nsa(pq, pkv, gt, qk_norm, cmp_pos, cmp_w1, cmp_w2):
    q_t, ks, vs_t, kw, vw_t, x_cmp = _nsa_prep_t(pq, pkv, qk_norm)
    kc, vc_t = _nsa_compress_t(x_cmp, cmp_pos, cmp_w1, cmp_w2, qk_norm[1])
    return _nsa_attention_t(q_t, kc, vc_t, ks, vs_t, kw, vw_t, gt)


def _mixer_heads(x, mod, nw, w_in, m_gate_b, m_norm_w, g_conv_w, g_a_log, g_dt_bias, g_norm_w,
                 n_qk_norm, n_cmp_pos, n_cmp_w1, n_cmp_w2):
    pm, pg, pq, pkv, gt = _in_proj(x, mod, nw, _permute_w_in(w_in))
    y_m = _mlstm(pm, gt, m_gate_b, m_norm_w)
    y_g = _gdn(pg, gt, g_conv_w, g_a_log, g_dt_bias, g_norm_w)
    y_n = _nsa(pq, pkv, gt, n_qk_norm, n_cmp_pos, n_cmp_w1, n_cmp_w2)
    return y_m, y_g, y_n


def kernel(x, c, ada_w, ada_b, norm_w, ffn_w_up, ffn_w_down, w_in, w_out, mlstm_gate_b, mlstm_norm_w,
           gdn_conv_w, gdn_a_log, gdn_dt_bias, gdn_norm_w, nsa_qk_norm, nsa_cmp_pos, nsa_cmp_w1, nsa_cmp_w2):
    depth = ada_w.shape[0]
    b = x.shape[0]
    mods = _modulation(c, ada_w, ada_b).reshape(depth, b, 9, D_MODEL)
    for l in range(depth):
        mod = mods[l]
        x = _ffn(x, mod, norm_w[l, 0], ffn_w_up[l, 0].astype(BF16), ffn_w_down[l, 0].astype(BF16), 0)
        ys = _mixer_heads(x, mod, norm_w[l, 1], w_in[l], mlstm_gate_b[l], mlstm_norm_w[l],
                          gdn_conv_w[l], gdn_a_log[l], gdn_dt_bias[l], gdn_norm_w[l],
                          nsa_qk_norm[l], nsa_cmp_pos[l], nsa_cmp_w1[l], nsa_cmp_w2[l])
        x = _ffn(x, mod, norm_w[l, 2], ffn_w_up[l, 1].astype(BF16), ffn_w_down[l, 1].astype(BF16), 6,
                 mixer=(*ys, w_out[l]))
    return x
```

```python
import functools

import jax
import jax.numpy as jnp
from jax import lax
from jax.experimental import pallas as pl
from jax.experimental.pallas import tpu as pltpu

F32 = jnp.float32
BF16 = jnp.bfloat16

D_MODEL = 1024
HEAD_DIM = 64
D_FF = 2816
EPS = 1e-6
NEG = -1e30
BIG = 1e30

M_HEADS = 4
G_HEADS = 4
N_HEADS = 8
N_KV_HEADS = 2
N_GROUP = N_HEADS // N_KV_HEADS
CHUNK = 64
GDN_PAR = 4
MLSTM_PAR = 4
CONV_K = 4
CMP_BLOCK = 32
CMP_STRIDE = 16
CMP_HIDDEN = 256
SLC_BLOCK = 64
N_SELECTED = 16
WINDOW = 512
Q_BLOCK = 128
ROPE_THETA = 10000.0

M_WIDTH = M_HEADS * HEAD_DIM
G_WIDTH = G_HEADS * HEAD_DIM
N_WIDTH = N_HEADS * HEAD_DIM
KV_WIDTH = N_KV_HEADS * HEAD_DIM
M_IN = 4 * M_WIDTH + 2 * M_HEADS
G_IN = 4 * G_WIDTH + 2 * G_HEADS
N_IN = N_WIDTH + 6 * KV_WIDTH + 3 * N_HEADS

LANES = 128
FF_TILE = 256
VMEM_LIMIT = 56 * 1024 * 1024

GT_M = 0
GT_G = 8
GT_N = 16


def _params(*sem):
    return pltpu.CompilerParams(dimension_semantics=sem, vmem_limit_bytes=VMEM_LIMIT)


def _dot(a, b):
    return jnp.dot(a, b, preferred_element_type=F32)


def _dot_nt(a, b):
    return lax.dot_general(a, b, (((1,), (1,)), ((), ())), preferred_element_type=F32)


def _dot_tn(a, b):
    return lax.dot_general(a, b, (((0,), (0,)), ((), ())), preferred_element_type=F32)


def _split_bf16(x, n):
    parts, r = [], x
    for _ in range(n):
        p = r.astype(BF16)
        parts.append(p)
        r = r - p.astype(F32)
    return parts


def _dot01_l(m01, x, n=3):
    acc = None
    for p in _split_bf16(x, n):
        t = _dot(m01, p)
        acc = t if acc is None else acc + t
    return acc


def _dot01_r(x, m01, n=3):
    acc = None
    for p in _split_bf16(x, n):
        t = _dot(p, m01)
        acc = t if acc is None else acc + t
    return acc


def _iota(shape, dim):
    return lax.broadcasted_iota(jnp.int32, shape, dim)


def _ind(cond, dtype=F32):
    return jnp.where(cond, 1.0, 0.0).astype(dtype)


def _softplus(x):
    return jnp.maximum(x, 0.0) + jnp.log1p(jnp.exp(-jnp.abs(x)))


def _group_ones(width):
    r, c = _iota((width, width), 0), _iota((width, width), 1)
    return _ind((r >> 6) == (c >> 6), BF16)


def _rope_lanes(x, cosf, sinf):
    lane = _iota(x.shape, 1)
    partner = jnp.where((lane & 63) < 32, pltpu.roll(x, 96, 1), pltpu.roll(x, 32, 1))
    return x * cosf + partner * sinf


def _mod_kernel(ct_ref, w_ref, b_ref, o_ref):
    ct = ct_ref[...]
    act = ct * jax.nn.sigmoid(ct)
    w = w_ref[0]
    rows = [jnp.sum(w * act[:, b:b + 1], axis=0, keepdims=True) for b in range(ct.shape[1])]
    o_ref[0] = jnp.concatenate(rows, axis=0) + b_ref[0]


def _modulation(c, ada_w, ada_b):
    depth, d, n = ada_w.shape
    b = c.shape[0]
    tn = 1024
    return pl.pallas_call(
        _mod_kernel,
        grid=(depth, n // tn),
        in_specs=[pl.BlockSpec((d, b), lambda l, j: (0, 0)),
                  pl.BlockSpec((1, d, tn), lambda l, j: (l, 0, j)),
                  pl.BlockSpec((1, 1, tn), lambda l, j: (l, 0, j))],
        out_specs=pl.BlockSpec((1, b, tn), lambda l, j: (l, 0, j)),
        out_shape=jax.ShapeDtypeStruct((depth, b, n), F32),
        compiler_params=_params("parallel", "parallel"),
        name="adaln_mod",
    )(c.T, ada_w, ada_b.reshape(depth, 1, n))


def _adaln(x, nw, shift, scale):
    y = x * lax.rsqrt(jnp.mean(x * x, axis=-1, keepdims=True) + EPS) * nw
    return y * (1.0 + scale) + shift


def _ffn_kernel(x_ref, mod_ref, nw_ref, wup_ref, wdn_ref, *rest, row0, res_w):
    x = x_ref[0]
    if len(rest) > 1:
        ym_ref, yg_ref, yn_ref, wout_ref, o_ref = rest
        y = jnp.concatenate([ym_ref[0], yg_ref[0], yn_ref[0]], axis=-1).astype(BF16)
        x = x + mod_ref[0, 5:6, :] * _dot(y, wout_ref[...])
    else:
        o_ref, = rest
    shift = mod_ref[0, row0:row0 + 1, :]
    scale = mod_ref[0, row0 + 1:row0 + 2, :]
    gate = mod_ref[0, row0 + 2:row0 + 3, :]
    h = _adaln(x, nw_ref[...], shift, scale).astype(BF16)
    acc = jnp.zeros(x.shape, F32)
    for c in range(D_FF // FF_TILE):
        lo = c * FF_TILE
        g = _dot(h, wup_ref[:, lo:lo + FF_TILE])
        u = _dot(h, wup_ref[:, D_FF + lo:D_FF + lo + FF_TILE])
        a = (jax.nn.silu(g) * u).astype(BF16)
        acc = acc + _dot(a, wdn_ref[lo:lo + FF_TILE, :])
    o_ref[0] = x + (res_w * gate) * acc


def _ffn(x, mod, nw, w_up, w_down, row0, mixer=None):
    b, t, d = x.shape
    tm = min(512, t)
    kern = functools.partial(_ffn_kernel, row0=row0, res_w=0.5)
    row = lambda w: pl.BlockSpec((1, tm, w), lambda i, j: (i, j, 0))
    const = lambda r, w: pl.BlockSpec((r, w), lambda i, j: (0, 0), pipeline_mode=pl.Buffered(1))
    in_specs = [row(d), pl.BlockSpec((1, 9, d), lambda i, j: (i, 0, 0)), pl.BlockSpec((1, d), lambda i, j: (0, 0)),
                const(d, 2 * D_FF), const(D_FF, d)]
    args = [x, mod, nw.reshape(1, d), w_up, w_down]
    if mixer is not None:
        y_m, y_g, y_n, w_out = mixer
        in_specs += [row(M_WIDTH), row(G_WIDTH), row(N_WIDTH), const(d, d)]
        args += [y_m, y_g, y_n, w_out.astype(BF16)]
    return pl.pallas_call(
        kern,
        grid=(b, t // tm),
        in_specs=in_specs,
        out_specs=row(d),
        out_shape=jax.ShapeDtypeStruct(x.shape, F32),
        compiler_params=_params("parallel", "parallel"),
        name="ffn",
    )(*args)


def _inproj_kernel(x_ref, mod_ref, nw_ref, w_ref, pm_ref, pg_ref, pq_ref, pkv_ref, gt_ref):
    x = x_ref[0]
    h = _adaln(x, nw_ref[...], mod_ref[0, 3:4, :], mod_ref[0, 4:5, :]).astype(BF16)
    off = 0
    for ref in (pm_ref, pg_ref, pq_ref, pkv_ref, gt_ref):
        wd = ref.shape[-1]
        ref[0] = _dot(h, w_ref[:, off:off + wd])
        off += wd


IN_PERM_WIDTH = 4 * M_WIDTH + 4 * G_WIDTH + N_WIDTH + 6 * KV_WIDTH + LANES


def _permute_w_in(w_in):
    g0, n0 = M_IN, M_IN + G_IN
    gates = jnp.concatenate([w_in[:, 4 * M_WIDTH:M_IN], w_in[:, g0 + 4 * G_WIDTH:g0 + G_IN],
                             w_in[:, n0 + N_WIDTH + 6 * KV_WIDTH:n0 + N_IN]], axis=1)
    gates = jnp.pad(gates, ((0, 0), (0, LANES - gates.shape[1])))
    return jnp.concatenate([w_in[:, :4 * M_WIDTH], w_in[:, g0:g0 + 4 * G_WIDTH],
                            w_in[:, n0:n0 + N_WIDTH + 6 * KV_WIDTH], gates], axis=1).astype(BF16)


def _in_proj(x, mod, nw, w_perm):
    b, t, d = x.shape
    tm = min(512, t)
    widths = (4 * M_WIDTH, 4 * G_WIDTH, N_WIDTH, 6 * KV_WIDTH, LANES)
    return pl.pallas_call(
        _inproj_kernel,
        grid=(b, t // tm),
        in_specs=[pl.BlockSpec((1, tm, d), lambda i, j: (i, j, 0)),
                  pl.BlockSpec((1, 9, d), lambda i, j: (i, 0, 0)),
                  pl.BlockSpec((1, d), lambda i, j: (0, 0)),
                  pl.BlockSpec((d, IN_PERM_WIDTH), lambda i, j: (0, 0), pipeline_mode=pl.Buffered(1))],
        out_specs=[pl.BlockSpec((1, tm, w), lambda i, j: (i, j, 0)) for w in widths],
        out_shape=[jax.ShapeDtypeStruct((b, t, w), F32) for w in widths],
        compiler_params=_params("parallel", "parallel"),
        name="mixer_in_proj",
    )(x, mod, nw.reshape(1, d), w_perm)


def _chunk_masks():
    r, c = _iota((CHUNK, CHUNK), 0), _iota((CHUNK, CHUNK), 1)
    return r, c


def _mlstm_kernel(pm_ref, gt_ref, gb_ref, nw_ref, o_ref, c_scr, m_scr, *, n_chunks):
    @pl.when(pl.program_id(1) == 0)
    def _():
        c_scr[...] = jnp.zeros(c_scr.shape, F32)
        m_scr[...] = jnp.zeros(m_scr.shape, F32)

    r, c = _chunk_masks()
    causal = r >= c
    low = _ind(causal, BF16)
    low_ones = jnp.concatenate([low, jnp.ones((CHUNK, CHUNK), BF16)], axis=1)
    strict_up = _ind(r > c)
    eye = _ind(r == c)
    nw = nw_ref[...]
    gb = gb_ref[...]

    ones_v = jnp.ones((CHUNK, HEAD_DIM), BF16)
    heads = range(M_HEADS)

    def chunk_group(cp, carry):
        rows = [pl.ds(pl.multiple_of((cp * MLSTM_PAR + j) * CHUNK, CHUNK), CHUNK) for j in range(MLSTM_PAR)]
        items = [(j, h) for j in range(MLSTM_PAR) for h in heads]
        every = lambda f, *cols: [f(*args) for args in zip(*cols)]
        g_blk = [gt_ref[0, rw, :] + gb for rw in rows]
        lf_blk = [-_softplus(-g) for g in g_blk]
        cum_blk = [_dot01_l(low, x) for x in lf_blk]
        pick = lambda blk, lane: [blk[j][:, lane + h:lane + h + 1] for j, h in items]
        li_c, lf_c, cum_c = pick(g_blk, GT_M), pick(lf_blk, GT_M + M_HEADS), pick(cum_blk, GT_M + M_HEADS)
        head = lambda k: [pm_ref[0, rows[j], k * M_WIDTH + h * HEAD_DIM:k * M_WIDTH + (h + 1) * HEAD_DIM]
                          for j, h in items]
        q = every(lambda x: (x * (HEAD_DIM ** -0.5)).astype(BF16), head(0))
        k = head(1)
        v_aug = every(lambda x: jnp.concatenate([x.astype(BF16), ones_v], axis=1), head(2))
        dmat = every(lambda f, i: _dot01_l(low_ones, jnp.concatenate([f * strict_up, i * eye], axis=0)), lf_c, li_c)
        dmat = every(lambda x: jnp.where(causal, x, NEG), dmat)
        d_max = every(lambda x: jnp.max(x, axis=-1, keepdims=True), dmat)
        qk = every(lambda a, b: _dot_nt(a, b.astype(BF16)), q, k)
        g_tot = every(lambda x: x[CHUNK - 1:CHUNK, :], cum_c)
        a = every(lambda gt, cm, li: gt - cm + li, g_tot, cum_c, li_c)
        m_loc = every(lambda x: jnp.max(x, axis=0, keepdims=True), a)
        kw = every(lambda kk, x, ml: (kk * jnp.exp(x - ml)).astype(BF16), k, a, m_loc)
        c_loc = every(_dot_tn, kw, v_aug)
        c_prev = [c_scr[h] for h in heads]
        m_prev = [m_scr[h] for h in heads]
        for j in range(MLSTM_PAR):
            sel = [j * M_HEADS + h for h in heads]
            m_inter = [cum_c[i] + m_prev[h] for h, i in enumerate(sel)]
            m_t = [jnp.maximum(m_inter[h], d_max[i]) for h, i in enumerate(sel)]
            s = [(qk[i] * jnp.exp(dmat[i] - m_t[h])).astype(BF16) for h, i in enumerate(sel)]
            inter = [jnp.exp(m_inter[h] - m_t[h]) for h in heads]
            nd = [_dot(s[h], v_aug[i]) + inter[h] * _dot(q[i], c_prev[h].astype(BF16)) for h, i in enumerate(sel)]
            den = [jnp.maximum(jnp.abs(nd[h]), jnp.exp(-m_t[h])) for h in heads]
            hh = [(nd[h] / pltpu.roll(den[h], HEAD_DIM, 1))[:, :HEAD_DIM] for h in heads]
            m_new = [jnp.maximum(g_tot[i] + m_prev[h], m_loc[i]) for h, i in enumerate(sel)]
            s_old = [jnp.exp(g_tot[i] + m_prev[h] - m_new[h]) for h, i in enumerate(sel)]
            s_new = [jnp.exp(m_loc[i] - m_new[h]) for h, i in enumerate(sel)]
            for h in heads:
                o_ref[0, rows[j], h * HEAD_DIM:(h + 1) * HEAD_DIM] = hh[h]
            c_prev = [s_old[h] * c_prev[h] + s_new[h] * c_loc[i] for h, i in enumerate(sel)]
            m_prev = m_new
        for h in heads:
            c_scr[h] = c_prev[h]
            m_scr[h] = m_prev[h]
        return carry

    lax.fori_loop(0, n_chunks // MLSTM_PAR, chunk_group, 0)

    hh = o_ref[0]
    ms = _dot01_r(hh * hh, _group_ones(M_WIDTH), 2) * (1.0 / HEAD_DIM)
    o_ref[0] = hh * lax.rsqrt(ms + EPS) * nw * jax.nn.sigmoid(pm_ref[0, :, 3 * M_WIDTH:4 * M_WIDTH])


def _mlstm(pm, gt, gate_b, norm_w):
    b, t, _ = pm.shape
    tc = min(512, t)
    gb = jnp.zeros((1, LANES), F32).at[0, GT_M:GT_M + 2 * M_HEADS].set(gate_b.reshape(-1))
    kern = functools.partial(_mlstm_kernel, n_chunks=tc // CHUNK)
    return pl.pallas_call(
        kern,
        grid=(b, t // tc),
        in_specs=[pl.BlockSpec((1, tc, 4 * M_WIDTH), lambda i, j: (i, j, 0)),
                  pl.BlockSpec((1, tc, LANES), lambda i, j: (i, j, 0)),
                  pl.BlockSpec((1, LANES), lambda i, j: (0, 0)),
                  pl.BlockSpec((1, M_WIDTH), lambda i, j: (0, 0))],
        out_specs=pl.BlockSpec((1, tc, M_WIDTH), lambda i, j: (i, j, 0)),
        out_shape=jax.ShapeDtypeStruct((b, t, M_WIDTH), F32),
        scratch_shapes=[pltpu.VMEM((M_HEADS, HEAD_DIM, 2 * HEAD_DIM), F32),
                        pltpu.VMEM((M_HEADS, 1, 1), F32)],
        compiler_params=_params("parallel", "arbitrary"),
        name="mlstm",
    )(pm, gt, gb, jnp.tile(norm_w.reshape(1, HEAD_DIM), (1, M_HEADS)))


def _gdn_kernel(pg_ref, gt_ref, cw_ref, hp_ref, nw_ref, o_ref, xbuf, qkv_scr, la_scr, s_scr,
                lin_scr, add_scr, g_scr, *, tc):
    first = pl.program_id(1) == 0

    @pl.when(first)
    def _():
        s_scr[...] = jnp.zeros(s_scr.shape, F32)
        xbuf[0:8, :] = jnp.zeros((8, 3 * G_WIDTH), F32)

    @pl.when(jnp.logical_not(first))
    def _():
        xbuf[0:8, :] = xbuf[tc:tc + 8, :]

    xbuf[8:8 + tc, :] = pg_ref[0, :, 0:3 * G_WIDTH]
    acc = None
    for kk in range(CONV_K):
        term = xbuf[8 - (CONV_K - 1) + kk:8 - (CONV_K - 1) + kk + tc, :] * cw_ref[kk:kk + 1, :]
        acc = term if acc is None else acc + term
    act = acc * jax.nn.sigmoid(acc)
    gones = _group_ones(G_WIDTH)
    for part in range(2):
        xx = act[:, part * G_WIDTH:(part + 1) * G_WIDTH]
        ssq = _dot01_r(xx * xx, gones, 2)
        xx = xx * lax.rsqrt(ssq + EPS)
        if part == 0:
            xx = xx * (HEAD_DIM ** -0.5)
        qkv_scr[:, part * G_WIDTH:(part + 1) * G_WIDTH] = xx
    qkv_scr[:, 2 * G_WIDTH:3 * G_WIDTH] = act[:, 2 * G_WIDTH:3 * G_WIDTH]
    gate = gt_ref[0]
    neg_rate = -jnp.exp(hp_ref[0:1, :])
    la_scr[...] = neg_rate * _softplus(gate + hp_ref[1:2, :])

    r, c = _chunk_masks()
    low = _ind(r >= c, BF16)
    strict_low = r > c
    incl_low = r >= c
    strict_up = _ind(r > c)
    eye = _ind(r == c)
    base_mask = _ind(((r >> 2) == (c >> 2)) & (r > c))
    merge_masks = [_ind((((r >> s) & 1) == 1) & ((c >> s) == (r >> s) - 1)) for s in (2, 3, 4, 5)]
    nw = nw_ref[...]

    def prepare(cp, carry):
        items = [(cp * GDN_PAR + j, h) for j in range(GDN_PAR) for h in range(G_HEADS)]
        every = lambda f, *cols: [f(*args) for args in zip(*cols)]
        rows = [pl.ds(pl.multiple_of((cp * GDN_PAR + j) * CHUNK, CHUNK), CHUNK) for j in range(GDN_PAR)]
        la_blk = [la_scr[rw, :] for rw in rows]
        gam_blk = [_dot01_l(low, x) for x in la_blk]
        beta_blk = [jax.nn.sigmoid(gt_ref[0, rw, :]) for rw in rows]
        pick = lambda blk, lane: [blk[i // G_HEADS][:, lane + h:lane + h + 1] for i, (_, h) in enumerate(items)]
        la_c, gam_c, beta_c = pick(la_blk, GT_G), pick(gam_blk, GT_G), pick(beta_blk, GT_G + G_HEADS)
        head = lambda k: [qkv_scr[rows[i // G_HEADS], k * G_WIDTH + h * HEAD_DIM:k * G_WIDTH + (h + 1) * HEAD_DIM]
                          for i, (_, h) in enumerate(items)]
        q, k, v = head(0), head(1), head(2)
        diff = every(lambda x: _dot01_l(low, x * strict_up), la_c)
        dec_strict = every(lambda x: jnp.exp(jnp.where(strict_low, x, NEG)), diff)
        dec_incl = every(lambda x: jnp.exp(jnp.where(incl_low, x, NEG)), diff)
        kb = every(lambda x: x.astype(BF16), k)
        qk_kk = every(lambda a, b: _dot_nt(jnp.concatenate([a.astype(BF16), b], axis=0), b), q, kb)
        amat = every(lambda b, x, dcy: b * x[CHUNK:] * dcy, beta_c, qk_kk, dec_strict)
        n0 = every(lambda a: (-(a * base_mask)).astype(BF16), amat)
        n0sq = every(lambda n: _dot(n, n).astype(BF16), n0)
        inv = every(lambda n: eye + n.astype(F32), n0)
        inv = every(lambda t, n2: t + _dot(t.astype(BF16), n2), inv, n0sq)
        for mask in merge_masks:
            invb = every(lambda t: t.astype(BF16), inv)
            half = every(lambda tb, a: _dot(tb, (a * mask).astype(BF16)).astype(BF16), invb, amat)
            inv = every(lambda t, hf, tb: t - _dot(hf, tb), inv, half, invb)
        e_gam = every(jnp.exp, gam_c)
        rhs = every(lambda b, e, vv, kk: jnp.concatenate([b * vv, (b * e) * kk], axis=1).astype(BF16),
                    beta_c, e_gam, v, k)
        sol = every(lambda t, x: _dot(t.astype(BF16), x).astype(BF16), inv, rhs)
        gam_last = every(lambda g: g[CHUNK - 1:CHUNK, :], gam_c)
        k_d = every(lambda kk, gl, g: (kk * jnp.exp(gl - g)).astype(BF16), k, gam_last, gam_c)
        p_sol = every(lambda x, dcy, sl: _dot((x[:CHUNK] * dcy).astype(BF16), sl), qk_kk, dec_incl, sol)
        kd_sol = every(_dot_tn, k_d, sol)
        for i, (ci, h) in enumerate(items):
            lin_scr[ci, h] = jnp.concatenate([kd_sol[i][:, HEAD_DIM:], q[i] * e_gam[i] - p_sol[i][:, HEAD_DIM:]],
                                             axis=0).astype(BF16)
            add_scr[ci, h] = jnp.concatenate([kd_sol[i][:, :HEAD_DIM], p_sol[i][:, :HEAD_DIM]], axis=0)
            g_scr[ci, h] = jnp.broadcast_to(jnp.exp(gam_last[i]), (1, HEAD_DIM))
        return carry

    lax.fori_loop(0, tc // (CHUNK * GDN_PAR), prepare, 0)

    heads = range(G_HEADS)

    def advance(ci, s_prev):
        r0 = pl.multiple_of(ci * CHUNK, CHUNK)
        prod = [_dot(lin_scr[ci, h], s_prev[h].astype(BF16)) for h in heads]
        add = [add_scr[ci, h] for h in heads]
        for h in heads:
            o_ref[0, pl.ds(r0, CHUNK), h * HEAD_DIM:(h + 1) * HEAD_DIM] = prod[h][CHUNK:] + add[h][CHUNK:]
        return tuple(g_scr[ci, h] * s_prev[h] - prod[h][:CHUNK] + add[h][:CHUNK] for h in heads)

    s_last = lax.fori_loop(0, tc // CHUNK, advance, tuple(s_scr[h] for h in heads))
    for h in heads:
        s_scr[h] = s_last[h]

    o = o_ref[0]
    ms = _dot01_r(o * o, gones, 2) * (1.0 / HEAD_DIM)
    z = pg_ref[0, :, 3 * G_WIDTH:4 * G_WIDTH]
    o_ref[0] = o * lax.rsqrt(ms + EPS) * nw * (z * jax.nn.sigmoid(z))


def _gdn(pg, gt, conv_w, a_log, dt_bias, norm_w):
    b, t, _ = pg.shape
    tc = min(512, t)
    hp = jnp.zeros((2, LANES), F32)
    hp = hp.at[0, GT_G:GT_G + G_HEADS].set(a_log).at[1, GT_G:GT_G + G_HEADS].set(dt_bias)
    kern = functools.partial(_gdn_kernel, tc=tc)
    return pl.pallas_call(
        kern,
        grid=(b, t // tc),
        in_specs=[pl.BlockSpec((1, tc, 4 * G_WIDTH), lambda i, j: (i, j, 0)),
                  pl.BlockSpec((1, tc, LANES), lambda i, j: (i, j, 0)),
                  pl.BlockSpec((CONV_K, 3 * G_WIDTH), lambda i, j: (0, 0)),
                  pl.BlockSpec((2, LANES), lambda i, j: (0, 0)),
                  pl.BlockSpec((1, G_WIDTH), lambda i, j: (0, 0))],
        out_specs=pl.BlockSpec((1, tc, G_WIDTH), lambda i, j: (i, j, 0)),
        out_shape=jax.ShapeDtypeStruct((b, t, G_WIDTH), F32),
        scratch_shapes=[pltpu.VMEM((tc + 8, 3 * G_WIDTH), F32),
                        pltpu.VMEM((tc, 3 * G_WIDTH), F32),
                        pltpu.VMEM((tc, LANES), F32),
                        pltpu.VMEM((G_HEADS, HEAD_DIM, HEAD_DIM), F32),
                        pltpu.VMEM((tc // CHUNK, G_HEADS, 2 * CHUNK, HEAD_DIM), BF16),
                        pltpu.VMEM((tc // CHUNK, G_HEADS, 2 * CHUNK, HEAD_DIM), F32),
                        pltpu.VMEM((tc // CHUNK, G_HEADS, 1, HEAD_DIM), F32)],
        compiler_params=_params("parallel", "arbitrary"),
        name="gdn",
    )(pg, gt, conv_w.astype(F32), hp, jnp.tile(norm_w.reshape(1, HEAD_DIM), (1, G_HEADS)))


def _rope_tables(pos):
    half = HEAD_DIM // 2
    inv_freq = jnp.power(ROPE_THETA, -jnp.arange(half, dtype=F32) / half)
    ang = pos.astype(F32)[:, None] * inv_freq[None, :]
    cos, sin = jnp.cos(ang), jnp.sin(ang)
    cosf = jnp.tile(cos, (1, LANES // half))
    sinf = jnp.tile(jnp.concatenate([-sin, sin], axis=1), (1, LANES // HEAD_DIM))
    return cosf, sinf


KS_AUG = LANES + HEAD_DIM


SLC_TILE = 512
WIN_SPAN = Q_BLOCK + WINDOW


V_ROWS = HEAD_DIM + 16
FORCED_BLOCKS = 3
LOG2E = 1.4426950408889634


def _nsa_prep_t_kernel(pq_ref, pkv_ref, cos_ref, sin_ref, nw_ref, q_ref, ks_ref, vs_ref, kw_ref, vw_ref, xc_ref,
                       cmp_scr):
    cosf, sinf = cos_ref[...], sin_ref[...]
    gones = _group_ones(LANES)
    tm = cosf.shape[0]

    for s in range(2):
        cmp_scr[s] = pkv_ref[0, :, s * KV_WIDTH:(s + 1) * KV_WIDTH]
    for p in range(CMP_STRIDE):
        for s in range(2):
            grp = cmp_scr[s, pl.ds(p, tm // CMP_STRIDE, stride=CMP_STRIDE), :]
            for h in range(N_KV_HEADS):
                xc_ref[0, s, h, :, p * HEAD_DIM:(p + 1) * HEAD_DIM] = grp[:, h * HEAD_DIM:(h + 1) * HEAD_DIM]

    def norm_rope(x, w):
        ssq = _dot01_r(x * x, gones, 2)
        y = x * lax.rsqrt(ssq * (1.0 / HEAD_DIM) + EPS) * w
        return _rope_lanes(y, cosf, sinf)

    for s in range(N_WIDTH // LANES):
        qs = norm_rope(pq_ref[0, :, s * LANES:(s + 1) * LANES], nw_ref[0:1, :]) * (HEAD_DIM ** -0.5 * LOG2E)
        qs_t = qs.T.astype(BF16)
        for half in range(2):
            hkv, g = divmod(2 * s + half, N_GROUP)
            for blk in range(tm // Q_BLOCK):
                col = (blk * N_GROUP + g) * Q_BLOCK
                q_ref[0, hkv, :, col:col + Q_BLOCK] = qs_t[half * HEAD_DIM:(half + 1) * HEAD_DIM,
                                                           blk * Q_BLOCK:(blk + 1) * Q_BLOCK]
    kv = lambda i: pkv_ref[0, :, i * KV_WIDTH:(i + 1) * KV_WIDTH]
    tok = pl.program_id(1) * tm + _iota((tm, LANES), 0)
    onehot = _ind((tok >> 6) == _iota((tm, LANES), 1), BF16)
    k_slc = norm_rope(kv(2), nw_ref[2:3, :])
    k_win = norm_rope(kv(4), nw_ref[3:4, :])
    ones = jnp.ones((V_ROWS - HEAD_DIM, tm), BF16)
    for h in range(N_KV_HEADS):
        ks_ref[0, h, :, 0:LANES] = onehot
        ks_ref[0, h, :, LANES:LANES + HEAD_DIM] = k_slc[:, h * HEAD_DIM:(h + 1) * HEAD_DIM].astype(BF16)
        kw_ref[0, h] = k_win[:, h * HEAD_DIM:(h + 1) * HEAD_DIM].astype(BF16)
    for ref, slab in ((vs_ref, kv(3)), (vw_ref, kv(5))):
        v_t = slab.T.astype(BF16)
        for h in range(N_KV_HEADS):
            ref[0, h, 0:HEAD_DIM, :] = v_t[h * HEAD_DIM:(h + 1) * HEAD_DIM]
            ref[0, h, HEAD_DIM:V_ROWS, :] = ones


def _nsa_prep_t(pq, pkv, qk_norm):
    b, t, _ = pq.shape
    assert t // SLC_BLOCK <= LANES
    tm = min(512, t)
    flat = CMP_STRIDE * HEAD_DIM
    cosf, sinf = _rope_tables(jnp.arange(t, dtype=jnp.int32))
    nw = jnp.tile(qk_norm, (1, LANES // HEAD_DIM))
    rows = lambda w: pl.BlockSpec((1, N_KV_HEADS, tm, w), lambda i, j: (i, 0, j, 0))
    cols = lambda r, w: pl.BlockSpec((1, N_KV_HEADS, r, w), lambda i, j: (i, 0, 0, j))
    shp = lambda r, w: jax.ShapeDtypeStruct((b, N_KV_HEADS, r, w), BF16)
    return pl.pallas_call(
        _nsa_prep_t_kernel,
        grid=(b, t // tm),
        in_specs=[pl.BlockSpec((1, tm, N_WIDTH), lambda i, j: (i, j, 0)),
                  pl.BlockSpec((1, tm, 6 * KV_WIDTH), lambda i, j: (i, j, 0)),
                  pl.BlockSpec((tm, LANES), lambda i, j: (j, 0)),
                  pl.BlockSpec((tm, LANES), lambda i, j: (j, 0)),
                  pl.BlockSpec((4, LANES), lambda i, j: (0, 0))],
        out_specs=[cols(HEAD_DIM, N_GROUP * tm), rows(KS_AUG), cols(V_ROWS, tm), rows(HEAD_DIM), cols(V_ROWS, tm),
                   pl.BlockSpec((1, 2, N_KV_HEADS, tm // CMP_STRIDE, flat), lambda i, j: (i, 0, 0, j, 0))],
        out_shape=[shp(HEAD_DIM, N_GROUP * t), shp(t, KS_AUG), shp(V_ROWS, t), shp(t, HEAD_DIM), shp(V_ROWS, t),
                   jax.ShapeDtypeStruct((b, 2, N_KV_HEADS, t // CMP_STRIDE, flat), F32)],
        scratch_shapes=[pltpu.VMEM((2, tm, KV_WIDTH), F32)],
        compiler_params=_params("parallel", "parallel"),
        name="nsa_prep",
    )(pq, pkv, cosf, sinf, nw)


def _compress_t_kernel(x_ref, pos_ref, w1_ref, w2_ref, cos_ref, sin_ref, nw_ref, kc_ref, vc_ref):
    x = x_ref[0, 0, 0]
    half = CMP_STRIDE * HEAD_DIM
    a = _dot((x + pos_ref[0, 0:1, :]).astype(BF16), w1_ref[0, 0:half, :])
    bmat = _dot((x + pos_ref[0, 1:2, :]).astype(BF16), w1_ref[0, half:2 * half, :])
    hid = a + jnp.concatenate([bmat[1:, :], jnp.zeros((1, CMP_HIDDEN), F32)], axis=0)
    hid = hid * jax.nn.sigmoid(hid)
    out = _dot(hid.astype(BF16), w2_ref[0])

    @pl.when(pl.program_id(2) == 0)
    def _():
        ssq = jnp.sum(out * out, axis=-1, keepdims=True)
        normed = out * lax.rsqrt(ssq * (1.0 / HEAD_DIM) + EPS) * nw_ref[...]
        kc_ref[0, 0] = _rope_lanes(normed, cos_ref[...], sin_ref[...]).astype(BF16)

    @pl.when(pl.program_id(2) == 1)
    def _():
        vc_ref[0, 0] = out.T[0:HEAD_DIM].astype(BF16)


def _nsa_compress_t(x, cmp_pos, cmp_w1, cmp_w2, kc_norm):
    b, _, _, n16, flat = x.shape
    pos = cmp_pos.reshape(2, 2, flat)
    w1 = cmp_w1.astype(BF16)
    w2 = jnp.pad(cmp_w2, ((0, 0), (0, 0), (0, LANES - HEAD_DIM))).astype(BF16)
    cosf, sinf = _rope_tables(jnp.arange(n16, dtype=jnp.int32) * CMP_STRIDE + CMP_BLOCK - 1)
    nw = jnp.pad(kc_norm.reshape(1, HEAD_DIM), ((0, 0), (0, LANES - HEAD_DIM)))
    return pl.pallas_call(
        _compress_t_kernel,
        grid=(b, N_KV_HEADS, 2),
        in_specs=[pl.BlockSpec((1, 1, 1, n16, flat), lambda i, h, s: (i, s, h, 0, 0)),
                  pl.BlockSpec((1, 2, flat), lambda i, h, s: (s, 0, 0)),
                  pl.BlockSpec((1, 2 * flat, CMP_HIDDEN), lambda i, h, s: (s, 0, 0)),
                  pl.BlockSpec((1, CMP_HIDDEN, LANES), lambda i, h, s: (s, 0, 0)),
                  pl.BlockSpec((n16, LANES), lambda i, h, s: (0, 0)),
                  pl.BlockSpec((n16, LANES), lambda i, h, s: (0, 0)),
                  pl.BlockSpec((1, LANES), lambda i, h, s: (0, 0))],
        out_specs=[pl.BlockSpec((1, 1, n16, LANES), lambda i, h, s: (i, h, 0, 0)),
                   pl.BlockSpec((1, 1, HEAD_DIM, n16), lambda i, h, s: (i, h, 0, 0))],
        out_shape=[jax.ShapeDtypeStruct((b, N_KV_HEADS, n16, LANES), BF16),
                   jax.ShapeDtypeStruct((b, N_KV_HEADS, HEAD_DIM, n16), BF16)],
        compiler_params=_params("parallel", "parallel", "arbitrary"),
        name="nsa_compress",
    )(x, pos, w1, w2, cosf, sinf, nw)


def _col_max(x):
    return jnp.max(x, axis=0, keepdims=True)


def _nsa_attn_t_kernel(q_ref, kc_ref, vc_ref, ks_ref, vs_ref, kw_ref, vw_ref, gt_ref, o_ref,
                       sa_scr, sb_scr, pa_scr, pb_scr, *, n_slc):
    hkv = pl.program_id(1)
    qi = pl.program_id(2)
    start = qi * Q_BLOCK
    cols = N_GROUP * Q_BLOCK
    q_t = q_ref[0, 0]
    n16 = kc_ref.shape[2]

    def per_head(x):
        return jnp.concatenate([x] * N_GROUP, axis=1)

    def finish(acc):
        return acc[0:HEAD_DIM] * (1.0 / acc[HEAD_DIM:HEAD_DIM + 1])

    def probs(s, m):
        return jnp.exp2(s - m).astype(BF16)

    gates_t = jax.nn.sigmoid(gt_ref[0]).T

    s_c = _dot(kc_ref[0, 0][:, :HEAD_DIM], q_t)
    tq_q = start + _iota((n16, Q_BLOCK), 1)
    valid_c = (_iota((n16, Q_BLOCK), 0) * CMP_STRIDE + (CMP_BLOCK - 1)) <= tq_q
    s_c = s_c + per_head(jnp.where(valid_c, 0.0, NEG))
    e_c = jnp.exp2(s_c - _col_max(s_c))
    any_c = per_head(_ind(start + _iota((1, Q_BLOCK), 1) >= CMP_BLOCK - 1))
    p_c = e_c * (any_c / jnp.sum(e_c, axis=0, keepdims=True))
    o_c = _dot(vc_ref[0, 0], p_c.astype(BF16))

    p_sum = p_c[:, 0:Q_BLOCK]
    for g in range(1, N_GROUP):
        p_sum = p_sum + p_c[:, g * Q_BLOCK:(g + 1) * Q_BLOCK]
    sb, cn = _iota((n_slc, n16), 0), _iota((n_slc, n16), 1)
    c_lo, s_lo = cn * CMP_STRIDE, sb * SLC_BLOCK
    ov = jnp.maximum(jnp.minimum(c_lo + CMP_BLOCK, s_lo + SLC_BLOCK) - jnp.maximum(c_lo, s_lo), 0)
    ov = jnp.where(cn < n16 - 1, ov, 0)
    overlap_t = (ov.astype(F32) * (1.0 / CMP_BLOCK)).astype(BF16)
    imp = _dot01_l(overlap_t, p_sum, 2)
    tq_i = start + _iota(imp.shape, 1)
    blk = _iota(imp.shape, 0)
    cur = tq_i >> 6
    forced = (blk == 0) | (blk == cur) | (blk == cur - 1)
    imp = jnp.where(forced, -3e38, imp)
    imp = jnp.where(blk * SLC_BLOCK <= tq_i, imp, NEG)
    blk_f = blk.astype(F32)
    sel_t = _ind(forced)
    for _ in range(N_SELECTED - FORCED_BLOCKS):
        top = _col_max(imp)
        first = jnp.min(jnp.where(imp == top, blk_f, float(n_slc)), axis=0, keepdims=True)
        hit = blk_f == first
        sel_t = jnp.where(hit, 1.0, sel_t)
        imp = jnp.where(hit, -3e38, imp)

    w0 = pl.multiple_of(jnp.maximum(start - WINDOW, 0), Q_BLOCK)
    s_w = _dot(kw_ref[0, 0, pl.ds(w0, WIN_SPAN), :], q_t)
    dist = start + _iota((WIN_SPAN, Q_BLOCK), 1) - (w0 + _iota((WIN_SPAN, Q_BLOCK), 0))
    s_w = s_w + per_head(jnp.where((dist >= 0) & (dist < WINDOW), 0.0, NEG))
    p_w = probs(s_w, _col_max(s_w))
    o_w = finish(_dot(vw_ref[0, 0, :, pl.ds(w0, WIN_SPAN)], p_w))

    own = pl.multiple_of(start, Q_BLOCK)
    s_d = _dot(ks_ref[0, 0, pl.ds(own, Q_BLOCK), LANES:LANES + HEAD_DIM], q_t)
    s_d = s_d + per_head(jnp.where(_iota((Q_BLOCK, Q_BLOCK), 0) <= _iota((Q_BLOCK, Q_BLOCK), 1), 0.0, NEG))
    m_d = _col_max(s_d)
    acc_d = _dot(vs_ref[0, 0, :, pl.ds(own, Q_BLOCK)], probs(s_d, m_d))

    bias_t = (jnp.where(blk < 2 * qi, sel_t, 0.0) - 1.0) * BIG
    if n_slc < LANES:
        bias_t = jnp.concatenate([bias_t, jnp.zeros((LANES - n_slc, Q_BLOCK), F32)], axis=0)
    bias_t = bias_t.astype(BF16)
    q_aug = jnp.concatenate([jnp.concatenate([bias_t] * N_GROUP, axis=1), q_t], axis=0)

    n_tiles = (start + SLC_TILE - 1) // SLC_TILE

    def scores(kt):
        k0 = pl.multiple_of(kt * SLC_TILE, SLC_TILE)
        return _dot(ks_ref[0, 0, pl.ds(k0, SLC_TILE), :], q_aug)

    def weighted(p, kt):
        k0 = pl.multiple_of(kt * SLC_TILE, SLC_TILE)
        return _dot(vs_ref[0, 0, :, pl.ds(k0, SLC_TILE)], p)

    top_tile = ks_ref.shape[2] // SLC_TILE - 1

    def half_step(kt, s_in, s_out, p_in, p_out, a_prev, m, acc):
        s_out[...] = scores(jnp.minimum(kt + 1, top_tile))
        acc = a_prev * acc + weighted(p_in[...], jnp.maximum(kt - 1, 0))
        s = s_in[...]
        m_new = jnp.maximum(m, _col_max(s))
        p_out[...] = probs(s, m_new)
        return jnp.exp2(m - m_new), m_new, acc

    def slc_pair(j, carry):
        carry = half_step(2 * j, sa_scr, sb_scr, pb_scr, pa_scr, *carry)
        return half_step(2 * j + 1, sb_scr, sa_scr, pa_scr, pb_scr, *carry)

    sa_scr[...] = scores(0)
    pb_scr[...] = jnp.zeros((SLC_TILE, cols), BF16)
    n_pairs = (n_tiles + 1) // 2
    a_last, _, acc_s = lax.fori_loop(0, n_pairs, slc_pair, (jnp.ones((1, cols), F32), m_d, acc_d))
    o_s = finish(a_last * acc_s + weighted(pb_scr[...], jnp.maximum(2 * n_pairs - 1, 0)))

    def gate_row(branch):
        rows_ = []
        for g in range(N_GROUP):
            c0 = GT_N + branch * N_HEADS + g
            c1 = c0 + N_GROUP
            rows_.append(jnp.where(hkv == 0, gates_t[c0:c0 + 1], gates_t[c1:c1 + 1]))
        return jnp.concatenate(rows_, axis=1)

    y_t = gate_row(0) * o_c + gate_row(1) * o_s + gate_row(2) * o_w
    y = jnp.concatenate([y_t[:, g * Q_BLOCK:(g + 1) * Q_BLOCK] for g in range(N_GROUP)], axis=0)
    o_ref[0] = y.T


def _nsa_attention_t(q_t, kc, vc_t, ks, vs_t, kw, vw_t, gt):
    b, _, t, _ = ks.shape
    n16 = kc.shape[2]
    n_slc = t // SLC_BLOCK
    cols = N_GROUP * Q_BLOCK
    assert (t // SLC_TILE) % 2 == 0
    assert n_slc >= N_SELECTED
    whole =lambda r, w: pl.BlockSpec((1, 1, r, w), lambda i, h, j: (i, h, 0, 0))
    kern = functools.partial(_nsa_attn_t_kernel, n_slc=n_slc)
    return pl.pallas_call(
        kern,
        grid=(b, N_KV_HEADS, t // Q_BLOCK),
        in_specs=[pl.BlockSpec((1, 1, HEAD_DIM, cols), lambda i, h, j: (i, h, 0, j)),
                  whole(n16, LANES), whole(HEAD_DIM, n16),
                  whole(t, KS_AUG), whole(V_ROWS, t), whole(t, HEAD_DIM), whole(V_ROWS, t),
                  pl.BlockSpec((1, Q_BLOCK, LANES), lambda i, h, j: (i, j, 0))],
        out_specs=pl.BlockSpec((1, Q_BLOCK, N_GROUP * HEAD_DIM), lambda i, h, j: (i, j, h)),
        out_shape=jax.ShapeDtypeStruct((b, t, N_WIDTH), F32),
        scratch_shapes=[pltpu.VMEM((SLC_TILE, cols), F32), pltpu.VMEM((SLC_TILE, cols), F32),
                        pltpu.VMEM((SLC_TILE, cols), BF16), pltpu.VMEM((SLC_TILE, cols), BF16)],
        compiler_params=_params("parallel", "parallel", "arbitrary"),
        name="nsa_attention",
    )(q_t, kc, vc_t, ks, vs_t, kw, vw_t, gt)


def _nsa(pq, pkv, gt, qk_norm, cmp_pos, cmp_w1, cmp_w2):
    q_t, ks, vs_t, kw, vw_t, x_cmp = _nsa_prep_t(pq, pkv, qk_norm)
    kc, vc_t = _nsa_compress_t(x_cmp, cmp_pos, cmp_w1, cmp_w2, qk_norm[1])
    return _nsa_attention_t(q_t, kc, vc_t, ks, vs_t, kw, vw_t, gt)


def _mixer_heads(x, mod, nw, w_in, m_gate_b, m_norm_w, g_conv_w, g_a_log, g_dt_bias, g_norm_w,
                 n_qk_norm, n_cmp_pos, n_cmp_w1, n_cmp_w2):
    pm, pg, pq, pkv, gt = _in_proj(x, mod, nw, _permute_w_in(w_in))
    y_m = _mlstm(pm, gt, m_gate_b, m_norm_w)
    y_g = _gdn(pg, gt, g_conv_w, g_a_log, g_dt_bias, g_norm_w)
    y_n = _nsa(pq, pkv, gt, n_qk_norm, n_cmp_pos, n_cmp_w1, n_cmp_w2)
    return y_m, y_g, y_n


def kernel(x, c, ada_w, ada_b, norm_w, ffn_w_up, ffn_w_down, w_in, w_out, mlstm_gate_b, mlstm_norm_w,
           gdn_conv_w, gdn_a_log, gdn_dt_bias, gdn_norm_w, nsa_qk_norm, nsa_cmp_pos, nsa_cmp_w1, nsa_cmp_w2):
    depth = ada_w.shape[0]
    b = x.shape[0]
    mods = _modulation(c, ada_w, ada_b).reshape(depth, b, 9, D_MODEL)
    for l in range(depth):
        mod = mods[l]
        x = _ffn(x, mod, norm_w[l, 0], ffn_w_up[l, 0].astype(BF16), ffn_w_down[l, 0].astype(BF16), 0)
        ys = _mixer_heads(x, mod, norm_w[l, 1], w_in[l], mlstm_gate_b[l], mlstm_norm_w[l],
                          gdn_conv_w[l], gdn_a_log[l], gdn_dt_bias[l], gdn_norm_w[l],
                          nsa_qk_norm[l], nsa_cmp_pos[l], nsa_cmp_w1[l], nsa_cmp_w2[l])
        x = _ffn(x, mod, norm_w[l, 2], ffn_w_up[l, 1].astype(BF16), ffn_w_down[l, 1].astype(BF16), 6,
                 mixer=(*ys, w_out[l]))
    return x
```

```python
import functools

import jax
import jax.numpy as jnp
from jax import lax
from jax.experimental import pallas as pl
from jax.experimental.pallas import tpu as pltpu

F32 = jnp.float32
BF16 = jnp.bfloat16

D_MODEL = 1024
HEAD_DIM = 64
D_FF = 2816
EPS = 1e-6
NEG = -1e30
BIG = 1e30

M_HEADS = 4
G_HEADS = 4
N_HEADS = 8
N_KV_HEADS = 2
N_GROUP = N_HEADS // N_KV_HEADS
CHUNK = 64
GDN_PAR = 4
MLSTM_PAR = 4
CONV_K = 4
CMP_BLOCK = 32
CMP_STRIDE = 16
CMP_HIDDEN = 256
SLC_BLOCK = 64
N_SELECTED = 16
WINDOW = 512
Q_BLOCK = 128
ROPE_THETA = 10000.0

M_WIDTH = M_HEADS * HEAD_DIM
G_WIDTH = G_HEADS * HEAD_DIM
N_WIDTH = N_HEADS * HEAD_DIM
KV_WIDTH = N_KV_HEADS * HEAD_DIM
M_IN = 4 * M_WIDTH + 2 * M_HEADS
G_IN = 4 * G_WIDTH + 2 * G_HEADS
N_IN = N_WIDTH + 6 * KV_WIDTH + 3 * N_HEADS

LANES = 128
FF_TILE = 256
VMEM_LIMIT = 56 * 1024 * 1024

GT_M = 0
GT_G = 8
GT_N = 16


def _params(*sem):
    return pltpu.CompilerParams(dimension_semantics=sem, vmem_limit_bytes=VMEM_LIMIT)


def _dot(a, b):
    return jnp.dot(a, b, preferred_element_type=F32)


def _dot_nt(a, b):
    return lax.dot_general(a, b, (((1,), (1,)), ((), ())), preferred_element_type=F32)


def _dot_tn(a, b):
    return lax.dot_general(a, b, (((0,), (0,)), ((), ())), preferred_element_type=F32)


def _split_bf16(x, n):
    parts, r = [], x
    for _ in range(n):
        p = r.astype(BF16)
        parts.append(p)
        r = r - p.astype(F32)
    return parts


def _dot01_l(m01, x, n=3):
    acc = None
    for p in _split_bf16(x, n):
        t = _dot(m01, p)
        acc = t if acc is None else acc + t
    return acc


def _dot01_r(x, m01, n=3):
    acc = None
    for p in _split_bf16(x, n):
        t = _dot(p, m01)
        acc = t if acc is None else acc + t
    return acc


def _iota(shape, dim):
    return lax.broadcasted_iota(jnp.int32, shape, dim)


def _ind(cond, dtype=F32):
    return jnp.where(cond, 1.0, 0.0).astype(dtype)


def _softplus(x):
    return jnp.maximum(x, 0.0) + jnp.log1p(jnp.exp(-jnp.abs(x)))


def _group_ones(width):
    r, c = _iota((width, width), 0), _iota((width, width), 1)
    return _ind((r >> 6) == (c >> 6), BF16)


def _rope_lanes(x, cosf, sinf):
    lane = _iota(x.shape, 1)
    partner = jnp.where((lane & 63) < 32, pltpu.roll(x, 96, 1), pltpu.roll(x, 32, 1))
    return x * cosf + partner * sinf


def _mod_kernel(ct_ref, w_ref, b_ref, o_ref):
    ct = ct_ref[...]
    act = ct * jax.nn.sigmoid(ct)
    w = w_ref[0]
    rows = [jnp.sum(w * act[:, b:b + 1], axis=0, keepdims=True) for b in range(ct.shape[1])]
    o_ref[0] = jnp.concatenate(rows, axis=0) + b_ref[0]


def _modulation(c, ada_w, ada_b):
    depth, d, n = ada_w.shape
    b = c.shape[0]
    tn = 1024
    return pl.pallas_call(
        _mod_kernel,
        grid=(depth, n // tn),
        in_specs=[pl.BlockSpec((d, b), lambda l, j: (0, 0)),
                  pl.BlockSpec((1, d, tn), lambda l, j: (l, 0, j)),
                  pl.BlockSpec((1, 1, tn), lambda l, j: (l, 0, j))],
        out_specs=pl.BlockSpec((1, b, tn), lambda l, j: (l, 0, j)),
        out_shape=jax.ShapeDtypeStruct((depth, b, n), F32),
        compiler_params=_params("parallel", "parallel"),
        name="adaln_mod",
    )(c.T, ada_w, ada_b.reshape(depth, 1, n))


def _adaln(x, nw, shift, scale):
    y = x * lax.rsqrt(jnp.mean(x * x, axis=-1, keepdims=True) + EPS) * nw
    return y * (1.0 + scale) + shift


def _ffn_kernel(x_ref, mod_ref, nw_ref, wup_ref, wdn_ref, *rest, row0, res_w):
    x = x_ref[0]
    if len(rest) > 1:
        ym_ref, yg_ref, yn_ref, wout_ref, o_ref = rest
        y = jnp.concatenate([ym_ref[0], yg_ref[0], yn_ref[0]], axis=-1).astype(BF16)
        x = x + mod_ref[0, 5:6, :] * _dot(y, wout_ref[...])
    else:
        o_ref, = rest
    shift = mod_ref[0, row0:row0 + 1, :]
    scale = mod_ref[0, row0 + 1:row0 + 2, :]
    gate = mod_ref[0, row0 + 2:row0 + 3, :]
    h = _adaln(x, nw_ref[...], shift, scale).astype(BF16)
    acc = jnp.zeros(x.shape, F32)
    for c in range(D_FF // FF_TILE):
        lo = c * FF_TILE
        g = _dot(h, wup_ref[:, lo:lo + FF_TILE])
        u = _dot(h, wup_ref[:, D_FF + lo:D_FF + lo + FF_TILE])
        a = (jax.nn.silu(g) * u).astype(BF16)
        acc = acc + _dot(a, wdn_ref[lo:lo + FF_TILE, :])
    o_ref[0] = x + (res_w * gate) * acc


def _ffn(x, mod, nw, w_up, w_down, row0, mixer=None):
    b, t, d = x.shape
    tm = min(512, t)
    kern = functools.partial(_ffn_kernel, row0=row0, res_w=0.5)
    row = lambda w: pl.BlockSpec((1, tm, w), lambda i, j: (i, j, 0))
    const = lambda r, w: pl.BlockSpec((r, w), lambda i, j: (0, 0), pipeline_mode=pl.Buffered(1))
    in_specs = [row(d), pl.BlockSpec((1, 9, d), lambda i, j: (i, 0, 0)), pl.BlockSpec((1, d), lambda i, j: (0, 0)),
                const(d, 2 * D_FF), const(D_FF, d)]
    args = [x, mod, nw.reshape(1, d), w_up, w_down]
    if mixer is not None:
        y_m, y_g, y_n, w_out = mixer
        in_specs += [row(M_WIDTH), row(G_WIDTH), row(N_WIDTH), const(d, d)]
        args += [y_m, y_g, y_n, w_out.astype(BF16)]
    return pl.pallas_call(
        kern,
        grid=(b, t // tm),
        in_specs=in_specs,
        out_specs=row(d),
        out_shape=jax.ShapeDtypeStruct(x.shape, F32),
        compiler_params=_params("parallel", "parallel"),
        name="ffn",
    )(*args)


def _inproj_kernel(x_ref, mod_ref, nw_ref, w_ref, pm_ref, pg_ref, pq_ref, pkv_ref, gt_ref):
    x = x_ref[0]
    h = _adaln(x, nw_ref[...], mod_ref[0, 3:4, :], mod_ref[0, 4:5, :]).astype(BF16)
    off = 0
    for ref in (pm_ref, pg_ref, pq_ref, pkv_ref, gt_ref):
        wd = ref.shape[-1]
        ref[0] = _dot(h, w_ref[:, off:off + wd])
        off += wd


IN_PERM_WIDTH = 4 * M_WIDTH + 4 * G_WIDTH + N_WIDTH + 6 * KV_WIDTH + LANES


def _permute_w_in(w_in):
    g0, n0 = M_IN, M_IN + G_IN
    gates = jnp.concatenate([w_in[:, 4 * M_WIDTH:M_IN], w_in[:, g0 + 4 * G_WIDTH:g0 + G_IN],
                             w_in[:, n0 + N_WIDTH + 6 * KV_WIDTH:n0 + N_IN]], axis=1)
    gates = jnp.pad(gates, ((0, 0), (0, LANES - gates.shape[1])))
    return jnp.concatenate([w_in[:, :4 * M_WIDTH], w_in[:, g0:g0 + 4 * G_WIDTH],
                            w_in[:, n0:n0 + N_WIDTH + 6 * KV_WIDTH], gates], axis=1).astype(BF16)


def _in_proj(x, mod, nw, w_perm):
    b, t, d = x.shape
    tm = min(512, t)
    widths = (4 * M_WIDTH, 4 * G_WIDTH, N_WIDTH, 6 * KV_WIDTH, LANES)
    return pl.pallas_call(
        _inproj_kernel,
        grid=(b, t // tm),
        in_specs=[pl.BlockSpec((1, tm, d), lambda i, j: (i, j, 0)),
                  pl.BlockSpec((1, 9, d), lambda i, j: (i, 0, 0)),
                  pl.BlockSpec((1, d), lambda i, j: (0, 0)),
                  pl.BlockSpec((d, IN_PERM_WIDTH), lambda i, j: (0, 0), pipeline_mode=pl.Buffered(1))],
        out_specs=[pl.BlockSpec((1, tm, w), lambda i, j: (i, j, 0)) for w in widths],
        out_shape=[jax.ShapeDtypeStruct((b, t, w), F32) for w in widths],
        compiler_params=_params("parallel", "parallel"),
        name="mixer_in_proj",
    )(x, mod, nw.reshape(1, d), w_perm)


def _chunk_masks():
    r, c = _iota((CHUNK, CHUNK), 0), _iota((CHUNK, CHUNK), 1)
    return r, c


def _mlstm_kernel(pm_ref, gt_ref, gb_ref, nw_ref, o_ref, c_scr, m_scr, *, n_chunks):
    @pl.when(pl.program_id(1) == 0)
    def _():
        c_scr[...] = jnp.zeros(c_scr.shape, F32)
        m_scr[...] = jnp.zeros(m_scr.shape, F32)

    r, c = _chunk_masks()
    causal = r >= c
    low = _ind(causal, BF16)
    low_ones = jnp.concatenate([low, jnp.ones((CHUNK, CHUNK), BF16)], axis=1)
    strict_up = _ind(r > c)
    eye = _ind(r == c)
    nw = nw_ref[...]
    gb = gb_ref[...]

    ones_v = jnp.ones((CHUNK, HEAD_DIM), BF16)
    heads = range(M_HEADS)

    def chunk_group(cp, carry):
        rows = [pl.ds(pl.multiple_of((cp * MLSTM_PAR + j) * CHUNK, CHUNK), CHUNK) for j in range(MLSTM_PAR)]
        items = [(j, h) for j in range(MLSTM_PAR) for h in heads]
        every = lambda f, *cols: [f(*args) for args in zip(*cols)]
        g_blk = [gt_ref[0, rw, :] + gb for rw in rows]
        lf_blk = [-_softplus(-g) for g in g_blk]
        cum_blk = [_dot01_l(low, x) for x in lf_blk]
        pick = lambda blk, lane: [blk[j][:, lane + h:lane + h + 1] for j, h in items]
        li_c, lf_c, cum_c = pick(g_blk, GT_M), pick(lf_blk, GT_M + M_HEADS), pick(cum_blk, GT_M + M_HEADS)
        head = lambda k: [pm_ref[0, rows[j], k * M_WIDTH + h * HEAD_DIM:k * M_WIDTH + (h + 1) * HEAD_DIM]
                          for j, h in items]
        q = every(lambda x: (x * (HEAD_DIM ** -0.5)).astype(BF16), head(0))
        k = head(1)
        v_aug = every(lambda x: jnp.concatenate([x.astype(BF16), ones_v], axis=1), head(2))
        dmat = every(lambda f, i: _dot01_l(low_ones, jnp.concatenate([f * strict_up, i * eye], axis=0)), lf_c, li_c)
        dmat = every(lambda x: jnp.where(causal, x, NEG), dmat)
        d_max = every(lambda x: jnp.max(x, axis=-1, keepdims=True), dmat)
        qk = every(lambda a, b: _dot_nt(a, b.astype(BF16)), q, k)
        g_tot = every(lambda x: x[CHUNK - 1:CHUNK, :], cum_c)
        a = every(lambda gt, cm, li: gt - cm + li, g_tot, cum_c, li_c)
        m_loc = every(lambda x: jnp.max(x, axis=0, keepdims=True), a)
        kw = every(lambda kk, x, ml: (kk * jnp.exp(x - ml)).astype(BF16), k, a, m_loc)
        c_loc = every(_dot_tn, kw, v_aug)
        c_prev = [c_scr[h] for h in heads]
        m_prev = [m_scr[h] for h in heads]
        for j in range(MLSTM_PAR):
            sel = [j * M_HEADS + h for h in heads]
            m_inter = [cum_c[i] + m_prev[h] for h, i in enumerate(sel)]
            m_t = [jnp.maximum(m_inter[h], d_max[i]) for h, i in enumerate(sel)]
            s = [(qk[i] * jnp.exp(dmat[i] - m_t[h])).astype(BF16) for h, i in enumerate(sel)]
            inter = [jnp.exp(m_inter[h] - m_t[h]) for h in heads]
            nd = [_dot(s[h], v_aug[i]) + inter[h] * _dot(q[i], c_prev[h].astype(BF16)) for h, i in enumerate(sel)]
            den = [jnp.maximum(jnp.abs(nd[h]), jnp.exp(-m_t[h])) for h in heads]
            hh = [(nd[h] / pltpu.roll(den[h], HEAD_DIM, 1))[:, :HEAD_DIM] for h in heads]
            m_new = [jnp.maximum(g_tot[i] + m_prev[h], m_loc[i]) for h, i in enumerate(sel)]
            s_old = [jnp.exp(g_tot[i] + m_prev[h] - m_new[h]) for h, i in enumerate(sel)]
            s_new = [jnp.exp(m_loc[i] - m_new[h]) for h, i in enumerate(sel)]
            for h in heads:
                o_ref[0, rows[j], h * HEAD_DIM:(h + 1) * HEAD_DIM] = hh[h]
            c_prev = [s_old[h] * c_prev[h] + s_new[h] * c_loc[i] for h, i in enumerate(sel)]
            m_prev = m_new
        for h in heads:
            c_scr[h] = c_prev[h]
            m_scr[h] = m_prev[h]
        return carry

    lax.fori_loop(0, n_chunks // MLSTM_PAR, chunk_group, 0)

    hh = o_ref[0]
    ms = _dot01_r(hh * hh, _group_ones(M_WIDTH), 2) * (1.0 / HEAD_DIM)
    o_ref[0] = hh * lax.rsqrt(ms + EPS) * nw * jax.nn.sigmoid(pm_ref[0, :, 3 * M_WIDTH:4 * M_WIDTH])


def _mlstm(pm, gt, gate_b, norm_w):
    b, t, _ = pm.shape
    tc = min(512, t)
    gb = jnp.zeros((1, LANES), F32).at[0, GT_M:GT_M + 2 * M_HEADS].set(gate_b.reshape(-1))
    kern = functools.partial(_mlstm_kernel, n_chunks=tc // CHUNK)
    return pl.pallas_call(
        kern,
        grid=(b, t // tc),
        in_specs=[pl.BlockSpec((1, tc, 4 * M_WIDTH), lambda i, j: (i, j, 0)),
                  pl.BlockSpec((1, tc, LANES), lambda i, j: (i, j, 0)),
                  pl.BlockSpec((1, LANES), lambda i, j: (0, 0)),
                  pl.BlockSpec((1, M_WIDTH), lambda i, j: (0, 0))],
        out_specs=pl.BlockSpec((1, tc, M_WIDTH), lambda i, j: (i, j, 0)),
        out_shape=jax.ShapeDtypeStruct((b, t, M_WIDTH), F32),
        scratch_shapes=[pltpu.VMEM((M_HEADS, HEAD_DIM, 2 * HEAD_DIM), F32),
                        pltpu.VMEM((M_HEADS, 1, 1), F32)],
        compiler_params=_params("parallel", "arbitrary"),
        name="mlstm",
    )(pm, gt, gb, jnp.tile(norm_w.reshape(1, HEAD_DIM), (1, M_HEADS)))


def _gdn_kernel(pg_ref, gt_ref, cw_ref, hp_ref, nw_ref, o_ref, xbuf, qkv_scr, la_scr, s_scr,
                lin_scr, add_scr, g_scr, *, tc):
    first = pl.program_id(1) == 0

    @pl.when(first)
    def _():
        s_scr[...] = jnp.zeros(s_scr.shape, F32)
        xbuf[0:8, :] = jnp.zeros((8, 3 * G_WIDTH), F32)

    @pl.when(jnp.logical_not(first))
    def _():
        xbuf[0:8, :] = xbuf[tc:tc + 8, :]

    xbuf[8:8 + tc, :] = pg_ref[0, :, 0:3 * G_WIDTH]
    acc = None
    for kk in range(CONV_K):
        term = xbuf[8 - (CONV_K - 1) + kk:8 - (CONV_K - 1) + kk + tc, :] * cw_ref[kk:kk + 1, :]
        acc = term if acc is None else acc + term
    act = acc * jax.nn.sigmoid(acc)
    gones = _group_ones(G_WIDTH)
    for part in range(2):
        xx = act[:, part * G_WIDTH:(part + 1) * G_WIDTH]
        ssq = _dot01_r(xx * xx, gones, 2)
        xx = xx * lax.rsqrt(ssq + EPS)
        if part == 0:
            xx = xx * (HEAD_DIM ** -0.5)
        qkv_scr[:, part * G_WIDTH:(part + 1) * G_WIDTH] = xx
    qkv_scr[:, 2 * G_WIDTH:3 * G_WIDTH] = act[:, 2 * G_WIDTH:3 * G_WIDTH]
    gate = gt_ref[0]
    neg_rate = -jnp.exp(hp_ref[0:1, :])
    la_scr[...] = neg_rate * _softplus(gate + hp_ref[1:2, :])

    r, c = _chunk_masks()
    low = _ind(r >= c, BF16)
    strict_low = r > c
    incl_low = r >= c
    strict_up = _ind(r > c)
    eye = _ind(r == c)
    base_mask = _ind(((r >> 2) == (c >> 2)) & (r > c))
    merge_masks = [_ind((((r >> s) & 1) == 1) & ((c >> s) == (r >> s) - 1)) for s in (2, 3, 4, 5)]
    nw = nw_ref[...]

    def prepare(cp, carry):
        items = [(cp * GDN_PAR + j, h) for j in range(GDN_PAR) for h in range(G_HEADS)]
        every = lambda f, *cols: [f(*args) for args in zip(*cols)]
        rows = [pl.ds(pl.multiple_of((cp * GDN_PAR + j) * CHUNK, CHUNK), CHUNK) for j in range(GDN_PAR)]
        la_blk = [la_scr[rw, :] for rw in rows]
        gam_blk = [_dot01_l(low, x) for x in la_blk]
        beta_blk = [jax.nn.sigmoid(gt_ref[0, rw, :]) for rw in rows]
        pick = lambda blk, lane: [blk[i // G_HEADS][:, lane + h:lane + h + 1] for i, (_, h) in enumerate(items)]
        la_c, gam_c, beta_c = pick(la_blk, GT_G), pick(gam_blk, GT_G), pick(beta_blk, GT_G + G_HEADS)
        head = lambda k: [qkv_scr[rows[i // G_HEADS], k * G_WIDTH + h * HEAD_DIM:k * G_WIDTH + (h + 1) * HEAD_DIM]
                          for i, (_, h) in enumerate(items)]
        q, k, v = head(0), head(1), head(2)
        diff = every(lambda x: _dot01_l(low, x * strict_up), la_c)
        dec_strict = every(lambda x: jnp.exp(jnp.where(strict_low, x, NEG)), diff)
        dec_incl = every(lambda x: jnp.exp(jnp.where(incl_low, x, NEG)), diff)
        kb = every(lambda x: x.astype(BF16), k)
        qk_kk = every(lambda a, b: _dot_nt(jnp.concatenate([a.astype(BF16), b], axis=0), b), q, kb)
        amat = every(lambda b, x, dcy: b * x[CHUNK:] * dcy, beta_c, qk_kk, dec_strict)
        n0 = every(lambda a: (-(a * base_mask)).astype(BF16), amat)
        n0sq = every(lambda n: _dot(n, n).astype(BF16), n0)
        inv = every(lambda n: eye + n.astype(F32), n0)
        inv = every(lambda t, n2: t + _dot(t.astype(BF16), n2), inv, n0sq)
        for mask in merge_masks:
            invb = every(lambda t: t.astype(BF16), inv)
            half = every(lambda tb, a: _dot(tb, (a * mask).astype(BF16)).astype(BF16), invb, amat)
            inv = every(lambda t, hf, tb: t - _dot(hf, tb), inv, half, invb)
        e_gam = every(jnp.exp, gam_c)
        rhs = every(lambda b, e, vv, kk: jnp.concatenate([b * vv, (b * e) * kk], axis=1).astype(BF16),
                    beta_c, e_gam, v, k)
        sol = every(lambda t, x: _dot(t.astype(BF16), x).astype(BF16), inv, rhs)
        gam_last = every(lambda g: g[CHUNK - 1:CHUNK, :], gam_c)
        k_d = every(lambda kk, gl, g: (kk * jnp.exp(gl - g)).astype(BF16), k, gam_last, gam_c)
        p_sol = every(lambda x, dcy, sl: _dot((x[:CHUNK] * dcy).astype(BF16), sl), qk_kk, dec_incl, sol)
        kd_sol = every(_dot_tn, k_d, sol)
        for i, (ci, h) in enumerate(items):
            lin_scr[ci, h] = jnp.concatenate([kd_sol[i][:, HEAD_DIM:], q[i] * e_gam[i] - p_sol[i][:, HEAD_DIM:]],
                                             axis=0).astype(BF16)
            add_scr[ci, h] = jnp.concatenate([kd_sol[i][:, :HEAD_DIM], p_sol[i][:, :HEAD_DIM]], axis=0)
            g_scr[ci, h] = jnp.broadcast_to(jnp.exp(gam_last[i]), (1, HEAD_DIM))
        return carry

    lax.fori_loop(0, tc // (CHUNK * GDN_PAR), prepare, 0)

    heads = range(G_HEADS)

    def advance(ci, s_prev):
        r0 = pl.multiple_of(ci * CHUNK, CHUNK)
        prod = [_dot(lin_scr[ci, h], s_prev[h].astype(BF16)) for h in heads]
        add = [add_scr[ci, h] for h in heads]
        for h in heads:
            o_ref[0, pl.ds(r0, CHUNK), h * HEAD_DIM:(h + 1) * HEAD_DIM] = prod[h][CHUNK:] + add[h][CHUNK:]
        return tuple(g_scr[ci, h] * s_prev[h] - prod[h][:CHUNK] + add[h][:CHUNK] for h in heads)

    s_last = lax.fori_loop(0, tc // CHUNK, advance, tuple(s_scr[h] for h in heads))
    for h in heads:
        s_scr[h] = s_last[h]

    o = o_ref[0]
    ms = _dot01_r(o * o, gones, 2) * (1.0 / HEAD_DIM)
    z = pg_ref[0, :, 3 * G_WIDTH:4 * G_WIDTH]
    o_ref[0] = o * lax.rsqrt(ms + EPS) * nw * (z * jax.nn.sigmoid(z))


def _gdn(pg, gt, conv_w, a_log, dt_bias, norm_w):
    b, t, _ = pg.shape
    tc = min(512, t)
    hp = jnp.zeros((2, LANES), F32)
    hp = hp.at[0, GT_G:GT_G + G_HEADS].set(a_log).at[1, GT_G:GT_G + G_HEADS].set(dt_bias)
    kern = functools.partial(_gdn_kernel, tc=tc)
    return pl.pallas_call(
        kern,
        grid=(b, t // tc),
        in_specs=[pl.BlockSpec((1, tc, 4 * G_WIDTH), lambda i, j: (i, j, 0)),
                  pl.BlockSpec((1, tc, LANES), lambda i, j: (i, j, 0)),
                  pl.BlockSpec((CONV_K, 3 * G_WIDTH), lambda i, j: (0, 0)),
                  pl.BlockSpec((2, LANES), lambda i, j: (0, 0)),
                  pl.BlockSpec((1, G_WIDTH), lambda i, j: (0, 0))],
        out_specs=pl.BlockSpec((1, tc, G_WIDTH), lambda i, j: (i, j, 0)),
        out_shape=jax.ShapeDtypeStruct((b, t, G_WIDTH), F32),
        scratch_shapes=[pltpu.VMEM((tc + 8, 3 * G_WIDTH), F32),
                        pltpu.VMEM((tc, 3 * G_WIDTH), F32),
                        pltpu.VMEM((tc, LANES), F32),
                        pltpu.VMEM((G_HEADS, HEAD_DIM, HEAD_DIM), F32),
                        pltpu.VMEM((tc // CHUNK, G_HEADS, 2 * CHUNK, HEAD_DIM), BF16),
                        pltpu.VMEM((tc // CHUNK, G_HEADS, 2 * CHUNK, HEAD_DIM), F32),
                        pltpu.VMEM((tc // CHUNK, G_HEADS, 1, HEAD_DIM), F32)],
        compiler_params=_params("parallel", "arbitrary"),
        name="gdn",
    )(pg, gt, conv_w.astype(F32), hp, jnp.tile(norm_w.reshape(1, HEAD_DIM), (1, G_HEADS)))


def _rope_tables(pos):
    half = HEAD_DIM // 2
    inv_freq = jnp.power(ROPE_THETA, -jnp.arange(half, dtype=F32) / half)
    ang = pos.astype(F32)[:, None] * inv_freq[None, :]
    cos, sin = jnp.cos(ang), jnp.sin(ang)
    cosf = jnp.tile(cos, (1, LANES // half))
    sinf = jnp.tile(jnp.concatenate([-sin, sin], axis=1), (1, LANES // HEAD_DIM))
    return cosf, sinf


KS_AUG = LANES + HEAD_DIM


SLC_TILE = 512
WIN_SPAN = Q_BLOCK + WINDOW


V_ROWS = HEAD_DIM + 16
FORCED_BLOCKS = 3
LOG2E = 1.4426950408889634


def _nsa_prep_t_kernel(pq_ref, pkv_ref, cos_ref, sin_ref, nw_ref, q_ref, ks_ref, vs_ref, kw_ref, vw_ref, xc_ref,
                       cmp_scr):
    cosf, sinf = cos_ref[...], sin_ref[...]
    gones = _group_ones(LANES)
    tm = cosf.shape[0]

    for s in range(2):
        cmp_scr[s] = pkv_ref[0, :, s * KV_WIDTH:(s + 1) * KV_WIDTH]
    for p in range(CMP_STRIDE):
        for s in range(2):
            grp = cmp_scr[s, pl.ds(p, tm // CMP_STRIDE, stride=CMP_STRIDE), :]
            for h in range(N_KV_HEADS):
                xc_ref[0, s, h, :, p * HEAD_DIM:(p + 1) * HEAD_DIM] = grp[:, h * HEAD_DIM:(h + 1) * HEAD_DIM]

    def norm_rope(x, w):
        ssq = _dot01_r(x * x, gones, 2)
        y = x * lax.rsqrt(ssq * (1.0 / HEAD_DIM) + EPS) * w
        return _rope_lanes(y, cosf, sinf)

    for s in range(N_WIDTH // LANES):
        qs = norm_rope(pq_ref[0, :, s * LANES:(s + 1) * LANES], nw_ref[0:1, :]) * (HEAD_DIM ** -0.5 * LOG2E)
        qs_t = qs.T.astype(BF16)
        for half in range(2):
            hkv, g = divmod(2 * s + half, N_GROUP)
            for blk in range(tm // Q_BLOCK):
                col = (blk * N_GROUP + g) * Q_BLOCK
                q_ref[0, hkv, :, col:col + Q_BLOCK] = qs_t[half * HEAD_DIM:(half + 1) * HEAD_DIM,
                                                           blk * Q_BLOCK:(blk + 1) * Q_BLOCK]
    kv = lambda i: pkv_ref[0, :, i * KV_WIDTH:(i + 1) * KV_WIDTH]
    tok = pl.program_id(1) * tm + _iota((tm, LANES), 0)
    onehot = _ind((tok >> 6) == _iota((tm, LANES), 1), BF16)
    k_slc = norm_rope(kv(2), nw_ref[2:3, :])
    k_win = norm_rope(kv(4), nw_ref[3:4, :])
    ones = jnp.ones((V_ROWS - HEAD_DIM, tm), BF16)
    for h in range(N_KV_HEADS):
        ks_ref[0, h, :, 0:LANES] = onehot
        ks_ref[0, h, :, LANES:LANES + HEAD_DIM] = k_slc[:, h * HEAD_DIM:(h + 1) * HEAD_DIM].astype(BF16)
        kw_ref[0, h] = k_win[:, h * HEAD_DIM:(h + 1) * HEAD_DIM].astype(BF16)
    for ref, slab in ((vs_ref, kv(3)), (vw_ref, kv(5))):
        v_t = slab.T.astype(BF16)
        for h in range(N_KV_HEADS):
            ref[0, h, 0:HEAD_DIM, :] = v_t[h * HEAD_DIM:(h + 1) * HEAD_DIM]
            ref[0, h, HEAD_DIM:V_ROWS, :] = ones


def _nsa_prep_t(pq, pkv, qk_norm):
    b, t, _ = pq.shape
    assert t // SLC_BLOCK <= LANES
    tm = min(512, t)
    flat = CMP_STRIDE * HEAD_DIM
    cosf, sinf = _rope_tables(jnp.arange(t, dtype=jnp.int32))
    nw = jnp.tile(qk_norm, (1, LANES // HEAD_DIM))
    rows = lambda w: pl.BlockSpec((1, N_KV_HEADS, tm, w), lambda i, j: (i, 0, j, 0))
    cols = lambda r, w: pl.BlockSpec((1, N_KV_HEADS, r, w), lambda i, j: (i, 0, 0, j))
    shp = lambda r, w: jax.ShapeDtypeStruct((b, N_KV_HEADS, r, w), BF16)
    return pl.pallas_call(
        _nsa_prep_t_kernel,
        grid=(b, t // tm),
        in_specs=[pl.BlockSpec((1, tm, N_WIDTH), lambda i, j: (i, j, 0)),
                  pl.BlockSpec((1, tm, 6 * KV_WIDTH), lambda i, j: (i, j, 0)),
                  pl.BlockSpec((tm, LANES), lambda i, j: (j, 0)),
                  pl.BlockSpec((tm, LANES), lambda i, j: (j, 0)),
                  pl.BlockSpec((4, LANES), lambda i, j: (0, 0))],
        out_specs=[cols(HEAD_DIM, N_GROUP * tm), rows(KS_AUG), cols(V_ROWS, tm), rows(HEAD_DIM), cols(V_ROWS, tm),
                   pl.BlockSpec((1, 2, N_KV_HEADS, tm // CMP_STRIDE, flat), lambda i, j: (i, 0, 0, j, 0))],
        out_shape=[shp(HEAD_DIM, N_GROUP * t), shp(t, KS_AUG), shp(V_ROWS, t), shp(t, HEAD_DIM), shp(V_ROWS, t),
                   jax.ShapeDtypeStruct((b, 2, N_KV_HEADS, t // CMP_STRIDE, flat), F32)],
        scratch_shapes=[pltpu.VMEM((2, tm, KV_WIDTH), F32)],
        compiler_params=_params("parallel", "parallel"),
        name="nsa_prep",
    )(pq, pkv, cosf, sinf, nw)


def _compress_t_kernel(x_ref, pos_ref, w1_ref, w2_ref, cos_ref, sin_ref, nw_ref, kc_ref, vc_ref):
    x = x_ref[0, 0, 0]
    half = CMP_STRIDE * HEAD_DIM
    a = _dot((x + pos_ref[0, 0:1, :]).astype(BF16), w1_ref[0, 0:half, :])
    bmat = _dot((x + pos_ref[0, 1:2, :]).astype(BF16), w1_ref[0, half:2 * half, :])
    hid = a + jnp.concatenate([bmat[1:, :], jnp.zeros((1, CMP_HIDDEN), F32)], axis=0)
    hid = hid * jax.nn.sigmoid(hid)
    out = _dot(hid.astype(BF16), w2_ref[0])

    @pl.when(pl.program_id(2) == 0)
    def _():
        ssq = jnp.sum(out * out, axis=-1, keepdims=True)
        normed = out * lax.rsqrt(ssq * (1.0 / HEAD_DIM) + EPS) * nw_ref[...]
        kc_ref[0, 0] = _rope_lanes(normed, cos_ref[...], sin_ref[...]).astype(BF16)

    @pl.when(pl.program_id(2) == 1)
    def _():
        vc_ref[0, 0] = out.T[0:HEAD_DIM].astype(BF16)


def _nsa_compress_t(x, cmp_pos, cmp_w1, cmp_w2, kc_norm):
    b, _, _, n16, flat = x.shape
    pos = cmp_pos.reshape(2, 2, flat)
    w1 = cmp_w1.astype(BF16)
    w2 = jnp.pad(cmp_w2, ((0, 0), (0, 0), (0, LANES - HEAD_DIM))).astype(BF16)
    cosf, sinf = _rope_tables(jnp.arange(n16, dtype=jnp.int32) * CMP_STRIDE + CMP_BLOCK - 1)
    nw = jnp.pad(kc_norm.reshape(1, HEAD_DIM), ((0, 0), (0, LANES - HEAD_DIM)))
    return pl.pallas_call(
        _compress_t_kernel,
        grid=(b, N_KV_HEADS, 2),
        in_specs=[pl.BlockSpec((1, 1, 1, n16, flat), lambda i, h, s: (i, s, h, 0, 0)),
                  pl.BlockSpec((1, 2, flat), lambda i, h, s: (s, 0, 0)),
                  pl.BlockSpec((1, 2 * flat, CMP_HIDDEN), lambda i, h, s: (s, 0, 0)),
                  pl.BlockSpec((1, CMP_HIDDEN, LANES), lambda i, h, s: (s, 0, 0)),
                  pl.BlockSpec((n16, LANES), lambda i, h, s: (0, 0)),
                  pl.BlockSpec((n16, LANES), lambda i, h, s: (0, 0)),
                  pl.BlockSpec((1, LANES), lambda i, h, s: (0, 0))],
        out_specs=[pl.BlockSpec((1, 1, n16, LANES), lambda i, h, s: (i, h, 0, 0)),
                   pl.BlockSpec((1, 1, HEAD_DIM, n16), lambda i, h, s: (i, h, 0, 0))],
        out_shape=[jax.ShapeDtypeStruct((b, N_KV_HEADS, n16, LANES), BF16),
                   jax.ShapeDtypeStruct((b, N_KV_HEADS, HEAD_DIM, n16), BF16)],
        compiler_params=_params("parallel", "parallel", "arbitrary"),
        name="nsa_compress",
    )(x, pos, w1, w2, cosf, sinf, nw)


def _col_max(x):
    return jnp.max(x, axis=0, keepdims=True)


def _nsa_attn_t_kernel(q_ref, kc_ref, vc_ref, ks_ref, vs_ref, kw_ref, vw_ref, gt_ref, o_ref,
                       sa_scr, sb_scr, pa_scr, pb_scr, *, n_slc):
    qi = pl.program_id(1)
    start = qi * Q_BLOCK
    cols = N_GROUP * Q_BLOCK
    kvh = range(N_KV_HEADS)
    q_t = [q_ref[0, h] for h in kvh]
    n16 = kc_ref.shape[2]

    def per_head(x):
        return jnp.concatenate([x] * N_GROUP, axis=1)

    def finish(acc):
        return acc[0:HEAD_DIM] * (1.0 / acc[HEAD_DIM:HEAD_DIM + 1])

    def probs(s, m):
        return jnp.exp2(s - m).astype(BF16)

    gates_t = jax.nn.sigmoid(gt_ref[0]).T

    tq_q = start + _iota((n16, Q_BLOCK), 1)
    valid_c = (_iota((n16, Q_BLOCK), 0) * CMP_STRIDE + (CMP_BLOCK - 1)) <= tq_q
    bias_c = per_head(jnp.where(valid_c, 0.0, NEG))
    any_c = per_head(_ind(start + _iota((1, Q_BLOCK), 1) >= CMP_BLOCK - 1))
    s_c = [_dot(kc_ref[0, h][:, :HEAD_DIM], q_t[h]) + bias_c for h in kvh]
    e_c = [jnp.exp2(x - _col_max(x)) for x in s_c]
    p_c = [x * (any_c / jnp.sum(x, axis=0, keepdims=True)) for x in e_c]
    o_c = [_dot(vc_ref[0, h], p_c[h].astype(BF16)) for h in kvh]

    p_sum = [x[:, 0:Q_BLOCK] for x in p_c]
    for g in range(1, N_GROUP):
        p_sum = [p_sum[h] + p_c[h][:, g * Q_BLOCK:(g + 1) * Q_BLOCK] for h in kvh]
    sb, cn = _iota((n_slc, n16), 0), _iota((n_slc, n16), 1)
    c_lo, s_lo = cn * CMP_STRIDE, sb * SLC_BLOCK
    ov = jnp.maximum(jnp.minimum(c_lo + CMP_BLOCK, s_lo + SLC_BLOCK) - jnp.maximum(c_lo, s_lo), 0)
    ov = jnp.where(cn < n16 - 1, ov, 0)
    overlap_t = (ov.astype(F32) * (1.0 / CMP_BLOCK)).astype(BF16)
    imp = [_dot01_l(overlap_t, x, 2) for x in p_sum]
    tq_i = start + _iota((n_slc, Q_BLOCK), 1)
    blk = _iota((n_slc, Q_BLOCK), 0)
    cur = tq_i >> 6
    forced = (blk == 0) | (blk == cur) | (blk == cur - 1)
    imp = [jnp.where(blk * SLC_BLOCK <= tq_i, jnp.where(forced, -3e38, x), NEG) for x in imp]
    blk_f = blk.astype(F32)
    sel_t = [_ind(forced) for _ in kvh]
    for _ in range(N_SELECTED - FORCED_BLOCKS):
        top = [_col_max(x) for x in imp]
        first = [jnp.min(jnp.where(imp[h] == top[h], blk_f, float(n_slc)), axis=0, keepdims=True) for h in kvh]
        hit = [blk_f == first[h] for h in kvh]
        sel_t = [jnp.where(hit[h], 1.0, sel_t[h]) for h in kvh]
        imp = [jnp.where(hit[h], -3e38, imp[h]) for h in kvh]

    w0 = pl.multiple_of(jnp.maximum(start - WINDOW, 0), Q_BLOCK)
    dist = start + _iota((WIN_SPAN, Q_BLOCK), 1) - (w0 + _iota((WIN_SPAN, Q_BLOCK), 0))
    bias_w = per_head(jnp.where((dist >= 0) & (dist < WINDOW), 0.0, NEG))
    s_w = [_dot(kw_ref[0, h, pl.ds(w0, WIN_SPAN), :], q_t[h]) + bias_w for h in kvh]
    p_w = [probs(x, _col_max(x)) for x in s_w]
    o_w = [finish(_dot(vw_ref[0, h, :, pl.ds(w0, WIN_SPAN)], p_w[h])) for h in kvh]

    own = pl.multiple_of(start, Q_BLOCK)
    bias_d = per_head(jnp.where(_iota((Q_BLOCK, Q_BLOCK), 0) <= _iota((Q_BLOCK, Q_BLOCK), 1), 0.0, NEG))
    s_d = [_dot(ks_ref[0, h, pl.ds(own, Q_BLOCK), LANES:LANES + HEAD_DIM], q_t[h]) + bias_d for h in kvh]
    m_d = [_col_max(x) for x in s_d]
    acc_d = [_dot(vs_ref[0, h, :, pl.ds(own, Q_BLOCK)], probs(s_d[h], m_d[h])) for h in kvh]

    def augmented(h):
        bias_t = (jnp.where(blk < 2 * qi, sel_t[h], 0.0) - 1.0) * BIG
        if n_slc < LANES:
            bias_t = jnp.concatenate([bias_t, jnp.zeros((LANES - n_slc, Q_BLOCK), F32)], axis=0)
        return jnp.concatenate([per_head(bias_t.astype(BF16)), q_t[h]], axis=0)

    q_aug = [augmented(h) for h in kvh]
    n_tiles = (start + SLC_TILE - 1) // SLC_TILE

    def scores(h, kt):
        k0 = pl.multiple_of(kt * SLC_TILE, SLC_TILE)
        return _dot(ks_ref[0, h, pl.ds(k0, SLC_TILE), :], q_aug[h])

    def weighted(h, p, kt):
        k0 = pl.multiple_of(kt * SLC_TILE, SLC_TILE)
        return _dot(vs_ref[0, h, :, pl.ds(k0, SLC_TILE)], p)

    top_tile = ks_ref.shape[2] // SLC_TILE - 1

    def half_step(kt, s_in, s_out, p_in, p_out, a_prev, m, acc):
        for h in kvh:
            s_out[h] = scores(h, jnp.minimum(kt + 1, top_tile))
        acc = [a_prev[h] * acc[h] + weighted(h, p_in[h], jnp.maximum(kt - 1, 0)) for h in kvh]
        s = [s_in[h] for h in kvh]
        m_new = [jnp.maximum(m[h], _col_max(s[h])) for h in kvh]
        for h in kvh:
            p_out[h] = probs(s[h], m_new[h])
        return [jnp.exp2(m[h] - m_new[h]) for h in kvh], m_new, acc

    def slc_pair(j, carry):
        carry = half_step(2 * j, sa_scr, sb_scr, pb_scr, pa_scr, *carry)
        return half_step(2 * j + 1, sb_scr, sa_scr, pa_scr, pb_scr, *carry)

    for h in kvh:
        sa_scr[h] = scores(h, 0)
        pb_scr[h] = jnp.zeros((SLC_TILE, cols), BF16)
    n_pairs = (n_tiles + 1) // 2
    init = ([jnp.ones((1, cols), F32) for _ in kvh], m_d, acc_d)
    a_last, _, acc_s = lax.fori_loop(0, n_pairs, slc_pair, init)
    o_s = [finish(a_last[h] * acc_s[h] + weighted(h, pb_scr[h], jnp.maximum(2 * n_pairs - 1, 0))) for h in kvh]

    def gate_row(h, branch):
        c0 = GT_N + branch * N_HEADS + h * N_GROUP
        return jnp.concatenate([gates_t[c0 + g:c0 + g + 1] for g in range(N_GROUP)], axis=1)

    for h in kvh:
        y_t = gate_row(h, 0) * o_c[h] + gate_row(h, 1) * o_s[h] + gate_row(h, 2) * o_w[h]
        y = jnp.concatenate([y_t[:, g * Q_BLOCK:(g + 1) * Q_BLOCK] for g in range(N_GROUP)], axis=0)
        o_ref[0, :, h * N_GROUP * HEAD_DIM:(h + 1) * N_GROUP * HEAD_DIM] = y.T


def _nsa_attention_t(q_t, kc, vc_t, ks, vs_t, kw, vw_t, gt):
    b, _, t, _ = ks.shape
    n16 = kc.shape[2]
    n_slc = t // SLC_BLOCK
    cols = N_GROUP * Q_BLOCK
    assert (t // SLC_TILE) % 2 == 0
    assert n_slc >= N_SELECTED
    whole = lambda r, w: pl.BlockSpec((1, N_KV_HEADS, r, w), lambda i, j: (i, 0, 0, 0))
    slot = lambda dtype: pltpu.VMEM((N_KV_HEADS, SLC_TILE, cols), dtype)
    kern = functools.partial(_nsa_attn_t_kernel, n_slc=n_slc)
    return pl.pallas_call(
        kern,
        grid=(b, t // Q_BLOCK),
        in_specs=[pl.BlockSpec((1, N_KV_HEADS, HEAD_DIM, cols), lambda i, j: (i, 0, 0, j)),
                  whole(n16, LANES), whole(HEAD_DIM, n16),
                  whole(t, KS_AUG), whole(V_ROWS, t), whole(t, HEAD_DIM), whole(V_ROWS, t),
                  pl.BlockSpec((1, Q_BLOCK, LANES), lambda i, j: (i, j, 0))],
        out_specs=pl.BlockSpec((1, Q_BLOCK, N_WIDTH), lambda i, j: (i, j, 0)),
        out_shape=jax.ShapeDtypeStruct((b, t, N_WIDTH), F32),
        scratch_shapes=[slot(F32), slot(F32), slot(BF16), slot(BF16)],
        compiler_params=_params("parallel", "arbitrary"),
        name="nsa_attention",
    )(q_t, kc, vc_t, ks, vs_t, kw, vw_t, gt)


def _nsa(pq, pkv, gt, qk_norm, cmp_pos, cmp_w1, cmp_w2):
    q_t, ks, vs_t, kw, vw_t, x_cmp = _nsa_prep_t(pq, pkv, qk_norm)
    kc, vc_t = _nsa_compress_t(x_cmp, cmp_pos, cmp_w1, cmp_w2, qk_norm[1])
    return _nsa_attention_t(q_t, kc, vc_t, ks, vs_t, kw, vw_t, gt)


def _mixer_heads(x, mod, nw, w_in, m_gate_b, m_norm_w, g_conv_w, g_a_log, g_dt_bias, g_norm_w,
                 n_qk_norm, n_cmp_pos, n_cmp_w1, n_cmp_w2):
    pm, pg, pq, pkv, gt = _in_proj(x, mod, nw, _permute_w_in(w_in))
    y_m = _mlstm(pm, gt, m_gate_b, m_norm_w)
    y_g = _gdn(pg, gt, g_conv_w, g_a_log, g_dt_bias, g_norm_w)
    y_n = _nsa(pq, pkv, gt, n_qk_norm, n_cmp_pos, n_cmp_w1, n_cmp_w2)
    return y_m, y_g, y_n


def kernel(x, c, ada_w, ada_b, norm_w, ffn_w_up, ffn_w_down, w_in, w_out, mlstm_gate_b, mlstm_norm_w,
           gdn_conv_w, gdn_a_log, gdn_dt_bias, gdn_norm_w, nsa_qk_norm, nsa_cmp_pos, nsa_cmp_w1, nsa_cmp_w2):
    depth = ada_w.shape[0]
    b = x.shape[0]
    mods = _modulation(c, ada_w, ada_b).reshape(depth, b, 9, D_MODEL)
    for l in range(depth):
        mod = mods[l]
        x = _ffn(x, mod, norm_w[l, 0], ffn_w_up[l, 0].astype(BF16), ffn_w_down[l, 0].astype(BF16), 0)
        ys = _mixer_heads(x, mod, norm_w[l, 1], w_in[l], mlstm_gate_b[l], mlstm_norm_w[l],
                          gdn_conv_w[l], gdn_a_log[l], gdn_dt_bias[l], gdn_norm_w[l],
                          nsa_qk_norm[l], nsa_cmp_pos[l], nsa_cmp_w1[l], nsa_cmp_w2[l])
        x = _ffn(x, mod, norm_w[l, 2], ffn_w_up[l, 1].astype(BF16), ffn_w_down[l, 1].astype(BF16), 6,
                 mixer=(*ys, w_out[l]))
    return x
```

```python
import functools

import jax
import jax.numpy as jnp
from jax import lax
from jax.experimental import pallas as pl
from jax.experimental.pallas import tpu as pltpu

F32 = jnp.float32
BF16 = jnp.bfloat16

D_MODEL = 1024
HEAD_DIM = 64
D_FF = 2816
EPS = 1e-6
NEG = -1e30
BIG = 1e30

M_HEADS = 4
G_HEADS = 4
N_HEADS = 8
N_KV_HEADS = 2
N_GROUP = N_HEADS // N_KV_HEADS
CHUNK = 64
GDN_PAR = 8
MLSTM_PAR = 8
CONV_K = 4
CMP_BLOCK = 32
CMP_STRIDE = 16
CMP_HIDDEN = 256
SLC_BLOCK = 64
N_SELECTED = 16
WINDOW = 512
Q_BLOCK = 128
ROPE_THETA = 10000.0

M_WIDTH = M_HEADS * HEAD_DIM
G_WIDTH = G_HEADS * HEAD_DIM
N_WIDTH = N_HEADS * HEAD_DIM
KV_WIDTH = N_KV_HEADS * HEAD_DIM
M_IN = 4 * M_WIDTH + 2 * M_HEADS
G_IN = 4 * G_WIDTH + 2 * G_HEADS
N_IN = N_WIDTH + 6 * KV_WIDTH + 3 * N_HEADS

LANES = 128
FF_TILE = 256
VMEM_LIMIT = 56 * 1024 * 1024

GT_M = 0
GT_G = 8
GT_N = 16


def _params(*sem):
    return pltpu.CompilerParams(dimension_semantics=sem, vmem_limit_bytes=VMEM_LIMIT)


def _dot(a, b):
    return jnp.dot(a, b, preferred_element_type=F32)


def _dot_nt(a, b):
    return lax.dot_general(a, b, (((1,), (1,)), ((), ())), preferred_element_type=F32)


def _dot_tn(a, b):
    return lax.dot_general(a, b, (((0,), (0,)), ((), ())), preferred_element_type=F32)


def _split_bf16(x, n):
    parts, r = [], x
    for _ in range(n):
        p = r.astype(BF16)
        parts.append(p)
        r = r - p.astype(F32)
    return parts


def _dot01_l(m01, x, n=3):
    acc = None
    for p in _split_bf16(x, n):
        t = _dot(m01, p)
        acc = t if acc is None else acc + t
    return acc


def _dot01_r(x, m01, n=3):
    acc = None
    for p in _split_bf16(x, n):
        t = _dot(p, m01)
        acc = t if acc is None else acc + t
    return acc


def _iota(shape, dim):
    return lax.broadcasted_iota(jnp.int32, shape, dim)


def _ind(cond, dtype=F32):
    return jnp.where(cond, 1.0, 0.0).astype(dtype)


def _softplus(x):
    return jnp.maximum(x, 0.0) + jnp.log1p(jnp.exp(-jnp.abs(x)))


def _group_ones(width):
    r, c = _iota((width, width), 0), _iota((width, width), 1)
    return _ind((r >> 6) == (c >> 6), BF16)


def _rope_lanes(x, cosf, sinf):
    lane = _iota(x.shape, 1)
    partner = jnp.where((lane & 63) < 32, pltpu.roll(x, 96, 1), pltpu.roll(x, 32, 1))
    return x * cosf + partner * sinf


def _mod_kernel(ct_ref, w_ref, b_ref, o_ref):
    ct = ct_ref[...]
    act = ct * jax.nn.sigmoid(ct)
    w = w_ref[0]
    rows = [jnp.sum(w * act[:, b:b + 1], axis=0, keepdims=True) for b in range(ct.shape[1])]
    o_ref[0] = jnp.concatenate(rows, axis=0) + b_ref[0]


def _modulation(c, ada_w, ada_b):
    depth, d, n = ada_w.shape
    b = c.shape[0]
    tn = 1024
    return pl.pallas_call(
        _mod_kernel,
        grid=(depth, n // tn),
        in_specs=[pl.BlockSpec((d, b), lambda l, j: (0, 0)),
                  pl.BlockSpec((1, d, tn), lambda l, j: (l, 0, j)),
                  pl.BlockSpec((1, 1, tn), lambda l, j: (l, 0, j))],
        out_specs=pl.BlockSpec((1, b, tn), lambda l, j: (l, 0, j)),
        out_shape=jax.ShapeDtypeStruct((depth, b, n), F32),
        compiler_params=_params("parallel", "parallel"),
        name="adaln_mod",
    )(c.T, ada_w, ada_b.reshape(depth, 1, n))


def _adaln(x, nw, shift, scale):
    y = x * lax.rsqrt(jnp.mean(x * x, axis=-1, keepdims=True) + EPS) * nw
    return y * (1.0 + scale) + shift


def _ffn_kernel(x_ref, mod_ref, nw_ref, wup_ref, wdn_ref, *rest, row0, res_w):
    x = x_ref[0]
    if len(rest) > 1:
        ym_ref, yg_ref, yn_ref, wout_ref, o_ref = rest
        y = jnp.concatenate([ym_ref[0], yg_ref[0], yn_ref[0]], axis=-1).astype(BF16)
        x = x + mod_ref[0, 5:6, :] * _dot(y, wout_ref[...])
    else:
        o_ref, = rest
    shift = mod_ref[0, row0:row0 + 1, :]
    scale = mod_ref[0, row0 + 1:row0 + 2, :]
    gate = mod_ref[0, row0 + 2:row0 + 3, :]
    h = _adaln(x, nw_ref[...], shift, scale).astype(BF16)
    acc = jnp.zeros(x.shape, F32)
    for c in range(D_FF // FF_TILE):
        lo = c * FF_TILE
        g = _dot(h, wup_ref[:, lo:lo + FF_TILE])
        u = _dot(h, wup_ref[:, D_FF + lo:D_FF + lo + FF_TILE])
        a = (jax.nn.silu(g) * u).astype(BF16)
        acc = acc + _dot(a, wdn_ref[lo:lo + FF_TILE, :])
    o_ref[0] = x + (res_w * gate) * acc


def _ffn(x, mod, nw, w_up, w_down, row0, mixer=None):
    b, t, d = x.shape
    tm = min(512, t)
    kern = functools.partial(_ffn_kernel, row0=row0, res_w=0.5)
    row = lambda w: pl.BlockSpec((1, tm, w), lambda i, j: (i, j, 0))
    const = lambda r, w: pl.BlockSpec((r, w), lambda i, j: (0, 0), pipeline_mode=pl.Buffered(1))
    in_specs = [row(d), pl.BlockSpec((1, 9, d), lambda i, j: (i, 0, 0)), pl.BlockSpec((1, d), lambda i, j: (0, 0)),
                const(d, 2 * D_FF), const(D_FF, d)]
    args = [x, mod, nw.reshape(1, d), w_up, w_down]
    if mixer is not None:
        y_m, y_g, y_n, w_out = mixer
        in_specs += [row(M_WIDTH), row(G_WIDTH), row(N_WIDTH), const(d, d)]
        args += [y_m, y_g, y_n, w_out.astype(BF16)]
    return pl.pallas_call(
        kern,
        grid=(b, t // tm),
        in_specs=in_specs,
        out_specs=row(d),
        out_shape=jax.ShapeDtypeStruct(x.shape, F32),
        compiler_params=_params("parallel", "parallel"),
        name="ffn",
    )(*args)


def _inproj_kernel(x_ref, mod_ref, nw_ref, w_ref, pm_ref, pg_ref, pq_ref, pkv_ref, gt_ref):
    x = x_ref[0]
    h = _adaln(x, nw_ref[...], mod_ref[0, 3:4, :], mod_ref[0, 4:5, :]).astype(BF16)
    off = 0
    for ref in (pm_ref, pg_ref, pq_ref, pkv_ref, gt_ref):
        wd = ref.shape[-1]
        ref[0] = _dot(h, w_ref[:, off:off + wd])
        off += wd


IN_PERM_WIDTH = 4 * M_WIDTH + 4 * G_WIDTH + N_WIDTH + 6 * KV_WIDTH + LANES


def _permute_w_in(w_in):
    g0, n0 = M_IN, M_IN + G_IN
    gates = jnp.concatenate([w_in[:, 4 * M_WIDTH:M_IN], w_in[:, g0 + 4 * G_WIDTH:g0 + G_IN],
                             w_in[:, n0 + N_WIDTH + 6 * KV_WIDTH:n0 + N_IN]], axis=1)
    gates = jnp.pad(gates, ((0, 0), (0, LANES - gates.shape[1])))
    return jnp.concatenate([w_in[:, :4 * M_WIDTH], w_in[:, g0:g0 + 4 * G_WIDTH],
                            w_in[:, n0:n0 + N_WIDTH + 6 * KV_WIDTH], gates], axis=1).astype(BF16)


def _in_proj(x, mod, nw, w_perm):
    b, t, d = x.shape
    tm = min(512, t)
    widths = (4 * M_WIDTH, 4 * G_WIDTH, N_WIDTH, 6 * KV_WIDTH, LANES)
    return pl.pallas_call(
        _inproj_kernel,
        grid=(b, t // tm),
        in_specs=[pl.BlockSpec((1, tm, d), lambda i, j: (i, j, 0)),
                  pl.BlockSpec((1, 9, d), lambda i, j: (i, 0, 0)),
                  pl.BlockSpec((1, d), lambda i, j: (0, 0)),
                  pl.BlockSpec((d, IN_PERM_WIDTH), lambda i, j: (0, 0), pipeline_mode=pl.Buffered(1))],
        out_specs=[pl.BlockSpec((1, tm, w), lambda i, j: (i, j, 0)) for w in widths],
        out_shape=[jax.ShapeDtypeStruct((b, t, w), F32) for w in widths],
        compiler_params=_params("parallel", "parallel"),
        name="mixer_in_proj",
    )(x, mod, nw.reshape(1, d), w_perm)


def _chunk_masks():
    r, c = _iota((CHUNK, CHUNK), 0), _iota((CHUNK, CHUNK), 1)
    return r, c


def _mlstm_kernel(pm_ref, gt_ref, gb_ref, nw_ref, o_ref, c_scr, m_scr, *, n_chunks):
    @pl.when(pl.program_id(1) == 0)
    def _():
        c_scr[...] = jnp.zeros(c_scr.shape, F32)
        m_scr[...] = jnp.zeros(m_scr.shape, F32)

    r, c = _chunk_masks()
    causal = r >= c
    low = _ind(causal, BF16)
    low_ones = jnp.concatenate([low, jnp.ones((CHUNK, CHUNK), BF16)], axis=1)
    strict_up = _ind(r > c)
    eye = _ind(r == c)
    nw = nw_ref[...]
    gb = gb_ref[...]

    ones_v = jnp.ones((CHUNK, HEAD_DIM), BF16)
    heads = range(M_HEADS)

    def chunk_group(cp, carry):
        rows = [pl.ds(pl.multiple_of((cp * MLSTM_PAR + j) * CHUNK, CHUNK), CHUNK) for j in range(MLSTM_PAR)]
        items = [(j, h) for j in range(MLSTM_PAR) for h in heads]
        every = lambda f, *cols: [f(*args) for args in zip(*cols)]
        g_blk = [gt_ref[0, rw, :] + gb for rw in rows]
        lf_blk = [-_softplus(-g) for g in g_blk]
        cum_blk = [_dot01_l(low, x) for x in lf_blk]
        pick = lambda blk, lane: [blk[j][:, lane + h:lane + h + 1] for j, h in items]
        li_c, lf_c, cum_c = pick(g_blk, GT_M), pick(lf_blk, GT_M + M_HEADS), pick(cum_blk, GT_M + M_HEADS)
        head = lambda k: [pm_ref[0, rows[j], k * M_WIDTH + h * HEAD_DIM:k * M_WIDTH + (h + 1) * HEAD_DIM]
                          for j, h in items]
        q = every(lambda x: (x * (HEAD_DIM ** -0.5)).astype(BF16), head(0))
        k = head(1)
        v_aug = every(lambda x: jnp.concatenate([x.astype(BF16), ones_v], axis=1), head(2))
        dmat = every(lambda f, i: _dot01_l(low_ones, jnp.concatenate([f * strict_up, i * eye], axis=0)), lf_c, li_c)
        dmat = every(lambda x: jnp.where(causal, x, NEG), dmat)
        d_max = every(lambda x: jnp.max(x, axis=-1, keepdims=True), dmat)
        qk = every(lambda a, b: _dot_nt(a, b.astype(BF16)), q, k)
        g_tot = every(lambda x: x[CHUNK - 1:CHUNK, :], cum_c)
        a = every(lambda gt, cm, li: gt - cm + li, g_tot, cum_c, li_c)
        m_loc = every(lambda x: jnp.max(x, axis=0, keepdims=True), a)
        kw = every(lambda kk, x, ml: (kk * jnp.exp(x - ml)).astype(BF16), k, a, m_loc)
        c_loc = every(_dot_tn, kw, v_aug)
        c_prev = [c_scr[h] for h in heads]
        m_prev = [m_scr[h] for h in heads]
        for j in range(MLSTM_PAR):
            sel = [j * M_HEADS + h for h in heads]
            m_inter = [cum_c[i] + m_prev[h] for h, i in enumerate(sel)]
            m_t = [jnp.maximum(m_inter[h], d_max[i]) for h, i in enumerate(sel)]
            s = [(qk[i] * jnp.exp(dmat[i] - m_t[h])).astype(BF16) for h, i in enumerate(sel)]
            inter = [jnp.exp(m_inter[h] - m_t[h]) for h in heads]
            nd = [_dot(s[h], v_aug[i]) + inter[h] * _dot(q[i], c_prev[h].astype(BF16)) for h, i in enumerate(sel)]
            den = [jnp.maximum(jnp.abs(nd[h]), jnp.exp(-m_t[h])) for h in heads]
            hh = [(nd[h] / pltpu.roll(den[h], HEAD_DIM, 1))[:, :HEAD_DIM] for h in heads]
            m_new = [jnp.maximum(g_tot[i] + m_prev[h], m_loc[i]) for h, i in enumerate(sel)]
            s_old = [jnp.exp(g_tot[i] + m_prev[h] - m_new[h]) for h, i in enumerate(sel)]
            s_new = [jnp.exp(m_loc[i] - m_new[h]) for h, i in enumerate(sel)]
            for h in heads:
                o_ref[0, rows[j], h * HEAD_DIM:(h + 1) * HEAD_DIM] = hh[h]
            c_prev = [s_old[h] * c_prev[h] + s_new[h] * c_loc[i] for h, i in enumerate(sel)]
            m_prev = m_new
        for h in heads:
            c_scr[h] = c_prev[h]
            m_scr[h] = m_prev[h]
        return carry

    lax.fori_loop(0, n_chunks // MLSTM_PAR, chunk_group, 0)

    hh = o_ref[0]
    ms = _dot01_r(hh * hh, _group_ones(M_WIDTH), 2) * (1.0 / HEAD_DIM)
    o_ref[0] = hh * lax.rsqrt(ms + EPS) * nw * jax.nn.sigmoid(pm_ref[0, :, 3 * M_WIDTH:4 * M_WIDTH])


def _mlstm(pm, gt, gate_b, norm_w):
    b, t, _ = pm.shape
    tc = min(512, t)
    gb = jnp.zeros((1, LANES), F32).at[0, GT_M:GT_M + 2 * M_HEADS].set(gate_b.reshape(-1))
    kern = functools.partial(_mlstm_kernel, n_chunks=tc // CHUNK)
    return pl.pallas_call(
        kern,
        grid=(b, t // tc),
        in_specs=[pl.BlockSpec((1, tc, 4 * M_WIDTH), lambda i, j: (i, j, 0)),
                  pl.BlockSpec((1, tc, LANES), lambda i, j: (i, j, 0)),
                  pl.BlockSpec((1, LANES), lambda i, j: (0, 0)),
                  pl.BlockSpec((1, M_WIDTH), lambda i, j: (0, 0))],
        out_specs=pl.BlockSpec((1, tc, M_WIDTH), lambda i, j: (i, j, 0)),
        out_shape=jax.ShapeDtypeStruct((b, t, M_WIDTH), F32),
        scratch_shapes=[pltpu.VMEM((M_HEADS, HEAD_DIM, 2 * HEAD_DIM), F32),
                        pltpu.VMEM((M_HEADS, 1, 1), F32)],
        compiler_params=_params("parallel", "arbitrary"),
        name="mlstm",
    )(pm, gt, gb, jnp.tile(norm_w.reshape(1, HEAD_DIM), (1, M_HEADS)))


def _gdn_kernel(pg_ref, gt_ref, cw_ref, hp_ref, nw_ref, o_ref, xbuf, qkv_scr, la_scr, s_scr,
                lin_scr, add_scr, g_scr, *, tc):
    first = pl.program_id(1) == 0

    @pl.when(first)
    def _():
        s_scr[...] = jnp.zeros(s_scr.shape, F32)
        xbuf[0:8, :] = jnp.zeros((8, 3 * G_WIDTH), F32)

    @pl.when(jnp.logical_not(first))
    def _():
        xbuf[0:8, :] = xbuf[tc:tc + 8, :]

    xbuf[8:8 + tc, :] = pg_ref[0, :, 0:3 * G_WIDTH]
    acc = None
    for kk in range(CONV_K):
        term = xbuf[8 - (CONV_K - 1) + kk:8 - (CONV_K - 1) + kk + tc, :] * cw_ref[kk:kk + 1, :]
        acc = term if acc is None else acc + term
    act = acc * jax.nn.sigmoid(acc)
    gones = _group_ones(G_WIDTH)
    for part in range(2):
        xx = act[:, part * G_WIDTH:(part + 1) * G_WIDTH]
        ssq = _dot01_r(xx * xx, gones, 2)
        xx = xx * lax.rsqrt(ssq + EPS)
        if part == 0:
            xx = xx * (HEAD_DIM ** -0.5)
        qkv_scr[:, part * G_WIDTH:(part + 1) * G_WIDTH] = xx
    qkv_scr[:, 2 * G_WIDTH:3 * G_WIDTH] = act[:, 2 * G_WIDTH:3 * G_WIDTH]
    gate = gt_ref[0]
    neg_rate = -jnp.exp(hp_ref[0:1, :])
    la_scr[...] = neg_rate * _softplus(gate + hp_ref[1:2, :])

    r, c = _chunk_masks()
    low = _ind(r >= c, BF16)
    strict_low = r > c
    incl_low = r >= c
    strict_up = _ind(r > c)
    eye = _ind(r == c)
    base_mask = _ind(((r >> 2) == (c >> 2)) & (r > c))
    merge_masks = [_ind((((r >> s) & 1) == 1) & ((c >> s) == (r >> s) - 1)) for s in (2, 3, 4, 5)]
    nw = nw_ref[...]

    def prepare(cp, carry):
        items = [(cp * GDN_PAR + j, h) for j in range(GDN_PAR) for h in range(G_HEADS)]
        every = lambda f, *cols: [f(*args) for args in zip(*cols)]
        rows = [pl.ds(pl.multiple_of((cp * GDN_PAR + j) * CHUNK, CHUNK), CHUNK) for j in range(GDN_PAR)]
        la_blk = [la_scr[rw, :] for rw in rows]
        gam_blk = [_dot01_l(low, x) for x in la_blk]
        beta_blk = [jax.nn.sigmoid(gt_ref[0, rw, :]) for rw in rows]
        pick = lambda blk, lane: [blk[i // G_HEADS][:, lane + h:lane + h + 1] for i, (_, h) in enumerate(items)]
        la_c, gam_c, beta_c = pick(la_blk, GT_G), pick(gam_blk, GT_G), pick(beta_blk, GT_G + G_HEADS)
        head = lambda k: [qkv_scr[rows[i // G_HEADS], k * G_WIDTH + h * HEAD_DIM:k * G_WIDTH + (h + 1) * HEAD_DIM]
                          for i, (_, h) in enumerate(items)]
        q, k, v = head(0), head(1), head(2)
        diff = every(lambda x: _dot01_l(low, x * strict_up), la_c)
        dec_strict = every(lambda x: jnp.exp(jnp.where(strict_low, x, NEG)), diff)
        dec_incl = every(lambda x: jnp.exp(jnp.where(incl_low, x, NEG)), diff)
        kb = every(lambda x: x.astype(BF16), k)
        qk_kk = every(lambda a, b: _dot_nt(jnp.concatenate([a.astype(BF16), b], axis=0), b), q, kb)
        amat = every(lambda b, x, dcy: b * x[CHUNK:] * dcy, beta_c, qk_kk, dec_strict)
        n0 = every(lambda a: (-(a * base_mask)).astype(BF16), amat)
        n0sq = every(lambda n: _dot(n, n).astype(BF16), n0)
        inv = every(lambda n: eye + n.astype(F32), n0)
        inv = every(lambda t, n2: t + _dot(t.astype(BF16), n2), inv, n0sq)
        for mask in merge_masks:
            invb = every(lambda t: t.astype(BF16), inv)
            half = every(lambda tb, a: _dot(tb, (a * mask).astype(BF16)).astype(BF16), invb, amat)
            inv = every(lambda t, hf, tb: t - _dot(hf, tb), inv, half, invb)
        e_gam = every(jnp.exp, gam_c)
        rhs = every(lambda b, e, vv, kk: jnp.concatenate([b * vv, (b * e) * kk], axis=1).astype(BF16),
                    beta_c, e_gam, v, k)
        sol = every(lambda t, x: _dot(t.astype(BF16), x).astype(BF16), inv, rhs)
        gam_last = every(lambda g: g[CHUNK - 1:CHUNK, :], gam_c)
        k_d = every(lambda kk, gl, g: (kk * jnp.exp(gl - g)).astype(BF16), k, gam_last, gam_c)
        p_sol = every(lambda x, dcy, sl: _dot((x[:CHUNK] * dcy).astype(BF16), sl), qk_kk, dec_incl, sol)
        kd_sol = every(_dot_tn, k_d, sol)
        for i, (ci, h) in enumerate(items):
            lin_scr[ci, h] = jnp.concatenate([kd_sol[i][:, HEAD_DIM:], q[i] * e_gam[i] - p_sol[i][:, HEAD_DIM:]],
                                             axis=0).astype(BF16)
            add_scr[ci, h] = jnp.concatenate([kd_sol[i][:, :HEAD_DIM], p_sol[i][:, :HEAD_DIM]], axis=0)
            g_scr[ci, h] = jnp.broadcast_to(jnp.exp(gam_last[i]), (1, HEAD_DIM))
        return carry

    lax.fori_loop(0, tc // (CHUNK * GDN_PAR), prepare, 0)

    heads = range(G_HEADS)

    def advance(ci, s_prev):
        r0 = pl.multiple_of(ci * CHUNK, CHUNK)
        prod = [_dot(lin_scr[ci, h], s_prev[h].astype(BF16)) for h in heads]
        add = [add_scr[ci, h] for h in heads]
        for h in heads:
            o_ref[0, pl.ds(r0, CHUNK), h * HEAD_DIM:(h + 1) * HEAD_DIM] = prod[h][CHUNK:] + add[h][CHUNK:]
        return tuple(g_scr[ci, h] * s_prev[h] - prod[h][:CHUNK] + add[h][:CHUNK] for h in heads)

    s_last = lax.fori_loop(0, tc // CHUNK, advance, tuple(s_scr[h] for h in heads))
    for h in heads:
        s_scr[h] = s_last[h]

    o = o_ref[0]
    ms = _dot01_r(o * o, gones, 2) * (1.0 / HEAD_DIM)
    z = pg_ref[0, :, 3 * G_WIDTH:4 * G_WIDTH]
    o_ref[0] = o * lax.rsqrt(ms + EPS) * nw * (z * jax.nn.sigmoid(z))


def _gdn(pg, gt, conv_w, a_log, dt_bias, norm_w):
    b, t, _ = pg.shape
    tc = min(512, t)
    hp = jnp.zeros((2, LANES), F32)
    hp = hp.at[0, GT_G:GT_G + G_HEADS].set(a_log).at[1, GT_G:GT_G + G_HEADS].set(dt_bias)
    kern = functools.partial(_gdn_kernel, tc=tc)
    return pl.pallas_call(
        kern,
        grid=(b, t // tc),
        in_specs=[pl.BlockSpec((1, tc, 4 * G_WIDTH), lambda i, j: (i, j, 0)),
                  pl.BlockSpec((1, tc, LANES), lambda i, j: (i, j, 0)),
                  pl.BlockSpec((CONV_K, 3 * G_WIDTH), lambda i, j: (0, 0)),
                  pl.BlockSpec((2, LANES), lambda i, j: (0, 0)),
                  pl.BlockSpec((1, G_WIDTH), lambda i, j: (0, 0))],
        out_specs=pl.BlockSpec((1, tc, G_WIDTH), lambda i, j: (i, j, 0)),
        out_shape=jax.ShapeDtypeStruct((b, t, G_WIDTH), F32),
        scratch_shapes=[pltpu.VMEM((tc + 8, 3 * G_WIDTH), F32),
                        pltpu.VMEM((tc, 3 * G_WIDTH), F32),
                        pltpu.VMEM((tc, LANES), F32),
                        pltpu.VMEM((G_HEADS, HEAD_DIM, HEAD_DIM), F32),
                        pltpu.VMEM((tc // CHUNK, G_HEADS, 2 * CHUNK, HEAD_DIM), BF16),
                        pltpu.VMEM((tc // CHUNK, G_HEADS, 2 * CHUNK, HEAD_DIM), F32),
                        pltpu.VMEM((tc // CHUNK, G_HEADS, 1, HEAD_DIM), F32)],
        compiler_params=_params("parallel", "arbitrary"),
        name="gdn",
    )(pg, gt, conv_w.astype(F32), hp, jnp.tile(norm_w.reshape(1, HEAD_DIM), (1, G_HEADS)))


def _rope_tables(pos):
    half = HEAD_DIM // 2
    inv_freq = jnp.power(ROPE_THETA, -jnp.arange(half, dtype=F32) / half)
    ang = pos.astype(F32)[:, None] * inv_freq[None, :]
    cos, sin = jnp.cos(ang), jnp.sin(ang)
    cosf = jnp.tile(cos, (1, LANES // half))
    sinf = jnp.tile(jnp.concatenate([-sin, sin], axis=1), (1, LANES // HEAD_DIM))
    return cosf, sinf


KS_AUG = LANES + HEAD_DIM


SLC_TILE = 512
WIN_SPAN = Q_BLOCK + WINDOW


V_ROWS = HEAD_DIM + 16
FORCED_BLOCKS = 3
LOG2E = 1.4426950408889634


def _nsa_prep_t_kernel(pq_ref, pkv_ref, cos_ref, sin_ref, nw_ref, q_ref, ks_ref, vs_ref, kw_ref, vw_ref, xc_ref,
                       cmp_scr):
    cosf, sinf = cos_ref[...], sin_ref[...]
    gones = _group_ones(LANES)
    tm = cosf.shape[0]

    for s in range(2):
        cmp_scr[s] = pkv_ref[0, :, s * KV_WIDTH:(s + 1) * KV_WIDTH]
    for p in range(CMP_STRIDE):
        for s in range(2):
            grp = cmp_scr[s, pl.ds(p, tm // CMP_STRIDE, stride=CMP_STRIDE), :]
            for h in range(N_KV_HEADS):
                xc_ref[0, s, h, :, p * HEAD_DIM:(p + 1) * HEAD_DIM] = grp[:, h * HEAD_DIM:(h + 1) * HEAD_DIM]

    def norm_rope(x, w):
        ssq = _dot01_r(x * x, gones, 2)
        y = x * lax.rsqrt(ssq * (1.0 / HEAD_DIM) + EPS) * w
        return _rope_lanes(y, cosf, sinf)

    for s in range(N_WIDTH // LANES):
        qs = norm_rope(pq_ref[0, :, s * LANES:(s + 1) * LANES], nw_ref[0:1, :]) * (HEAD_DIM ** -0.5 * LOG2E)
        qs_t = qs.T.astype(BF16)
        for half in range(2):
            hkv, g = divmod(2 * s + half, N_GROUP)
            for blk in range(tm // Q_BLOCK):
                col = (blk * N_GROUP + g) * Q_BLOCK
                q_ref[0, hkv, :, col:col + Q_BLOCK] = qs_t[half * HEAD_DIM:(half + 1) * HEAD_DIM,
                                                           blk * Q_BLOCK:(blk + 1) * Q_BLOCK]
    kv = lambda i: pkv_ref[0, :, i * KV_WIDTH:(i + 1) * KV_WIDTH]
    tok = pl.program_id(1) * tm + _iota((tm, LANES), 0)
    onehot = _ind((tok >> 6) == _iota((tm, LANES), 1), BF16)
    k_slc = norm_rope(kv(2), nw_ref[2:3, :])
    k_win = norm_rope(kv(4), nw_ref[3:4, :])
    ones = jnp.ones((V_ROWS - HEAD_DIM, tm), BF16)
    for h in range(N_KV_HEADS):
        ks_ref[0, h, :, 0:LANES] = onehot
        ks_ref[0, h, :, LANES:LANES + HEAD_DIM] = k_slc[:, h * HEAD_DIM:(h + 1) * HEAD_DIM].astype(BF16)
        kw_ref[0, h] = k_win[:, h * HEAD_DIM:(h + 1) * HEAD_DIM].astype(BF16)
    for ref, slab in ((vs_ref, kv(3)), (vw_ref, kv(5))):
        v_t = slab.T.astype(BF16)
        for h in range(N_KV_HEADS):
            ref[0, h, 0:HEAD_DIM, :] = v_t[h * HEAD_DIM:(h + 1) * HEAD_DIM]
            ref[0, h, HEAD_DIM:V_ROWS, :] = ones


def _nsa_prep_t(pq, pkv, qk_norm):
    b, t, _ = pq.shape
    assert t // SLC_BLOCK <= LANES
    tm = min(512, t)
    flat = CMP_STRIDE * HEAD_DIM
    cosf, sinf = _rope_tables(jnp.arange(t, dtype=jnp.int32))
    nw = jnp.tile(qk_norm, (1, LANES // HEAD_DIM))
    rows = lambda w: pl.BlockSpec((1, N_KV_HEADS, tm, w), lambda i, j: (i, 0, j, 0))
    cols = lambda r, w: pl.BlockSpec((1, N_KV_HEADS, r, w), lambda i, j: (i, 0, 0, j))
    shp = lambda r, w: jax.ShapeDtypeStruct((b, N_KV_HEADS, r, w), BF16)
    return pl.pallas_call(
        _nsa_prep_t_kernel,
        grid=(b, t // tm),
        in_specs=[pl.BlockSpec((1, tm, N_WIDTH), lambda i, j: (i, j, 0)),
                  pl.BlockSpec((1, tm, 6 * KV_WIDTH), lambda i, j: (i, j, 0)),
                  pl.BlockSpec((tm, LANES), lambda i, j: (j, 0)),
                  pl.BlockSpec((tm, LANES), lambda i, j: (j, 0)),
                  pl.BlockSpec((4, LANES), lambda i, j: (0, 0))],
        out_specs=[cols(HEAD_DIM, N_GROUP * tm), rows(KS_AUG), cols(V_ROWS, tm), rows(HEAD_DIM), cols(V_ROWS, tm),
                   pl.BlockSpec((1, 2, N_KV_HEADS, tm // CMP_STRIDE, flat), lambda i, j: (i, 0, 0, j, 0))],
        out_shape=[shp(HEAD_DIM, N_GROUP * t), shp(t, KS_AUG), shp(V_ROWS, t), shp(t, HEAD_DIM), shp(V_ROWS, t),
                   jax.ShapeDtypeStruct((b, 2, N_KV_HEADS, t // CMP_STRIDE, flat), F32)],
        scratch_shapes=[pltpu.VMEM((2, tm, KV_WIDTH), F32)],
        compiler_params=_params("parallel", "parallel"),
        name="nsa_prep",
    )(pq, pkv, cosf, sinf, nw)


def _compress_t_kernel(x_ref, pos_ref, w1_ref, w2_ref, cos_ref, sin_ref, nw_ref, kc_ref, vc_ref):
    x = x_ref[0, 0, 0]
    half = CMP_STRIDE * HEAD_DIM
    a = _dot((x + pos_ref[0, 0:1, :]).astype(BF16), w1_ref[0, 0:half, :])
    bmat = _dot((x + pos_ref[0, 1:2, :]).astype(BF16), w1_ref[0, half:2 * half, :])
    hid = a + jnp.concatenate([bmat[1:, :], jnp.zeros((1, CMP_HIDDEN), F32)], axis=0)
    hid = hid * jax.nn.sigmoid(hid)
    out = _dot(hid.astype(BF16), w2_ref[0])

    @pl.when(pl.program_id(2) == 0)
    def _():
        ssq = jnp.sum(out * out, axis=-1, keepdims=True)
        normed = out * lax.rsqrt(ssq * (1.0 / HEAD_DIM) + EPS) * nw_ref[...]
        kc_ref[0, 0] = _rope_lanes(normed, cos_ref[...], sin_ref[...]).astype(BF16)

    @pl.when(pl.program_id(2) == 1)
    def _():
        vc_ref[0, 0] = out.T[0:HEAD_DIM].astype(BF16)


def _nsa_compress_t(x, cmp_pos, cmp_w1, cmp_w2, kc_norm):
    b, _, _, n16, flat = x.shape
    pos = cmp_pos.reshape(2, 2, flat)
    w1 = cmp_w1.astype(BF16)
    w2 = jnp.pad(cmp_w2, ((0, 0), (0, 0), (0, LANES - HEAD_DIM))).astype(BF16)
    cosf, sinf = _rope_tables(jnp.arange(n16, dtype=jnp.int32) * CMP_STRIDE + CMP_BLOCK - 1)
    nw = jnp.pad(kc_norm.reshape(1, HEAD_DIM), ((0, 0), (0, LANES - HEAD_DIM)))
    return pl.pallas_call(
        _compress_t_kernel,
        grid=(b, N_KV_HEADS, 2),
        in_specs=[pl.BlockSpec((1, 1, 1, n16, flat), lambda i, h, s: (i, s, h, 0, 0)),
                  pl.BlockSpec((1, 2, flat), lambda i, h, s: (s, 0, 0)),
                  pl.BlockSpec((1, 2 * flat, CMP_HIDDEN), lambda i, h, s: (s, 0, 0)),
                  pl.BlockSpec((1, CMP_HIDDEN, LANES), lambda i, h, s: (s, 0, 0)),
                  pl.BlockSpec((n16, LANES), lambda i, h, s: (0, 0)),
                  pl.BlockSpec((n16, LANES), lambda i, h, s: (0, 0)),
                  pl.BlockSpec((1, LANES), lambda i, h, s: (0, 0))],
        out_specs=[pl.BlockSpec((1, 1, n16, LANES), lambda i, h, s: (i, h, 0, 0)),
                   pl.BlockSpec((1, 1, HEAD_DIM, n16), lambda i, h, s: (i, h, 0, 0))],
        out_shape=[jax.ShapeDtypeStruct((b, N_KV_HEADS, n16, LANES), BF16),
                   jax.ShapeDtypeStruct((b, N_KV_HEADS, HEAD_DIM, n16), BF16)],
        compiler_params=_params("parallel", "parallel", "arbitrary"),
        name="nsa_compress",
    )(x, pos, w1, w2, cosf, sinf, nw)


def _col_max(x):
    return jnp.max(x, axis=0, keepdims=True)


def _nsa_attn_t_kernel(q_ref, kc_ref, vc_ref, ks_ref, vs_ref, kw_ref, vw_ref, gt_ref, o_ref,
                       sa_scr, sb_scr, pa_scr, pb_scr, *, n_slc):
    qi = pl.program_id(1)
    start = qi * Q_BLOCK
    cols = N_GROUP * Q_BLOCK
    kvh = range(N_KV_HEADS)
    q_t = [q_ref[0, h] for h in kvh]
    n16 = kc_ref.shape[2]

    def per_head(x):
        return jnp.concatenate([x] * N_GROUP, axis=1)

    def finish(acc):
        return acc[0:HEAD_DIM] * (1.0 / acc[HEAD_DIM:HEAD_DIM + 1])

    def probs(s, m):
        return jnp.exp2(s - m).astype(BF16)

    gates_t = jax.nn.sigmoid(gt_ref[0]).T

    tq_q = start + _iota((n16, Q_BLOCK), 1)
    valid_c = (_iota((n16, Q_BLOCK), 0) * CMP_STRIDE + (CMP_BLOCK - 1)) <= tq_q
    bias_c = per_head(jnp.where(valid_c, 0.0, NEG))
    any_c = per_head(_ind(start + _iota((1, Q_BLOCK), 1) >= CMP_BLOCK - 1))
    s_c = [_dot(kc_ref[0, h][:, :HEAD_DIM], q_t[h]) + bias_c for h in kvh]
    e_c = [jnp.exp2(x - _col_max(x)) for x in s_c]
    p_c = [x * (any_c / jnp.sum(x, axis=0, keepdims=True)) for x in e_c]
    o_c = [_dot(vc_ref[0, h], p_c[h].astype(BF16)) for h in kvh]

    p_sum = [x[:, 0:Q_BLOCK] for x in p_c]
    for g in range(1, N_GROUP):
        p_sum = [p_sum[h] + p_c[h][:, g * Q_BLOCK:(g + 1) * Q_BLOCK] for h in kvh]
    sb, cn = _iota((n_slc, n16), 0), _iota((n_slc, n16), 1)
    c_lo, s_lo = cn * CMP_STRIDE, sb * SLC_BLOCK
    ov = jnp.maximum(jnp.minimum(c_lo + CMP_BLOCK, s_lo + SLC_BLOCK) - jnp.maximum(c_lo, s_lo), 0)
    ov = jnp.where(cn < n16 - 1, ov, 0)
    overlap_t = (ov.astype(F32) * (1.0 / CMP_BLOCK)).astype(BF16)
    imp = [_dot01_l(overlap_t, x, 2) for x in p_sum]
    tq_i = start + _iota((n_slc, Q_BLOCK), 1)
    blk = _iota((n_slc, Q_BLOCK), 0)
    cur = tq_i >> 6
    forced = (blk == 0) | (blk == cur) | (blk == cur - 1)
    imp = [jnp.where(blk * SLC_BLOCK <= tq_i, jnp.where(forced, -3e38, x), NEG) for x in imp]
    blk_f = blk.astype(F32)
    sel_t = [_ind(forced) for _ in kvh]
    for _ in range(N_SELECTED - FORCED_BLOCKS):
        top = [_col_max(x) for x in imp]
        first = [jnp.min(jnp.where(imp[h] == top[h], blk_f, float(n_slc)), axis=0, keepdims=True) for h in kvh]
        hit = [blk_f == first[h] for h in kvh]
        sel_t = [jnp.where(hit[h], 1.0, sel_t[h]) for h in kvh]
        imp = [jnp.where(hit[h], -3e38, imp[h]) for h in kvh]

    w0 = pl.multiple_of(jnp.maximum(start - WINDOW, 0), Q_BLOCK)
    dist = start + _iota((WIN_SPAN, Q_BLOCK), 1) - (w0 + _iota((WIN_SPAN, Q_BLOCK), 0))
    bias_w = per_head(jnp.where((dist >= 0) & (dist < WINDOW), 0.0, NEG))
    s_w = [_dot(kw_ref[0, h, pl.ds(w0, WIN_SPAN), :], q_t[h]) + bias_w for h in kvh]
    p_w = [probs(x, _col_max(x)) for x in s_w]
    o_w = [finish(_dot(vw_ref[0, h, :, pl.ds(w0, WIN_SPAN)], p_w[h])) for h in kvh]

    own = pl.multiple_of(start, Q_BLOCK)
    bias_d = per_head(jnp.where(_iota((Q_BLOCK, Q_BLOCK), 0) <= _iota((Q_BLOCK, Q_BLOCK), 1), 0.0, NEG))
    s_d = [_dot(ks_ref[0, h, pl.ds(own, Q_BLOCK), LANES:LANES + HEAD_DIM], q_t[h]) + bias_d for h in kvh]
    m_d = [_col_max(x) for x in s_d]
    acc_d = [_dot(vs_ref[0, h, :, pl.ds(own, Q_BLOCK)], probs(s_d[h], m_d[h])) for h in kvh]

    def augmented(h):
        bias_t = (jnp.where(blk < 2 * qi, sel_t[h], 0.0) - 1.0) * BIG
        if n_slc < LANES:
            bias_t = jnp.concatenate([bias_t, jnp.zeros((LANES - n_slc, Q_BLOCK), F32)], axis=0)
        return jnp.concatenate([per_head(bias_t.astype(BF16)), q_t[h]], axis=0)

    q_aug = [augmented(h) for h in kvh]
    n_tiles = (start + SLC_TILE - 1) // SLC_TILE

    def scores(h, kt):
        k0 = pl.multiple_of(kt * SLC_TILE, SLC_TILE)
        return _dot(ks_ref[0, h, pl.ds(k0, SLC_TILE), :], q_aug[h])

    def weighted(h, p, kt):
        k0 = pl.multiple_of(kt * SLC_TILE, SLC_TILE)
        return _dot(vs_ref[0, h, :, pl.ds(k0, SLC_TILE)], p)

    top_tile = ks_ref.shape[2] // SLC_TILE - 1

    def half_step(kt, s_in, s_out, p_in, p_out, a_prev, m, acc):
        for h in kvh:
            s_out[h] = scores(h, jnp.minimum(kt + 1, top_tile))
        acc = [a_prev[h] * acc[h] + weighted(h, p_in[h], jnp.maximum(kt - 1, 0)) for h in kvh]
        s = [s_in[h] for h in kvh]
        m_new = [jnp.maximum(m[h], _col_max(s[h])) for h in kvh]
        for h in kvh:
            p_out[h] = probs(s[h], m_new[h])
        return [jnp.exp2(m[h] - m_new[h]) for h in kvh], m_new, acc

    def slc_pair(j, carry):
        carry = half_step(2 * j, sa_scr, sb_scr, pb_scr, pa_scr, *carry)
        return half_step(2 * j + 1, sb_scr, sa_scr, pa_scr, pb_scr, *carry)

    for h in kvh:
        sa_scr[h] = scores(h, 0)
        pb_scr[h] = jnp.zeros((SLC_TILE, cols), BF16)
    n_pairs = (n_tiles + 1) // 2
    init = ([jnp.ones((1, cols), F32) for _ in kvh], m_d, acc_d)
    a_last, _, acc_s = lax.fori_loop(0, n_pairs, slc_pair, init)
    o_s = [finish(a_last[h] * acc_s[h] + weighted(h, pb_scr[h], jnp.maximum(2 * n_pairs - 1, 0))) for h in kvh]

    def gate_row(h, branch):
        c0 = GT_N + branch * N_HEADS + h * N_GROUP
        return jnp.concatenate([gates_t[c0 + g:c0 + g + 1] for g in range(N_GROUP)], axis=1)

    for h in kvh:
        y_t = gate_row(h, 0) * o_c[h] + gate_row(h, 1) * o_s[h] + gate_row(h, 2) * o_w[h]
        y = jnp.concatenate([y_t[:, g * Q_BLOCK:(g + 1) * Q_BLOCK] for g in range(N_GROUP)], axis=0)
        o_ref[0, :, h * N_GROUP * HEAD_DIM:(h + 1) * N_GROUP * HEAD_DIM] = y.T


def _nsa_attention_t(q_t, kc, vc_t, ks, vs_t, kw, vw_t, gt):
    b, _, t, _ = ks.shape
    n16 = kc.shape[2]
    n_slc = t // SLC_BLOCK
    cols = N_GROUP * Q_BLOCK
    assert (t // SLC_TILE) % 2 == 0
    assert n_slc >= N_SELECTED
    whole = lambda r, w: pl.BlockSpec((1, N_KV_HEADS, r, w), lambda i, j: (i, 0, 0, 0))
    slot = lambda dtype: pltpu.VMEM((N_KV_HEADS, SLC_TILE, cols), dtype)
    kern = functools.partial(_nsa_attn_t_kernel, n_slc=n_slc)
    return pl.pallas_call(
        kern,
        grid=(b, t // Q_BLOCK),
        in_specs=[pl.BlockSpec((1, N_KV_HEADS, HEAD_DIM, cols), lambda i, j: (i, 0, 0, j)),
                  whole(n16, LANES), whole(HEAD_DIM, n16),
                  whole(t, KS_AUG), whole(V_ROWS, t), whole(t, HEAD_DIM), whole(V_ROWS, t),
                  pl.BlockSpec((1, Q_BLOCK, LANES), lambda i, j: (i, j, 0))],
        out_specs=pl.BlockSpec((1, Q_BLOCK, N_WIDTH), lambda i, j: (i, j, 0)),
        out_shape=jax.ShapeDtypeStruct((b, t, N_WIDTH), F32),
        scratch_shapes=[slot(F32), slot(F32), slot(BF16), slot(BF16)],
        compiler_params=_params("parallel", "arbitrary"),
        name="nsa_attention",
    )(q_t, kc, vc_t, ks, vs_t, kw, vw_t, gt)


def _nsa(pq, pkv, gt, qk_norm, cmp_pos, cmp_w1, cmp_w2):
    q_t, ks, vs_t, kw, vw_t, x_cmp = _nsa_prep_t(pq, pkv, qk_norm)
    kc, vc_t = _nsa_compress_t(x_cmp, cmp_pos, cmp_w1, cmp_w2, qk_norm[1])
    return _nsa_attention_t(q_t, kc, vc_t, ks, vs_t, kw, vw_t, gt)


def _mixer_heads(x, mod, nw, w_in, m_gate_b, m_norm_w, g_conv_w, g_a_log, g_dt_bias, g_norm_w,
                 n_qk_norm, n_cmp_pos, n_cmp_w1, n_cmp_w2):
    pm, pg, pq, pkv, gt = _in_proj(x, mod, nw, _permute_w_in(w_in))
    y_m = _mlstm(pm, gt, m_gate_b, m_norm_w)
    y_g = _gdn(pg, gt, g_conv_w, g_a_log, g_dt_bias, g_norm_w)
    y_n = _nsa(pq, pkv, gt, n_qk_norm, n_cmp_pos, n_cmp_w1, n_cmp_w2)
    return y_m, y_g, y_n


def kernel(x, c, ada_w, ada_b, norm_w, ffn_w_up, ffn_w_down, w_in, w_out, mlstm_gate_b, mlstm_norm_w,
           gdn_conv_w, gdn_a_log, gdn_dt_bias, gdn_norm_w, nsa_qk_norm, nsa_cmp_pos, nsa_cmp_w1, nsa_cmp_w2):
    depth = ada_w.shape[0]
    b = x.shape[0]
    mods = _modulation(c, ada_w, ada_b).reshape(depth, b, 9, D_MODEL)
    for l in range(depth):
        mod = mods[l]
        x = _ffn(x, mod, norm_w[l, 0], ffn_w_up[l, 0].astype(BF16), ffn_w_down[l, 0].astype(BF16), 0)
        ys = _mixer_heads(x, mod, norm_w[l, 1], w_in[l], mlstm_gate_b[l], mlstm_norm_w[l],
                          gdn_conv_w[l], gdn_a_log[l], gdn_dt_bias[l], gdn_norm_w[l],
                          nsa_qk_norm[l], nsa_cmp_pos[l], nsa_cmp_w1[l], nsa_cmp_w2[l])
        x = _ffn(x, mod, norm_w[l, 2], ffn_w_up[l, 1].astype(BF16), ffn_w_down[l, 1].astype(BF16), 6,
                 mixer=(*ys, w_out[l]))
    return x
```
